```python
import math
import jax
import jax.numpy as jnp
from jax import lax
import numpy as np

D_MODEL = 1024
BATCH = 2
SEQ = 8192
DEPTH = 2

GRID_W = 64
CTX_LEN = 256
HEAD_DIM = 64
D_MIX = D_MODEL
A_HEADS = D_MIX // (2 * HEAD_DIM)
A_DIM = A_HEADS * HEAD_DIM
B_Q_HEADS = D_MIX // (2 * HEAD_DIM)
B_KV_HEADS = max(1, B_Q_HEADS // 4)
B_Q_DIM = B_Q_HEADS * HEAD_DIM
B_KV_DIM = B_KV_HEADS * HEAD_DIM
C_HEADS = D_MIX // HEAD_DIM
C_DIM = C_HEADS * HEAD_DIM
EVEN_SPLITS = (3 * A_DIM, A_DIM, A_HEADS, A_HEADS, A_HEADS, A_HEADS, B_Q_DIM, B_KV_DIM, B_KV_DIM)
IN_EVEN = sum(EVEN_SPLITS)
IN_ODD = 3 * C_DIM
SHORT_CONV = 3
DELTA_CHUNK = 64
Q_BLOCK = 128
NA_ROWS_MAX = 8
NA_COLS = 16
ROPE_THETA = 10000.0
N_EXPERTS = 16
EXPERT_FF = D_MODEL
EC_CAPACITY = 2
N_EVEN = (DEPTH + 1) // 2
N_ODD = DEPTH // 2
EPS = 1e-6
F32 = jnp.float32

kernel_name = 'hybrid_deltanet_gqa_natten_ecmoe_dit'


def rms_norm(x, g):
    xf = x.astype(F32)
    y = xf * lax.rsqrt(jnp.mean(xf * xf, axis=-1, keepdims=True) + EPS)
    return (y * g.astype(F32)).astype(x.dtype)


def l2_norm(x):
    return x * lax.rsqrt(jnp.sum(x * x, axis=-1, keepdims=True) + EPS)


def split_cols(p, sizes):
    return jnp.split(p, np.cumsum(sizes)[:-1].tolist(), axis=-1)


def modulate(x, g, shift, scale):
    return rms_norm(x, g) * (1 + scale) + shift


def axial_rope(n, dh):
    t = jnp.arange(n)
    rows = (t // GRID_W).astype(F32)
    cols = (t % GRID_W).astype(F32)
    n_freq = dh // 4
    inv = ROPE_THETA ** (-jnp.arange(n_freq, dtype=F32) / n_freq)
    ang = jnp.concatenate([rows[:, None] * inv, cols[:, None] * inv], axis=-1)
    return jnp.cos(ang), jnp.sin(ang)


def apply_rope(x, cos, sin):
    xf = x.astype(F32)
    x1, x2 = jnp.split(xf, 2, axis=-1)
    c, s = cos[:, None, :], sin[:, None, :]
    return jnp.concatenate([x1 * c - x2 * s, x1 * s + x2 * c], axis=-1).astype(x.dtype)


def short_conv(x, w):
    pad = w.shape[0] // 2
    return lax.conv_general_dilated(x, w[:, None, :].astype(x.dtype), (1,), [(pad, pad)],
                                    dimension_numbers=('NWC', 'WIO', 'NWC'),
                                    feature_group_count=x.shape[-1])


def deltanet_prep(qkv, a_f, b_f, a_b, b_b, conv_w, a_log, dt_bias):
    bn, n, _ = qkv.shape
    qkv = jax.nn.silu(short_conv(qkv, conv_w)).astype(F32)
    q, k, v = jnp.split(qkv, 3, axis=-1)
    q = l2_norm(q.reshape(bn, n, A_HEADS, HEAD_DIM)) * HEAD_DIM ** -0.5
    k = l2_norm(k.reshape(bn, n, A_HEADS, HEAD_DIM))
    v = v.reshape(bn, n, A_HEADS, HEAD_DIM)
    dirs = []
    for d, (a_in, b_in) in enumerate(((a_f, b_f), (a_b, b_b))):
        log_a = -jnp.exp(a_log[d].astype(F32)) * jax.nn.softplus(a_in.astype(F32) + dt_bias[d].astype(F32))
        beta = jax.nn.sigmoid(b_in.astype(F32))
        dirs.append((log_a, beta))
    return q, k, v, dirs


def delta_rule_chunked(q, k, v, log_a, beta, s0):
    bn, n, h, dk = q.shape
    dv = v.shape[-1]
    L = DELTA_CHUNK
    nc = n // L
    chunk = lambda t: t.reshape(bn, nc, L, h, t.shape[-1]).transpose(1, 0, 3, 2, 4)
    qc, kc, vc = chunk(q), chunk(k), chunk(v)
    g = jnp.cumsum(log_a.reshape(bn, nc, L, h).transpose(1, 0, 3, 2), axis=-1)
    bc = beta.reshape(bn, nc, L, h).transpose(1, 0, 3, 2)
    tri = np.tril(np.ones((L, L), dtype=bool))
    strict = np.tril(np.ones((L, L), dtype=bool), -1)
    gam = jnp.exp(jnp.where(tri, g[..., :, None] - g[..., None, :], -jnp.inf))
    kb = kc * bc[..., None]
    m = jnp.where(strict, jnp.einsum('nbhid,nbhjd->nbhij', kb, kc) * gam, 0.0)
    t_mat = m + jnp.eye(L, dtype=m.dtype)
    rhs = jnp.concatenate([vc * bc[..., None], kb * jnp.exp(g)[..., None]], axis=-1)
    sol = lax.linalg.triangular_solve(t_mat, rhs, left_side=True, lower=True, unit_diagonal=True)
    u, w = sol[..., :dv], sol[..., dv:]
    a_qk = jnp.einsum('nbhid,nbhjd->nbhij', qc, kc) * gam

    def step(s, inp):
        q_i, k_i, u_i, w_i, g_i, a_i = inp
        v_new = u_i - jnp.einsum('bhld,bhdv->bhlv', w_i, s)
        o_i = (jnp.einsum('bhld,bhdv->bhlv', q_i * jnp.exp(g_i)[..., None], s)
               + jnp.einsum('bhij,bhjv->bhiv', a_i, v_new))
        g_last = g_i[..., -1:]
        s = (s * jnp.exp(g_last)[..., None]
             + jnp.einsum('bhld,bhlv->bhdv', k_i * jnp.exp(g_last - g_i)[..., None], v_new))
        return s, o_i

    s_fin, o = lax.scan(step, s0, (qc, kc, u, w, g, a_qk))
    return s_fin, o.transpose(1, 0, 3, 2, 4).reshape(bn, n, h, dv)


def bidir_deltanet(lat, ctx_s):
    q_l, k_l, v_l, d_l = lat
    q_c, k_c, v_c, d_c = ctx_s
    s0 = jnp.zeros((q_l.shape[0], A_HEADS, HEAD_DIM, HEAD_DIM), F32)
    o_lat = jnp.zeros_like(v_l)
    o_ctx = jnp.zeros_like(v_c)
    for d in range(2):
        f = (lambda t: jnp.flip(t, axis=1)) if d == 1 else (lambda t: t)
        s_ctx, oc = delta_rule_chunked(f(q_c), f(k_c), f(v_c), f(d_c[d][0]), f(d_c[d][1]), s0)
        _, ol = delta_rule_chunked(f(q_l), f(k_l), f(v_l), f(d_l[d][0]), f(d_l[d][1]), s_ctx)
        o_lat = o_lat + f(ol)
        o_ctx = o_ctx + f(oc)
    return o_lat, o_ctx


def gated_out(o, z, g):
    bn, n, _ = z.shape
    y = rms_norm(o.astype(z.dtype), g) * jax.nn.silu(z.reshape(bn, n, A_HEADS, HEAD_DIM))
    return y.reshape(bn, n, A_DIM)


def attend_gqa(q, keys, vals):
    s = jnp.einsum('bqhgd,bkhd->bhgqk', q, keys).astype(F32)
    p = jax.nn.softmax(s, axis=-1).astype(vals.dtype)
    return jnp.einsum('bhgqk,bkhd->bqhgd', p, vals)


def gqa_latent(q, k, v, k_ctx, v_ctx):
    bn, n, hq, dh = q.shape
    hkv = k.shape[2]
    keys = jnp.concatenate([k_ctx, k], axis=1)
    vals = jnp.concatenate([v_ctx, v], axis=1)
    qb = q.reshape(bn, n // Q_BLOCK, Q_BLOCK, hkv, hq // hkv, dh).transpose(1, 0, 2, 3, 4, 5)
    o = lax.map(lambda q_blk: attend_gqa(q_blk, keys, vals), qb)
    return o.transpose(1, 0, 2, 3, 4, 5).reshape(bn, n, hq * dh)


def gqa_context(q, k, v):
    bn, n, hq, dh = q.shape
    hkv = k.shape[2]
    o = attend_gqa(q.reshape(bn, n, hkv, hq // hkv, dh), k, v)
    return o.reshape(bn, n, hq * dh)


def neighbourhood_latent(q, k, v, k_ctx, v_ctx, rpb):
    bn, n, h, dh = q.shape
    n_ctx = k_ctx.shape[1]
    rows = n // GRID_W
    wr = min(NA_ROWS_MAX, rows)
    r = np.arange(rows)
    row_start = np.clip(r - wr // 2, 0, rows - wr)
    row_off = row_start[:, None] + np.arange(wr)[None, :] - r[:, None]
    col = np.arange(GRID_W)
    col_idx = np.clip(col - NA_COLS // 2, 0, GRID_W - NA_COLS)[:, None] + np.arange(NA_COLS)[None, :]
    col_off = col_idx - col[:, None]
    k_grid = k.reshape(bn, rows, GRID_W, h, dh)
    v_grid = v.reshape(bn, rows, GRID_W, h, dh)
    q_rows = q.reshape(bn, rows, GRID_W, h, dh).transpose(1, 0, 2, 3, 4)

    def row_block(inp):
        q_r, r_start, r_off = inp
        k_band = lax.dynamic_slice_in_dim(k_grid, r_start, wr, axis=1)[:, :, col_idx]
        v_band = lax.dynamic_slice_in_dim(v_grid, r_start, wr, axis=1)[:, :, col_idx]
        bias = rpb[:, r_off[None, :, None] + (NA_ROWS_MAX - 1), col_off[:, None, :] + (NA_COLS - 1)]
        s_loc = jnp.einsum('bqhd,brqchd->bhqrc', q_r, k_band).astype(F32) + bias.astype(F32)[None]
        s_ctx = jnp.einsum('bqhd,bkhd->bhqk', q_r, k_ctx).astype(F32)
        s = jnp.concatenate([s_ctx, s_loc.reshape(bn, h, GRID_W, wr * NA_COLS)], axis=-1)
        p = jax.nn.softmax(s, axis=-1).astype(v.dtype)
        p_ctx = p[..., :n_ctx]
        p_loc = p[..., n_ctx:].reshape(bn, h, GRID_W, wr, NA_COLS)
        return (jnp.einsum('bhqk,bkhd->bqhd', p_ctx, v_ctx)
                + jnp.einsum('bhqrc,brqchd->bqhd', p_loc, v_band))

    o = lax.map(row_block, (q_rows, jnp.asarray(row_start, jnp.int32), jnp.asarray(row_off, jnp.int32)))
    return o.transpose(1, 0, 2, 3, 4).reshape(bn, n, h * dh)


def even_mixer(a_lat, a_ctx, w_in, conv_w, a_log, dt_bias, out_norm_g, q_norm_g, k_norm_g, w_out,
               cos, sin, need_ctx):
    pl = split_cols(a_lat @ w_in, EVEN_SPLITS)
    pc = split_cols(a_ctx @ w_in, EVEN_SPLITS)
    lat_a = deltanet_prep(pl[0], *pl[2:6], conv_w, a_log, dt_bias)
    ctx_a = deltanet_prep(pc[0], *pc[2:6], conv_w, a_log, dt_bias)
    o_lat_a, o_ctx_a = bidir_deltanet(lat_a, ctx_a)

    def qkv_b(p):
        bn, n, _ = p[6].shape
        q = rms_norm(p[6].reshape(bn, n, B_Q_HEADS, HEAD_DIM), q_norm_g) * HEAD_DIM ** -0.5
        k = rms_norm(p[7].reshape(bn, n, B_KV_HEADS, HEAD_DIM), k_norm_g)
        v = p[8].reshape(bn, n, B_KV_HEADS, HEAD_DIM)
        return q, k, v

    q_l, k_l, v_l = qkv_b(pl)
    q_c, k_c, v_c = qkv_b(pc)
    o_lat_b = gqa_latent(apply_rope(q_l, cos, sin), apply_rope(k_l, cos, sin), v_l, k_c, v_c)
    y_lat = jnp.concatenate([gated_out(o_lat_a, pl[1], out_norm_g), o_lat_b], axis=-1) @ w_out
    if not need_ctx:
        return y_lat, None
    o_ctx_b = gqa_context(q_c, k_c, v_c)
    y_ctx = jnp.concatenate([gated_out(o_ctx_a, pc[1], out_norm_g), o_ctx_b], axis=-1) @ w_out
    return y_lat, y_ctx


def odd_mixer(a_lat, a_ctx, w_in, rpb, w_out, need_ctx):
    def qkv_c(a):
        bn, n, _ = a.shape
        p = (a @ w_in).reshape(bn, n, 3, C_HEADS, HEAD_DIM)
        return p[:, :, 0] * HEAD_DIM ** -0.5, p[:, :, 1], p[:, :, 2]

    q_l, k_l, v_l = qkv_c(a_lat)
    q_c, k_c, v_c = qkv_c(a_ctx)
    y_lat = neighbourhood_latent(q_l, k_l, v_l, k_c, v_c, rpb) @ w_out
    if not need_ctx:
        return y_lat, None
    y_ctx = gqa_context(q_c, k_c, v_c) @ w_out
    return y_lat, y_ctx


def expert_choice_ffn(h, w_router, w_gate, w_up, w_down):
    bn, t, d = h.shape
    cap = max(1, EC_CAPACITY * t // N_EXPERTS)
    aff = jax.nn.softmax((h @ w_router).astype(F32), axis=-1)
    gate, idx = lax.top_k(jnp.swapaxes(aff, 1, 2), cap)
    xs = jax.vmap(lambda hb, ib: hb[ib])(h, idx)
    hid = jax.nn.silu(jnp.einsum('becd,edf->becf', xs, w_gate)) * jnp.einsum('becd,edf->becf', xs, w_up)
    y = jnp.einsum('becf,efd->becd', hid, w_down) * gate[..., None].astype(h.dtype)
    return jax.vmap(lambda yb, ib: jnp.zeros((t, d), yb.dtype).at[ib.reshape(-1)].add(yb.reshape(-1, d)))(y, idx)


def setup_inputs(seed: int = 0) -> dict:
    key = jax.random.key(seed)
    ks = jax.random.split(key, 26)

    def nrm(k, shape, scale):
        return jax.random.normal(k, shape, F32) * scale

    def gain(k, shape):
        return 1.0 + 0.02 * jax.random.normal(k, shape, F32)

    dt = jnp.exp(jax.random.uniform(ks[9], (N_EVEN, 2, A_HEADS), F32, math.log(1e-3), math.log(1e-1)))
    return {
        'x': nrm(ks[0], (BATCH, SEQ, D_MODEL), 1.0),
        'c': nrm(ks[1], (BATCH, D_MODEL), 1.0),
        'ctx': nrm(ks[2], (BATCH, CTX_LEN, D_MODEL), 1.0),
        'c_ctx': nrm(ks[3], (D_MODEL,), 1.0),
        'ada_w': nrm(ks[4], (DEPTH, D_MODEL, 6 * D_MODEL), 0.5 * D_MODEL ** -0.5),
        'ada_b': nrm(ks[5], (DEPTH, 6 * D_MODEL), 0.02),
        'ln1_g': gain(ks[6], (DEPTH, D_MODEL)),
        'ln2_g': gain(ks[7], (DEPTH, D_MODEL)),
        'even_w_in': nrm(ks[8], (N_EVEN, D_MODEL, IN_EVEN), D_MODEL ** -0.5),
        'even_conv_w': nrm(ks[10], (N_EVEN, SHORT_CONV, 3 * A_DIM), SHORT_CONV ** -0.5),
        'even_a_log': jnp.log(jax.random.uniform(ks[11], (N_EVEN, 2, A_HEADS), F32, 1.0, 16.0)),
        'even_dt_bias': dt + jnp.log(-jnp.expm1(-dt)),
        'even_out_norm_g': gain(ks[12], (N_EVEN, HEAD_DIM)),
        'even_q_norm_g': gain(ks[13], (N_EVEN, HEAD_DIM)),
        'even_k_norm_g': gain(ks[14], (N_EVEN, HEAD_DIM)),
        'even_w_out': nrm(ks[15], (N_EVEN, D_MIX, D_MODEL), D_MIX ** -0.5),
        'odd_w_in': nrm(ks[16], (N_ODD, D_MODEL, IN_ODD), D_MODEL ** -0.5),
        'odd_rpb': nrm(ks[17], (N_ODD, C_HEADS, 2 * NA_ROWS_MAX - 1, 2 * NA_COLS - 1), 0.2),
        'odd_w_out': nrm(ks[18], (N_ODD, C_DIM, D_MODEL), C_DIM ** -0.5),
        'router_w': nrm(ks[19], (DEPTH, D_MODEL, N_EXPERTS), D_MODEL ** -0.5),
        'expert_w_gate': nrm(ks[20], (DEPTH, N_EXPERTS, D_MODEL, EXPERT_FF), D_MODEL ** -0.5),
        'expert_w_up': nrm(ks[21], (DEPTH, N_EXPERTS, D_MODEL, EXPERT_FF), D_MODEL ** -0.5),
        'expert_w_down': nrm(ks[22], (DEPTH, N_EXPERTS, EXPERT_FF, D_MODEL), EXPERT_FF ** -0.5),
        'final_norm_g': gain(ks[23], (D_MODEL,)),
    }


def reference(x, c, ctx, c_ctx, ada_w, ada_b, ln1_g, ln2_g, even_w_in, even_conv_w, even_a_log,
              even_dt_bias, even_out_norm_g, even_q_norm_g, even_k_norm_g, even_w_out, odd_w_in,
              odd_rpb, odd_w_out, router_w, expert_w_gate, expert_w_up, expert_w_down, final_norm_g):
    cos, sin = axial_rope(x.shape[1], HEAD_DIM)
    h, hc = x, ctx
    s_lat = jax.nn.silu(c)
    s_ctx = jax.nn.silu(c_ctx)[None]
    for layer in range(DEPTH):
        need_ctx = layer < DEPTH - 1
        i = layer // 2
        mod_l = jnp.split((s_lat @ ada_w[layer] + ada_b[layer])[:, None, :], 6, axis=-1)
        mod_c = jnp.split((s_ctx @ ada_w[layer] + ada_b[layer])[:, None, :], 6, axis=-1)
        a_lat = modulate(h, ln1_g[layer], mod_l[0], mod_l[1])
        a_ctx = modulate(hc, ln1_g[layer], mod_c[0], mod_c[1])
        if layer % 2 == 0:
            y_lat, y_ctx = even_mixer(a_lat, a_ctx, even_w_in[i], even_conv_w[i], even_a_log[i],
                                      even_dt_bias[i], even_out_norm_g[i], even_q_norm_g[i],
                                      even_k_norm_g[i], even_w_out[i], cos, sin, need_ctx)
        else:
            y_lat, y_ctx = odd_mixer(a_lat, a_ctx, odd_w_in[i], odd_rpb[i], odd_w_out[i], need_ctx)
        moe = (router_w[layer], expert_w_gate[layer], expert_w_up[layer], expert_w_down[layer])
        h = h + mod_l[2] * y_lat
        h = h + mod_l[5] * expert_choice_ffn(modulate(h, ln2_g[layer], mod_l[3], mod_l[4]), *moe)
        if need_ctx:
            hc = hc + mod_c[2] * y_ctx
            hc = hc + mod_c[5] * expert_choice_ffn(modulate(hc, ln2_g[layer], mod_c[3], mod_c[4]), *moe)
    return rms_norm(h, final_norm_g)
```

```python
import functools
import math

import numpy as np
import jax
import jax.numpy as jnp
from jax import lax
from jax.experimental import pallas as pl
from jax.experimental.pallas import tpu as pltpu

F32, BF16, I32 = jnp.float32, jnp.bfloat16, jnp.int32

HEAD_DIM = 64
GRID_W = 64
DELTA_CHUNK = 64
NA_ROWS = 8
NA_COLS = 16
ROPE_THETA = 10000.0
EC_CAPACITY = 2
EPS = 1e-6

LANES = 128
TILE = 256
WIN = 384
VMEM_LIMIT = 56 * 1024 * 1024


def _cparams(*sem):
    return pltpu.CompilerParams(dimension_semantics=sem, vmem_limit_bytes=VMEM_LIMIT)


def _split3(a):
    a1 = a.astype(BF16)
    r1 = a - a1.astype(F32)
    a2 = r1.astype(BF16)
    a3 = (r1 - a2.astype(F32)).astype(BF16)
    return a1, a2, a3


def _dot(a, b):
    return jnp.dot(a, b, preferred_element_type=F32)


def _dot_t(a, b):
    return lax.dot_general(a, b, (((1,), (1,)), ((), ())), preferred_element_type=F32)


def _dot_exact_rhs(a, b_bf16):
    a1, a2, a3 = _split3(a)
    return _dot(a1, b_bf16) + _dot(a2, b_bf16) + _dot(a3, b_bf16)


def _dot_hi(a, b):
    a1 = a.astype(BF16)
    a2 = (a - a1.astype(F32)).astype(BF16)
    b1 = b.astype(BF16)
    b2 = (b - b1.astype(F32)).astype(BF16)
    return _dot(a1, b1) + _dot(a1, b2) + _dot(a2, b1)


def _sigmoid(x):
    return 1.0 / (1.0 + jnp.exp(-x))


def _silu(x):
    return x * _sigmoid(x)


def _modulate(x, g, shift, scale):
    ms = jnp.mean(x * x, axis=-1, keepdims=True)
    return x * lax.rsqrt(ms + EPS) * g * (1.0 + scale) + shift


def _adaln_kernel(s_ref, w_ref, b_ref, o_ref):
    s = _silu(s_ref[...])
    o_ref[...] = _dot_hi(s, w_ref[...]) + b_ref[...]


def _adaln(rows, w, b):
    d, n = w.shape
    tn = n // 4
    return pl.pallas_call(
        _adaln_kernel,
        grid=(n // tn,),
        in_specs=[pl.BlockSpec((8, d), lambda j: (0, 0)),
                  pl.BlockSpec((d, tn), lambda j: (0, j)),
                  pl.BlockSpec((1, tn), lambda j: (0, j))],
        out_specs=pl.BlockSpec((8, tn), lambda j: (0, j)),
        out_shape=jax.ShapeDtypeStruct((8, n), F32),
        compiler_params=_cparams("arbitrary"),
        name="adaln",
    )(rows, w, b.reshape(1, n))


def _modproj_kernel(h_ref, mod_ref, g_ref, w_ref, o_ref):
    mod = mod_ref[0, 0]
    a = _modulate(h_ref[0], g_ref[...], mod[0:1], mod[1:2])
    o_ref[0] = _dot(a.astype(BF16), w_ref[...]).astype(o_ref.dtype)


def _modproj(h, mod, g, w_bf16, nct, out_dtype):
    bsz, t, d = h.shape
    n = w_bf16.shape[1]
    return pl.pallas_call(
        _modproj_kernel,
        grid=(bsz, t // TILE),
        in_specs=[pl.BlockSpec((1, TILE, d), lambda b, i: (b, i, 0)),
                  pl.BlockSpec((1, 1, 6, d), lambda b, i: (b, jnp.where(i < nct, 0, 1), 0, 0)),
                  pl.BlockSpec((1, d), lambda b, i: (0, 0)),
                  pl.BlockSpec((d, n), lambda b, i: (0, 0))],
        out_specs=pl.BlockSpec((1, TILE, n), lambda b, i: (b, i, 0)),
        out_shape=jax.ShapeDtypeStruct((bsz, t, n), out_dtype),
        compiler_params=_cparams("arbitrary", "arbitrary"),
        name="modproj",
    )(h, mod, g.reshape(1, d), w_bf16)


def _router_kernel(h_ref, mod_ref, g_ref, wr_ref, a_ref, aff_ref, afft_ref, *, n_exp):
    mod = mod_ref[0, 0]
    a = _modulate(h_ref[0], g_ref[...], mod[3:4], mod[4:5])
    a_ref[0] = a.astype(BF16)
    logits = _dot_hi(a, wr_ref[...])
    lane = lax.broadcasted_iota(I32, logits.shape, 1)
    logits = jnp.where(lane < n_exp, logits, -jnp.inf)
    p = jnp.exp(logits - jnp.max(logits, axis=-1, keepdims=True))
    aff = p / jnp.sum(p, axis=-1, keepdims=True)
    aff_ref[0] = aff
    afft_ref[0] = aff.T


def _router(h, mod, g, w_router, nct):
    bsz, t, d = h.shape
    n_exp = w_router.shape[1]
    wr = jnp.zeros((d, LANES), F32).at[:, :n_exp].set(w_router)
    return pl.pallas_call(
        functools.partial(_router_kernel, n_exp=n_exp),
        grid=(bsz, t // TILE),
        in_specs=[pl.BlockSpec((1, TILE, d), lambda b, i: (b, i, 0)),
                  pl.BlockSpec((1, 1, 6, d), lambda b, i: (b, jnp.where(i < nct, 0, 1), 0, 0)),
                  pl.BlockSpec((1, d), lambda b, i: (0, 0)),
                  pl.BlockSpec((d, LANES), lambda b, i: (0, 0))],
        out_specs=[pl.BlockSpec((1, TILE, d), lambda b, i: (b, i, 0)),
                   pl.BlockSpec((1, TILE, LANES), lambda b, i: (b, i, 0)),
                   pl.BlockSpec((1, LANES, TILE), lambda b, i: (b, 0, i))],
        out_shape=[jax.ShapeDtypeStruct((bsz, t, d), BF16),
                   jax.ShapeDtypeStruct((bsz, t, LANES), F32),
                   jax.ShapeDtypeStruct((bsz, LANES, t), F32)],
        compiler_params=_cparams("arbitrary", "arbitrary"),
        name="router",
    )(h, mod, g.reshape(1, d), wr)


def _select_kernel(afft_ref, pos_ref, post_ref, cb_ref, cnt_ref, *, n_exp, n_ctx, k_ctx, k_lat, t):
    aff = afft_ref[0][:n_exp]
    bits = pltpu.bitcast(aff, I32)
    tok = lax.broadcasted_iota(I32, (n_exp, t), 1)
    is_ctx = tok < n_ctx

    def kth_largest(mb, k):
        def body(it, thr):
            cand = thr | jnp.left_shift(jnp.int32(1), 30 - it)
            cnt = jnp.sum((mb >= cand).astype(I32), axis=1, keepdims=True)
            return jnp.where(cnt >= k, cand, thr)
        return lax.fori_loop(0, 31, body, jnp.zeros((n_exp, 1), I32))

    mb_l = jnp.where(is_ctx, -1, bits)
    thr_l = kth_largest(mb_l, k_lat)
    gt = mb_l > thr_l
    eq = mb_l == thr_l
    need_l = k_lat - jnp.sum(gt.astype(I32), axis=1, keepdims=True)
    if k_ctx > 0:
        mb_c = jnp.where(is_ctx, bits, -1)
        thr_c = kth_largest(mb_c, k_ctx)
        gt_c = mb_c > thr_c
        eq_c = mb_c == thr_c
        need_c = k_ctx - jnp.sum(gt_c.astype(I32), axis=1, keepdims=True)
        gt = gt | gt_c
        eq = eq | eq_c
    else:
        need_c = jnp.zeros((n_exp, 1), I32)

    rr = lax.broadcasted_iota(I32, (LANES, LANES), 0)
    cc = lax.broadcasted_iota(I32, (LANES, LANES), 1)
    upper = (rr <= cc).astype(BF16)
    stacked = jnp.concatenate([gt, eq], axis=0).astype(F32).astype(BF16)
    carry = jnp.zeros((2 * n_exp, 1), F32)
    eq_ctx_total = jnp.zeros((n_exp, 1), F32)
    for blk in range(t // LANES):
        c = _dot(stacked[:, blk * LANES:(blk + 1) * LANES], upper) + carry
        cnt_ref[:, blk * LANES:(blk + 1) * LANES] = c
        carry = c[:, LANES - 1:LANES]
        if (blk + 1) * LANES == n_ctx:
            eq_ctx_total = carry[n_exp:]
    cgt = cnt_ref[:n_exp, :].astype(I32)
    ceq = cnt_ref[n_exp:, :].astype(I32)
    eq_ctx_total = eq_ctx_total.astype(I32)
    ceq_set = jnp.where(is_ctx, ceq, ceq - eq_ctx_total)
    need = jnp.where(is_ctx, need_c, need_l)
    sel = gt | (eq & (ceq_set <= need))
    eq_before = jnp.where(is_ctx, 0, jnp.minimum(eq_ctx_total, need_c))
    posinc = cgt + eq_before + jnp.minimum(ceq_set, need)
    pos = jnp.where(sel, posinc - 1, -1)
    pos_ref[0] = pos

    lane = lax.broadcasted_iota(I32, (n_exp, LANES), 1)
    cb = jnp.zeros((n_exp, LANES), I32)
    for i in range(1, t // TILE):
        cb = jnp.where(lane == i, posinc[:, i * TILE - 1:i * TILE], cb)
    cb_ref[0] = cb

    padded = jnp.concatenate([pos.astype(F32), jnp.full((LANES - n_exp, t), -1.0, F32)], axis=0)
    for i in range(t // TILE):
        post_ref[0, i * TILE:(i + 1) * TILE, :] = padded[:, i * TILE:(i + 1) * TILE].T.astype(I32)


def _select(afft, n_exp, n_ctx, k_ctx, k_lat):
    bsz, _, t = afft.shape
    assert n_ctx % LANES == 0 and t % TILE == 0 and t // TILE <= LANES
    return pl.pallas_call(
        functools.partial(_select_kernel, n_exp=n_exp, n_ctx=n_ctx, k_ctx=k_ctx, k_lat=k_lat, t=t),
        grid=(bsz,),
        in_specs=[pl.BlockSpec((1, LANES, t), lambda b: (b, 0, 0))],
        out_specs=[pl.BlockSpec((1, n_exp, t), lambda b: (b, 0, 0)),
                   pl.BlockSpec((1, t, LANES), lambda b: (b, 0, 0)),
                   pl.BlockSpec((1, n_exp, LANES), lambda b: (b, 0, 0))],
        out_shape=[jax.ShapeDtypeStruct((bsz, n_exp, t), I32),
                   jax.ShapeDtypeStruct((bsz, t, LANES), I32),
                   jax.ShapeDtypeStruct((bsz, n_exp, LANES), I32)],
        scratch_shapes=[pltpu.VMEM((2 * n_exp, t), F32)],
        compiler_params=_cparams("arbitrary"),
        name="select",
    )(afft)


def _ffn_rows(cap_tot):
    for rc in range(min(cap_tot, 512), 7, -8):
        if cap_tot % rc == 0:
            return rc
    raise ValueError(cap_tot)


def _expert_kernel(cb_ref, pos_ref, a_ref, wg_ref, wu_ref, wd_ref, y_ref, x_ref, *, n_exp, ntiles, cap_tot):
    e, b, i = pl.program_id(0), pl.program_id(1), pl.program_id(2)

    @pl.when(i == 0)
    def _():
        x_ref[...] = jnp.zeros_like(x_ref)

    start = cb_ref[(b * n_exp + e) * ntiles + i]
    start_al = pl.multiple_of((start // LANES) * LANES, LANES)
    rel = pos_ref[0, 0, 0] - start_al
    row = lax.broadcasted_iota(I32, (WIN, TILE), 0)
    onehot = jnp.where(row == rel, 1.0, 0.0).astype(BF16)
    x_ref[pl.ds(start_al, WIN), :] += _dot(onehot, a_ref[0])

    @pl.when(i == ntiles - 1)
    def _():
        rc = _ffn_rows(cap_tot)
        wg = wg_ref[0].astype(BF16)
        wu = wu_ref[0].astype(BF16)
        wd = wd_ref[0].astype(BF16)
        for r0 in range(0, cap_tot, rc):
            xc = x_ref[r0:r0 + rc, :].astype(BF16)
            hid = _silu(_dot(xc, wg)) * _dot(xc, wu)
            y_ref[0, 0, r0:r0 + rc, :] = _dot(hid.astype(BF16), wd)
        y_ref[0, 0, cap_tot:, :] = jnp.zeros((WIN, y_ref.shape[-1]), F32)


def _experts(cb_flat, pos, a, wg, wu, wd, cap_tot):
    bsz, n_exp, t = pos.shape
    d = a.shape[-1]
    ff = wg.shape[-1]
    ntiles = t // TILE
    rows = cap_tot + WIN
    pos5 = pos.reshape(bsz, n_exp, ntiles, 1, TILE)
    grid_spec = pltpu.PrefetchScalarGridSpec(
        num_scalar_prefetch=1,
        grid=(n_exp, bsz, ntiles),
        in_specs=[pl.BlockSpec((1, 1, 1, 1, TILE), lambda e, b, i, cb: (b, e, i, 0, 0)),
                  pl.BlockSpec((1, TILE, d), lambda e, b, i, cb: (b, i, 0)),
                  pl.BlockSpec((1, d, ff), lambda e, b, i, cb: (e, 0, 0)),
                  pl.BlockSpec((1, d, ff), lambda e, b, i, cb: (e, 0, 0)),
                  pl.BlockSpec((1, ff, d), lambda e, b, i, cb: (e, 0, 0))],
        out_specs=pl.BlockSpec((1, 1, rows, d), lambda e, b, i, cb: (b, e, 0, 0)),
        scratch_shapes=[pltpu.VMEM((rows, d), F32)],
    )
    return pl.pallas_call(
        functools.partial(_expert_kernel, n_exp=n_exp, ntiles=ntiles, cap_tot=cap_tot),
        grid_spec=grid_spec,
        out_shape=jax.ShapeDtypeStruct((bsz, n_exp, rows, d), F32),
        compiler_params=_cparams("arbitrary", "arbitrary", "arbitrary"),
        name="experts",
    )(cb_flat, pos5, a, wg, wu, wd)


def _combine(cb_flat, h, mod, aff, post, y, g_final, nct, final_norm):
    bsz, t, d = h.shape
    n_exp = y.shape[1]
    ntiles = t // TILE
    first = nct if final_norm else 0
    grid_spec = pltpu.PrefetchScalarGridSpec(
        num_scalar_prefetch=1,
        grid=(bsz, ntiles - first),
        in_specs=[pl.BlockSpec((1, TILE, d), lambda b, i, cb: (b, i + first, 0)),
                  pl.BlockSpec((1, 1, 6, d), lambda b, i, cb: (b, jnp.where(i + first < nct, 0, 1), 0, 0)),
                  pl.BlockSpec((1, TILE, LANES), lambda b, i, cb: (b, i + first, 0)),
                  pl.BlockSpec((1, TILE, LANES), lambda b, i, cb: (b, i + first, 0)),
                  pl.BlockSpec(memory_space=pl.ANY),
                  pl.BlockSpec((1, d), lambda b, i, cb: (0, 0))],
        out_specs=pl.BlockSpec((1, TILE, d), lambda b, i, cb: (b, i, 0)),
        scratch_shapes=[pltpu.VMEM((2, WIN, d), F32), pltpu.SemaphoreType.DMA((2,))],
    )

    def body(cb_ref, h_ref, mod_ref, aff_ref, post_ref, y_hbm, g_ref, o_ref, ybuf, sem):
        _combine_tile(cb_ref, h_ref, mod_ref, aff_ref, post_ref, y_hbm, g_ref, o_ref, ybuf, sem,
                      n_exp=n_exp, ntiles=ntiles, first=first, final_norm=final_norm)

    return pl.pallas_call(
        body,
        grid_spec=grid_spec,
        out_shape=jax.ShapeDtypeStruct((bsz, t - first * TILE, d), F32),
        compiler_params=_cparams("arbitrary", "arbitrary"),
        name="combine",
    )(cb_flat, h, mod, aff, post, y, g_final.reshape(1, d))


def _combine_tile(cb_ref, h_ref, mod_ref, aff_ref, post_ref, y_hbm, g_ref, o_ref, ybuf, sem,
                  *, n_exp, ntiles, first, final_norm):
    b, i = pl.program_id(0), pl.program_id(1) + first

    def window(e, slot):
        start = cb_ref[(b * n_exp + e) * ntiles + i]
        start_al = pl.multiple_of((start // LANES) * LANES, LANES)
        copy = pltpu.make_async_copy(y_hbm.at[b, e, pl.ds(start_al, WIN), :], ybuf.at[slot], sem.at[slot])
        return copy, start_al

    window(0, 0)[0].start()
    acc = jnp.zeros(o_ref.shape[1:], F32)
    lane = lax.broadcasted_iota(I32, (TILE, WIN), 1)
    post = post_ref[0]
    aff = aff_ref[0]
    for e in range(n_exp):
        slot = e % 2
        if e + 1 < n_exp:
            window(e + 1, 1 - slot)[0].start()
        copy, start_al = window(e, slot)
        copy.wait()
        rel = post[:, e:e + 1] - start_al
        onehot = jnp.where(lane == rel, 1.0, 0.0).astype(BF16)
        acc = acc + aff[:, e:e + 1] * _dot(onehot, ybuf[slot].astype(BF16))
    mod = mod_ref[0, 0]
    out = h_ref[0] + mod[5:6] * acc
    if final_norm:
        ms = jnp.mean(out * out, axis=-1, keepdims=True)
        out = out * lax.rsqrt(ms + EPS) * g_ref[...]
    o_ref[0] = out


def _moe(h, mod, ln2_g, w_router, wg, wu, wd, g_final, *, n_ctx, route_ctx, final_norm):
    bsz, t, d = h.shape
    n_exp = w_router.shape[1]
    nct = n_ctx // TILE
    n_lat = t - n_ctx
    k_lat = max(1, EC_CAPACITY * n_lat // n_exp)
    k_ctx = max(1, EC_CAPACITY * n_ctx // n_exp) if route_ctx else 0
    a, aff, afft = _router(h, mod, ln2_g, w_router, nct)
    pos, post, cb = _select(afft, n_exp, n_ctx, k_ctx, k_lat)
    cb_flat = cb[:, :, :t // TILE].reshape(-1)
    y = _experts(cb_flat, pos, a, wg, wu, wd, k_ctx + k_lat)
    return _combine(cb_flat, h, mod, aff, post, y, g_final, nct, final_norm)


HEAD_SHIFT = int(math.log2(HEAD_DIM))


def _same_head(rows, cols):
    r = lax.shift_right_logical(lax.broadcasted_iota(I32, (rows, cols), 0), HEAD_SHIFT)
    c = lax.shift_right_logical(lax.broadcasted_iota(I32, (rows, cols), 1), HEAD_SHIFT)
    return jnp.where(r == c, 1.0, 0.0)


def _head_sum(x2):
    n = x2.shape[-1]
    return _dot_exact_rhs(x2, _same_head(n, n).astype(BF16))


def _outproj_even_kernel(h_ref, mod_ref, of_ref, ob_ref, z_ref, at_ref, g_ref, w_ref, o_ref, *, a_dim):
    mod = mod_ref[0, 0]
    o = of_ref[0] + ob_ref[0]
    ms = _head_sum(o * o) * (1.0 / HEAD_DIM)
    ya = o * lax.rsqrt(ms + EPS) * g_ref[...] * _silu(z_ref[0])
    y = _dot(ya.astype(BF16), w_ref[:a_dim, :]) + _dot(at_ref[0], w_ref[a_dim:, :])
    o_ref[0] = h_ref[0] + mod[2:3] * y


def _outproj_even(h, mod, o_delta, p, o_attn, out_norm_g, w_bf16, nct, *, a_dim):
    bsz, t, d = h.shape
    o_f, o_b = o_delta
    g = jnp.tile(out_norm_g, a_dim // HEAD_DIM).reshape(1, a_dim)
    tok = lambda b, i: (b, i, 0)
    return pl.pallas_call(
        functools.partial(_outproj_even_kernel, a_dim=a_dim),
        grid=(bsz, t // TILE),
        in_specs=[pl.BlockSpec((1, TILE, d), tok),
                  pl.BlockSpec((1, 1, 6, d), lambda b, i: (b, jnp.where(i < nct, 0, 1), 0, 0)),
                  pl.BlockSpec((1, TILE, a_dim), tok),
                  pl.BlockSpec((1, TILE, a_dim), tok),
                  pl.BlockSpec((1, TILE, a_dim), lambda b, i: (b, i, 3)),
                  pl.BlockSpec((1, TILE, a_dim), tok),
                  pl.BlockSpec((1, a_dim), lambda b, i: (0, 0)),
                  pl.BlockSpec((d, d), lambda b, i: (0, 0))],
        out_specs=pl.BlockSpec((1, TILE, d), tok),
        out_shape=jax.ShapeDtypeStruct((bsz, t, d), F32),
        compiler_params=_cparams("arbitrary", "arbitrary"),
        name="outproj_even",
    )(h, mod, o_f, o_b, p, o_attn, g, w_bf16)


def _outproj_odd_kernel(h_ref, mod_ref, a_ref, w_ref, o_ref):
    mod = mod_ref[0, 0]
    o_ref[0] = h_ref[0] + mod[2:3] * _dot(a_ref[0], w_ref[...])


def _outproj_odd(h, mod, o_na, w_bf16, nct):
    bsz, t, d = h.shape
    n_lat = o_na.shape[1]
    return pl.pallas_call(
        _outproj_odd_kernel,
        grid=(bsz, n_lat // TILE),
        in_specs=[pl.BlockSpec((1, TILE, d), lambda b, i: (b, i + nct, 0)),
                  pl.BlockSpec((1, 1, 6, d), lambda b, i: (b, 1, 0, 0)),
                  pl.BlockSpec((1, TILE, d), lambda b, i: (b, i, 0)),
                  pl.BlockSpec((d, d), lambda b, i: (0, 0))],
        out_specs=pl.BlockSpec((1, TILE, d), lambda b, i: (b, i, 0)),
        out_shape=jax.ShapeDtypeStruct((bsz, n_lat, d), F32),
        compiler_params=_cparams("arbitrary", "arbitrary"),
        name="outproj_odd",
    )(h, mod, o_na, w_bf16)


def _rope_tables(n_ctx, n_lat):
    tt = jnp.arange(n_lat)
    n_freq = HEAD_DIM // 4
    inv = ROPE_THETA ** (-jnp.arange(n_freq, dtype=F32) / n_freq)
    ang = jnp.concatenate([(tt // GRID_W).astype(F32)[:, None] * inv, (tt % GRID_W).astype(F32)[:, None] * inv], -1)
    cos = jnp.concatenate([jnp.ones((n_ctx, 2 * n_freq), F32), jnp.cos(ang)], axis=0)
    sin = jnp.concatenate([jnp.zeros((n_ctx, 2 * n_freq), F32), jnp.sin(ang)], axis=0)
    reps = LANES // HEAD_DIM
    return jnp.tile(cos, (1, 2 * reps)), jnp.tile(jnp.concatenate([-sin, sin], axis=1), (1, reps))


def _rope(x, cos, sin):
    n = x.shape[-1]
    half = HEAD_DIM // 2
    lane = lax.broadcasted_iota(I32, x.shape, 1)
    first = (lane & (HEAD_DIM - 1)) < half
    partner = jnp.where(first, pltpu.roll(x, n - half, 1), pltpu.roll(x, half, 1))
    reps = n // LANES
    cos_n = jnp.concatenate([cos] * reps, axis=1) if reps > 1 else cos
    sin_n = jnp.concatenate([sin] * reps, axis=1) if reps > 1 else sin
    return x * cos_n + partner * sin_n


def _gqa_prep_kernel(q_ref, k_ref, v_ref, cos_ref, sin_ref, gq_ref, gk_ref, qo_ref, ko_ref, vo_ref):
    cos, sin = cos_ref[...], sin_ref[...]
    q = q_ref[0]
    q = q * lax.rsqrt(_head_sum(q * q) * (1.0 / HEAD_DIM) + EPS) * gq_ref[...] * (HEAD_DIM ** -0.5)
    qo_ref[0] = _rope(q, cos, sin).astype(BF16)
    k = k_ref[0]
    k = k * lax.rsqrt(_head_sum(k * k) * (1.0 / HEAD_DIM) + EPS) * gk_ref[...]
    k = _rope(k, cos, sin).astype(BF16)
    v = v_ref[0].astype(BF16)
    for hh in range(k.shape[-1] // HEAD_DIM):
        ko_ref[0, hh] = k[:, hh * HEAD_DIM:(hh + 1) * HEAD_DIM]
        vo_ref[0, hh] = v[:, hh * HEAD_DIM:(hh + 1) * HEAD_DIM]


def _gqa_prep(p, cos, sin, q_norm_g, k_norm_g, *, a_dim, kv_dim):
    bsz, t, _ = p.shape
    q_dim = a_dim
    kvh = kv_dim // HEAD_DIM
    assert kv_dim == LANES
    c_q = (4 * a_dim) // q_dim
    c_k = (4 * a_dim + q_dim) // kv_dim
    gq = jnp.tile(q_norm_g, q_dim // HEAD_DIM).reshape(1, q_dim)
    gk = jnp.tile(k_norm_g, kvh).reshape(1, kv_dim)
    return pl.pallas_call(
        _gqa_prep_kernel,
        grid=(bsz, t // TILE),
        in_specs=[pl.BlockSpec((1, TILE, q_dim), lambda b, i: (b, i, c_q)),
                  pl.BlockSpec((1, TILE, kv_dim), lambda b, i: (b, i, c_k)),
                  pl.BlockSpec((1, TILE, kv_dim), lambda b, i: (b, i, c_k + 1)),
                  pl.BlockSpec((TILE, LANES), lambda b, i: (i, 0)),
                  pl.BlockSpec((TILE, LANES), lambda b, i: (i, 0)),
                  pl.BlockSpec((1, q_dim), lambda b, i: (0, 0)),
                  pl.BlockSpec((1, kv_dim), lambda b, i: (0, 0))],
        out_specs=[pl.BlockSpec((1, TILE, q_dim), lambda b, i: (b, i, 0)),
                   pl.BlockSpec((1, kvh, TILE, HEAD_DIM), lambda b, i: (b, 0, i, 0)),
                   pl.BlockSpec((1, kvh, TILE, HEAD_DIM), lambda b, i: (b, 0, i, 0))],
        out_shape=[jax.ShapeDtypeStruct((bsz, t, q_dim), BF16),
                   jax.ShapeDtypeStruct((bsz, kvh, t, HEAD_DIM), BF16),
                   jax.ShapeDtypeStruct((bsz, kvh, t, HEAD_DIM), BF16)],
        compiler_params=_cparams("arbitrary", "arbitrary"),
        name="gqa_prep",
    )(p, p, p, cos, sin, gq, gk)


def _gqa_kernel(q_ref, k_ref, v_ref, o_ref, qs_ref, m_ref, l_ref, acc_ref, *, n_ctx, nct, tq, tk, group):
    i, kv = pl.program_id(2), pl.program_id(3)

    @pl.when(kv == 0)
    def _():
        m_ref[...] = jnp.full_like(m_ref, -jnp.inf)
        l_ref[...] = jnp.zeros_like(l_ref)
        acc_ref[...] = jnp.zeros_like(acc_ref)
        q = q_ref[0]
        for g in range(group):
            qs_ref[g * tq:(g + 1) * tq, :] = q[:, g * HEAD_DIM:(g + 1) * HEAD_DIM]

    @pl.when((i >= nct) | (kv * tk < n_ctx))
    def _():
        s = _dot_t(qs_ref[...], k_ref[0, 0])
        col = kv * tk + lax.broadcasted_iota(I32, s.shape, 1)
        s = jnp.where((col < n_ctx) | (i >= nct), s, -jnp.inf)
        m_old = m_ref[...]
        m_new = jnp.maximum(m_old, jnp.max(s, axis=-1, keepdims=True))
        alpha = jnp.exp(m_old - m_new)
        p = jnp.exp(s - m_new)
        l_ref[...] = alpha * l_ref[...] + jnp.sum(p, axis=-1, keepdims=True)
        acc_ref[...] = alpha * acc_ref[...] + _dot(p.astype(BF16), v_ref[0, 0])
        m_ref[...] = m_new

    @pl.when(kv == pl.num_programs(3) - 1)
    def _():
        out = acc_ref[...] / l_ref[...]
        for g in range(group):
            o_ref[0, :, g * HEAD_DIM:(g + 1) * HEAD_DIM] = out[g * tq:(g + 1) * tq, :].astype(o_ref.dtype)


def _gqa_attention(q, k, v, *, n_ctx):
    bsz, t, q_dim = q.shape
    kvh = k.shape[1]
    group = q_dim // HEAD_DIM // kvh
    tq = TILE
    nt = t // TILE
    mult = max(m for m in (1, 2, 3, 4, 5) if nt % m == 0)
    tk = TILE * mult
    return pl.pallas_call(
        functools.partial(_gqa_kernel, n_ctx=n_ctx, nct=n_ctx // tq, tq=tq, tk=tk, group=group),
        grid=(bsz, kvh, t // tq, t // tk),
        in_specs=[pl.BlockSpec((1, tq, group * HEAD_DIM), lambda b, h, i, j: (b, i, h)),
                  pl.BlockSpec((1, 1, tk, HEAD_DIM), lambda b, h, i, j: (b, h, j, 0)),
                  pl.BlockSpec((1, 1, tk, HEAD_DIM), lambda b, h, i, j: (b, h, j, 0))],
        out_specs=pl.BlockSpec((1, tq, group * HEAD_DIM), lambda b, h, i, j: (b, i, h)),
        out_shape=jax.ShapeDtypeStruct((bsz, t, q_dim), BF16),
        scratch_shapes=[pltpu.VMEM((group * tq, HEAD_DIM), BF16),
                        pltpu.VMEM((group * tq, 1), F32),
                        pltpu.VMEM((group * tq, 1), F32),
                        pltpu.VMEM((group * tq, HEAD_DIM), F32)],
        compiler_params=_cparams("arbitrary", "arbitrary", "arbitrary", "arbitrary"),
        name="gqa_attention",
    )(q, k, v)


def _natten_bias(rpb):
    n_heads = rpb.shape[0]
    col = np.arange(GRID_W)
    col_start = np.clip(col - NA_COLS // 2, 0, GRID_W - NA_COLS)
    c2 = np.arange(GRID_W)
    valid = (c2[None, :] >= col_start[:, None]) & (c2[None, :] < col_start[:, None] + NA_COLS)
    cidx = np.clip(c2[None, :] - col[:, None] + NA_COLS - 1, 0, 2 * NA_COLS - 2)
    ridx = np.arange(NA_ROWS)[:, None] + np.arange(NA_ROWS)[None, :]
    tab = rpb[:, ridx[:, :, None, None], cidx[None, None, :, :]]
    tab = jnp.where(valid[None, None, None], tab, -jnp.inf)
    tab = tab.transpose(0, 1, 3, 2, 4).reshape(n_heads, NA_ROWS, GRID_W, NA_ROWS * GRID_W)
    return tab.reshape(n_heads // 2, 2, NA_ROWS, GRID_W, NA_ROWS * GRID_W).astype(F32)


def _natten_kernel(q_ref, k_ref, v_ref, bias_ref, o_ref, *, n_ctx, n_rows, rblk):
    rb = pl.program_id(2)
    first_head = lax.broadcasted_iota(I32, (1, LANES), 1) < HEAD_DIM
    kc = k_ref[0, 0:n_ctx, :]
    vc = v_ref[0, 0:n_ctx, :]
    band = NA_ROWS * GRID_W
    for rr in range(rblk):
        rg = rb * rblk + rr
        rs = jnp.clip(rg - NA_ROWS // 2, 0, n_rows - NA_ROWS)
        variant = rs - rg + (NA_ROWS - 1)
        start = pl.multiple_of(n_ctx + rs * GRID_W, GRID_W)
        kb = k_ref[0, pl.ds(start, band), :]
        vb = v_ref[0, pl.ds(start, band), :]
        q = q_ref[0, rr * GRID_W:(rr + 1) * GRID_W, :] * (HEAD_DIM ** -0.5)
        outs = []
        for hh in range(2):
            qh = jnp.where(first_head if hh == 0 else jnp.logical_not(first_head), q, jnp.zeros_like(q))
            sc = _dot_t(qh, kc)
            sl = _dot_t(qh, kb) + bias_ref[0, hh, variant]
            m = jnp.maximum(jnp.max(sc, axis=-1, keepdims=True), jnp.max(sl, axis=-1, keepdims=True))
            pc = jnp.exp(sc - m)
            pb = jnp.exp(sl - m)
            l = jnp.sum(pc, axis=-1, keepdims=True) + jnp.sum(pb, axis=-1, keepdims=True)
            outs.append((_dot(pc.astype(BF16), vc) + _dot(pb.astype(BF16), vb)) / l)
        o_ref[0, rr * GRID_W:(rr + 1) * GRID_W, :] = jnp.where(first_head, outs[0], outs[1]).astype(o_ref.dtype)


def _natten(p, bias, *, n_ctx, d):
    bsz, t, _ = p.shape
    n_lat = t - n_ctx
    n_rows = n_lat // GRID_W
    rblk = TILE // GRID_W
    npairs = d // LANES
    assert n_rows >= NA_ROWS and n_ctx % TILE == 0
    return pl.pallas_call(
        functools.partial(_natten_kernel, n_ctx=n_ctx, n_rows=n_rows, rblk=rblk),
        grid=(bsz, npairs, n_lat // TILE),
        in_specs=[pl.BlockSpec((1, TILE, LANES), lambda b, hp, r: (b, n_ctx // TILE + r, hp)),
                  pl.BlockSpec((1, t, LANES), lambda b, hp, r: (b, 0, npairs + hp)),
                  pl.BlockSpec((1, t, LANES), lambda b, hp, r: (b, 0, 2 * npairs + hp)),
                  pl.BlockSpec((1, 2, NA_ROWS, GRID_W, NA_ROWS * GRID_W), lambda b, hp, r: (hp, 0, 0, 0, 0))],
        out_specs=pl.BlockSpec((1, TILE, LANES), lambda b, hp, r: (b, r, hp)),
        out_shape=jax.ShapeDtypeStruct((bsz, n_lat, d), BF16),
        compiler_params=_cparams("arbitrary", "arbitrary", "arbitrary"),
        name="natten",
    )(p, p, p, bias)


def _dot_exact_lhs(a_bf16, b):
    b1, b2, b3 = _split3(b)
    return _dot(a_bf16, b1) + _dot(a_bf16, b2) + _dot(a_bf16, b3)


def _delta_prep_kernel(x_ref, prev_ref, next_ref, gate_ref, w_ref, alog_ref, dt_ref, qkv_ref, tab_ref, xs_ref,
                       *, nct, a_dim):
    i = pl.program_id(1)
    last = pl.num_programs(1) - 1
    x = x_ref[0]
    rows = x.shape[0]
    xs_ref[0:8, :] = prev_ref[0]
    xs_ref[8:8 + rows, :] = x
    xs_ref[8 + rows:16 + rows, :] = next_ref[0]
    r = lax.broadcasted_iota(I32, (rows, 1), 0)
    at_start = (i == 0) | (i == nct)
    at_end = (i == nct - 1) | (i == last)
    xm = jnp.where((r == 0) & at_start, 0.0, xs_ref[7:7 + rows, :])
    xp = jnp.where((r == rows - 1) & at_end, 0.0, xs_ref[9:9 + rows, :])
    w = w_ref[...]
    y = _silu(w[0:1] * xm + w[1:2] * x + w[2:3] * xp)
    q, k, v = y[:, :a_dim], y[:, a_dim:2 * a_dim], y[:, 2 * a_dim:]
    qkv_ref[0, :, :a_dim] = q * lax.rsqrt(_head_sum(q * q) + EPS) * (HEAD_DIM ** -0.5)
    qkv_ref[0, :, a_dim:2 * a_dim] = k * lax.rsqrt(_head_sum(k * k) + EPS)
    qkv_ref[0, :, 2 * a_dim:] = v

    nh = a_dim // HEAD_DIM
    g = gate_ref[0]
    z = g + dt_ref[...]
    softplus = jnp.maximum(z, 0.0) + jnp.log1p(jnp.exp(-jnp.abs(z)))
    log_a = -jnp.exp(alog_ref[...]) * softplus
    beta = _sigmoid(g)
    ri = lax.broadcasted_iota(I32, (rows, rows), 0)
    ci = lax.broadcasted_iota(I32, (rows, rows), 1)
    chunk_shift = int(math.log2(DELTA_CHUNK))
    same_chunk = lax.shift_right_logical(ri, chunk_shift) == lax.shift_right_logical(ci, chunk_shift)
    prefix = jnp.where(same_chunk & (ci <= ri), 1.0, 0.0).astype(BF16)
    suffix = jnp.where(same_chunk & (ci >= ri), 1.0, 0.0).astype(BF16)
    ones_chunk = jnp.where(same_chunk, 1.0, 0.0).astype(BF16)
    g_f = _dot_exact_lhs(prefix, log_a)
    g_b = _dot_exact_lhs(suffix, log_a)
    lane = lax.broadcasted_iota(I32, g.shape, 1)
    narrow = jnp.where(lane < nh, g_f, jnp.where(lane < 2 * nh, beta, jnp.where(lane < 3 * nh, g_b, beta)))
    er = lax.broadcasted_iota(I32, (LANES, 4 * a_dim), 0)
    ec = lax.shift_right_logical(lax.broadcasted_iota(I32, (LANES, 4 * a_dim), 1), HEAD_SHIFT)
    expand = jnp.where(er == ec, 1.0, 0.0).astype(BF16)
    wide = _dot_exact_rhs(narrow, expand)
    pos_r = lax.broadcasted_iota(I32, (rows, a_dim), 0) & (DELTA_CHUNK - 1)
    pos_c = lax.broadcasted_iota(I32, (rows, a_dim), 1) & (HEAD_DIM - 1)
    diag = jnp.where(pos_r == pos_c, 1.0, 0.0)
    for d_i in range(2):
        gx = wide[:, 2 * d_i * a_dim:(2 * d_i + 1) * a_dim]
        bx = wide[:, (2 * d_i + 1) * a_dim:(2 * d_i + 2) * a_dim]
        g_col = _dot_exact_lhs(ones_chunk, gx * diag)
        base = 3 * d_i * a_dim
        tab_ref[0, :, base:base + a_dim] = gx
        tab_ref[0, :, base + a_dim:base + 2 * a_dim] = bx
        tab_ref[0, :, base + 2 * a_dim:base + 3 * a_dim] = gx - g_col


def _delta_prep(p, conv_w, a_log, dt_bias, *, n_ctx, a_dim):
    bsz, t, _ = p.shape
    nh = a_dim // HEAD_DIM
    assert DELTA_CHUNK == HEAD_DIM and TILE % DELTA_CHUNK == 0 and 4 * nh <= LANES
    nct = n_ctx // TILE
    c_gate = (4 * a_dim + a_dim + 2 * LANES) // LANES
    alog = jnp.zeros((1, LANES), F32).at[0, :nh].set(a_log[0]).at[0, 2 * nh:3 * nh].set(a_log[1])
    dtb = jnp.zeros((1, LANES), F32).at[0, :nh].set(dt_bias[0]).at[0, 2 * nh:3 * nh].set(dt_bias[1])
    nblk8 = t // 8
    per = TILE // 8
    return pl.pallas_call(
        functools.partial(_delta_prep_kernel, nct=nct, a_dim=a_dim),
        grid=(bsz, t // TILE),
        in_specs=[pl.BlockSpec((1, TILE, 3 * a_dim), lambda b, i: (b, i, 0)),
                  pl.BlockSpec((1, 8, 3 * a_dim), lambda b, i: (b, jnp.maximum(i * per - 1, 0), 0)),
                  pl.BlockSpec((1, 8, 3 * a_dim), lambda b, i: (b, jnp.minimum((i + 1) * per, nblk8 - 1), 0)),
                  pl.BlockSpec((1, TILE, LANES), lambda b, i: (b, i, c_gate)),
                  pl.BlockSpec((3, 3 * a_dim), lambda b, i: (0, 0)),
                  pl.BlockSpec((1, LANES), lambda b, i: (0, 0)),
                  pl.BlockSpec((1, LANES), lambda b, i: (0, 0))],
        out_specs=[pl.BlockSpec((1, TILE, 3 * a_dim), lambda b, i: (b, i, 0)),
                   pl.BlockSpec((1, TILE, 6 * a_dim), lambda b, i: (b, i, 0))],
        out_shape=[jax.ShapeDtypeStruct((bsz, t, 3 * a_dim), F32),
                   jax.ShapeDtypeStruct((bsz, t, 6 * a_dim), F32)],
        scratch_shapes=[pltpu.VMEM((TILE + 16, 3 * a_dim), F32)],
        compiler_params=_cparams("arbitrary", "arbitrary"),
        name="delta_prep",
    )(p, p, p, p, conv_w, alog, dtb)


PACK = 4


def _block_diag(x, mask):
    return jnp.concatenate([x] * PACK, axis=0) * mask


def _delta_chunk(q, k, v, g, beta, dm, s, upper, mask, tri_incl, tri_strict, eye):
    lc = q.shape[0]
    eg = jnp.exp(g)
    g_last = g[0:1] if upper else g[lc - 1:lc]
    gam = jnp.exp(jnp.where(tri_incl, dm, -jnp.inf))
    k_bd = _block_diag(k, mask).astype(BF16)
    kk = _dot_t(k.astype(BF16), k_bd)
    qk = _dot_t(q.astype(BF16), k_bd)
    m = jnp.where(tri_strict, kk * beta * gam, 0.0)
    a_qk = qk * gam
    n = -m
    x = eye + n
    n = _dot_hi(n, _block_diag(n, mask))
    steps = int(math.log2(lc)) - 1
    for it in range(steps):
        if it < steps - 1:
            both = _dot_hi(n, jnp.concatenate([_block_diag(x, mask), _block_diag(n, mask)], axis=1))
            x = x + both[:, :x.shape[1]]
            n = both[:, x.shape[1]:]
        else:
            x = x + _dot_hi(n, _block_diag(x, mask))
    rhs = jnp.concatenate([_block_diag(v * beta, mask), _block_diag(k * beta * eg, mask)], axis=1)
    uw = _dot_hi(x, rhs)
    u, w = uw[:, :q.shape[1]], uw[:, q.shape[1]:]
    s_bf = s.astype(BF16)
    v_new = u - _dot(w.astype(BF16), s_bf)
    o = _dot((q * eg).astype(BF16), s_bf) + _dot(a_qk.astype(BF16), _block_diag(v_new, mask).astype(BF16))
    k_dec = (k * jnp.exp(g_last - g)).astype(BF16)
    kv = lax.dot_general(k_dec, v_new.astype(BF16), (((0,), (0,)), ((), ())), preferred_element_type=F32)
    s_new = s * jnp.exp(g_last) + mask * kv
    return o, s_new


def _delta_scan_kernel(xf_ref, xb_ref, tf_ref, tb_ref, of_ref, ob_ref, s_ref, *, a_dim):
    c = pl.program_id(1)

    @pl.when(c == 0)
    def _():
        s_ref[...] = jnp.zeros_like(s_ref)

    lc = DELTA_CHUNK
    width = PACK * HEAD_DIM
    mask = _same_head(PACK * lc, width)
    ri = lax.broadcasted_iota(I32, (lc, width), 0)
    ci = lax.broadcasted_iota(I32, (lc, width), 1) & (HEAD_DIM - 1)
    eye = jnp.where(ri == ci, 1.0, 0.0)
    for d_i, (x_ref, t_ref, o_ref) in enumerate(((xf_ref, tf_ref, of_ref), (xb_ref, tb_ref, ob_ref))):
        upper = d_i == 1
        tri_incl = (ci >= ri) if upper else (ci <= ri)
        tri_strict = (ci > ri) if upper else (ci < ri)
        for grp in range(a_dim // width):
            lo, hi = grp * width, (grp + 1) * width
            q = x_ref[0, :, lo:hi]
            k = x_ref[0, :, a_dim + lo:a_dim + hi]
            v = x_ref[0, :, 2 * a_dim + lo:2 * a_dim + hi]
            g = t_ref[0, :, lo:hi]
            beta = t_ref[0, :, a_dim + lo:a_dim + hi]
            dm = t_ref[0, :, 2 * a_dim + lo:2 * a_dim + hi]
            idx = d_i * (a_dim // width) + grp
            o, s_new = _delta_chunk(q, k, v, g, beta, dm, s_ref[idx], upper, mask, tri_incl, tri_strict, eye)
            s_ref[idx] = s_new
            o_ref[0, :, lo:hi] = o


def _delta_scan(qkvn, tab, *, n_ctx, a_dim):
    bsz, t, _ = qkvn.shape
    lc = DELTA_CHUNK
    nc, ncc = t // lc, n_ctx // lc
    fwd = lambda b, c: (b, c, 0)
    bwd = lambda b, c: (b, jnp.where(c < ncc, ncc - 1 - c, nc - 1 - (c - ncc)), 0)
    bwd_tab = lambda b, c: (b, jnp.where(c < ncc, ncc - 1 - c, nc - 1 - (c - ncc)), 1)
    return pl.pallas_call(
        functools.partial(_delta_scan_kernel, a_dim=a_dim),
        grid=(bsz, nc),
        in_specs=[pl.BlockSpec((1, lc, 3 * a_dim), fwd),
                  pl.BlockSpec((1, lc, 3 * a_dim), bwd),
                  pl.BlockSpec((1, lc, 3 * a_dim), fwd),
                  pl.BlockSpec((1, lc, 3 * a_dim), bwd_tab)],
        out_specs=[pl.BlockSpec((1, lc, a_dim), fwd),
                   pl.BlockSpec((1, lc, a_dim), bwd)],
        out_shape=[jax.ShapeDtypeStruct((bsz, t, a_dim), F32),
                   jax.ShapeDtypeStruct((bsz, t, a_dim), F32)],
        scratch_shapes=[pltpu.VMEM((2 * a_dim // (PACK * HEAD_DIM), PACK * HEAD_DIM, PACK * HEAD_DIM), F32)],
        compiler_params=_cparams("arbitrary", "arbitrary"),
        name="delta_scan",
    )(qkvn, qkvn, tab, tab)


def kernel(x, c, ctx, c_ctx, ada_w, ada_b, ln1_g, ln2_g, even_w_in, even_conv_w, even_a_log, even_dt_bias, even_out_norm_g, even_q_norm_g, even_k_norm_g, even_w_out, odd_w_in, odd_rpb, odd_w_out, router_w, expert_w_gate, expert_w_up, expert_w_down, final_norm_g):
    bsz, n_lat, d = x.shape
    n_ctx = ctx.shape[1]
    t = n_ctx + n_lat
    nct = n_ctx // TILE
    h = jnp.concatenate([ctx, x], axis=1)
    rows = jnp.zeros((8, d), F32).at[0].set(c_ctx).at[1:1 + bsz].set(c)

    def mod_of(layer):
        m = _adaln(rows, ada_w[layer], ada_b[layer]).reshape(8, 6, d)
        return jnp.stack([jnp.broadcast_to(m[0], (bsz, 6, d)), m[1:1 + bsz]], axis=1)

    mod = mod_of(0)
    a_dim = d // 2
    a_heads = a_dim // HEAD_DIM
    w_in = even_w_in[0]
    o_qkv, o_z, o_g = 0, 3 * a_dim, 4 * a_dim
    o_bq = o_g + 4 * a_heads
    kv_dim = (w_in.shape[1] - o_bq - a_dim) // 2
    o_bk, o_bv = o_bq + a_dim, o_bq + a_dim + kv_dim
    gates_w = jnp.zeros((d, LANES), F32).at[:, :4 * a_heads].set(w_in[:, o_g:o_bq])
    w_perm = jnp.concatenate([w_in[:, :o_g], w_in[:, o_bq:], gates_w], axis=1).astype(BF16)
    p = _modproj(h, mod, ln1_g[0], w_perm, nct, F32)
    qkvn, gexp = _delta_prep(p, even_conv_w[0], even_a_log[0], even_dt_bias[0], n_ctx=n_ctx, a_dim=a_dim)
    o_delta = _delta_scan(qkvn, gexp, n_ctx=n_ctx, a_dim=a_dim)
    cos, sin = _rope_tables(n_ctx, n_lat)
    qb, kb, vb = _gqa_prep(p, cos, sin, even_q_norm_g[0], even_k_norm_g[0], a_dim=a_dim, kv_dim=kv_dim)
    o_attn = _gqa_attention(qb, kb, vb, n_ctx=n_ctx)
    h = _outproj_even(h, mod, o_delta, p, o_attn, even_out_norm_g[0], even_w_out[0].astype(BF16), nct, a_dim=a_dim)
    h = _moe(h, mod, ln2_g[0], router_w[0], expert_w_gate[0], expert_w_up[0], expert_w_down[0], final_norm_g,
             n_ctx=n_ctx, route_ctx=True, final_norm=False)

    mod = mod_of(1)
    p = _modproj(h, mod, ln1_g[1], odd_w_in[0].astype(BF16), nct, BF16)
    o_na = _natten(p, _natten_bias(odd_rpb[0]), n_ctx=n_ctx, d=d)
    h_lat = _outproj_odd(h, mod, o_na, odd_w_out[0].astype(BF16), nct)
    return _moe(h_lat, mod, ln2_g[1], router_w[1], expert_w_gate[1], expert_w_up[1], expert_w_down[1], final_norm_g,
                n_ctx=0, route_ctx=False, final_norm=True)
```

```python
import functools
import math

import numpy as np
import jax
import jax.numpy as jnp
from jax import lax
from jax.experimental import pallas as pl
from jax.experimental.pallas import tpu as pltpu

F32, BF16, I32 = jnp.float32, jnp.bfloat16, jnp.int32

HEAD_DIM = 64
GRID_W = 64
DELTA_CHUNK = 64
NA_ROWS = 8
NA_COLS = 16
ROPE_THETA = 10000.0
EC_CAPACITY = 2
EPS = 1e-6
LOG2_E = math.log2(math.e)

LANES = 128
TILE = 256
WIN = 384
VMEM_LIMIT = 56 * 1024 * 1024


def _cparams(*sem):
    return pltpu.CompilerParams(dimension_semantics=sem, vmem_limit_bytes=VMEM_LIMIT)


def _split3(a):
    a1 = a.astype(BF16)
    r1 = a - a1.astype(F32)
    a2 = r1.astype(BF16)
    a3 = (r1 - a2.astype(F32)).astype(BF16)
    return a1, a2, a3


def _dot(a, b):
    return jnp.dot(a, b, preferred_element_type=F32)


def _dot_t(a, b):
    return lax.dot_general(a, b, (((1,), (1,)), ((), ())), preferred_element_type=F32)


def _dot_exact_rhs(a, b_bf16):
    a1, a2, a3 = _split3(a)
    return _dot(a1, b_bf16) + _dot(a2, b_bf16) + _dot(a3, b_bf16)


def _dot_hi(a, b):
    a1 = a.astype(BF16)
    a2 = (a - a1.astype(F32)).astype(BF16)
    b1 = b.astype(BF16)
    b2 = (b - b1.astype(F32)).astype(BF16)
    return _dot(a1, b1) + _dot(a1, b2) + _dot(a2, b1)


def _sigmoid(x):
    return 1.0 / (1.0 + jnp.exp(-x))


def _silu(x):
    return x * _sigmoid(x)


def _modulate(x, g, shift, scale):
    ms = jnp.mean(x * x, axis=-1, keepdims=True)
    return x * lax.rsqrt(ms + EPS) * g * (1.0 + scale) + shift


def _adaln_kernel(s_ref, w_ref, b_ref, o_ref):
    s = _silu(s_ref[...])
    o_ref[...] = _dot_hi(s, w_ref[...]) + b_ref[...]


def _adaln(rows, w, b):
    d, n = w.shape
    tn = n // 4
    return pl.pallas_call(
        _adaln_kernel,
        grid=(n // tn,),
        in_specs=[pl.BlockSpec((8, d), lambda j: (0, 0)),
                  pl.BlockSpec((d, tn), lambda j: (0, j)),
                  pl.BlockSpec((1, tn), lambda j: (0, j))],
        out_specs=pl.BlockSpec((8, tn), lambda j: (0, j)),
        out_shape=jax.ShapeDtypeStruct((8, n), F32),
        compiler_params=_cparams("arbitrary"),
        name="adaln",
    )(rows, w, b.reshape(1, n))


def _modproj_kernel(h_ref, mod_ref, g_ref, w_ref, o_ref):
    mod = mod_ref[0, 0]
    a = _modulate(h_ref[0], g_ref[...], mod[0:1], mod[1:2])
    o_ref[0] = _dot(a.astype(BF16), w_ref[...]).astype(o_ref.dtype)


def _modproj(h, mod, g, w_bf16, nct, out_dtype):
    bsz, t, d = h.shape
    n = w_bf16.shape[1]
    return pl.pallas_call(
        _modproj_kernel,
        grid=(bsz, t // TILE),
        in_specs=[pl.BlockSpec((1, TILE, d), lambda b, i: (b, i, 0)),
                  pl.BlockSpec((1, 1, 6, d), lambda b, i: (b, jnp.where(i < nct, 0, 1), 0, 0)),
                  pl.BlockSpec((1, d), lambda b, i: (0, 0)),
                  pl.BlockSpec((d, n), lambda b, i: (0, 0))],
        out_specs=pl.BlockSpec((1, TILE, n), lambda b, i: (b, i, 0)),
        out_shape=jax.ShapeDtypeStruct((bsz, t, n), out_dtype),
        compiler_params=_cparams("arbitrary", "arbitrary"),
        name="modproj",
    )(h, mod, g.reshape(1, d), w_bf16)


def _router_kernel(h_ref, mod_ref, g_ref, wr_ref, a_ref, aff_ref, afft_ref, *, n_exp):
    mod = mod_ref[0, 0]
    a = _modulate(h_ref[0], g_ref[...], mod[3:4], mod[4:5])
    a_ref[0] = a.astype(BF16)
    logits = _dot_hi(a, wr_ref[...])
    lane = lax.broadcasted_iota(I32, logits.shape, 1)
    logits = jnp.where(lane < n_exp, logits, -jnp.inf)
    p = jnp.exp(logits - jnp.max(logits, axis=-1, keepdims=True))
    aff = p / jnp.sum(p, axis=-1, keepdims=True)
    aff_ref[0] = aff
    afft_ref[0] = aff.T


def _router(h, mod, g, w_router, nct):
    bsz, t, d = h.shape
    n_exp = w_router.shape[1]
    wr = jnp.zeros((d, LANES), F32).at[:, :n_exp].set(w_router)
    return pl.pallas_call(
        functools.partial(_router_kernel, n_exp=n_exp),
        grid=(bsz, t // TILE),
        in_specs=[pl.BlockSpec((1, TILE, d), lambda b, i: (b, i, 0)),
                  pl.BlockSpec((1, 1, 6, d), lambda b, i: (b, jnp.where(i < nct, 0, 1), 0, 0)),
                  pl.BlockSpec((1, d), lambda b, i: (0, 0)),
                  pl.BlockSpec((d, LANES), lambda b, i: (0, 0))],
        out_specs=[pl.BlockSpec((1, TILE, d), lambda b, i: (b, i, 0)),
                   pl.BlockSpec((1, TILE, LANES), lambda b, i: (b, i, 0)),
                   pl.BlockSpec((1, LANES, TILE), lambda b, i: (b, 0, i))],
        out_shape=[jax.ShapeDtypeStruct((bsz, t, d), BF16),
                   jax.ShapeDtypeStruct((bsz, t, LANES), F32),
                   jax.ShapeDtypeStruct((bsz, LANES, t), F32)],
        compiler_params=_cparams("arbitrary", "arbitrary"),
        name="router",
    )(h, mod, g.reshape(1, d), wr)


def _select_kernel(afft_ref, pos_ref, post_ref, cb_ref, cnt_ref, sel_ref, bucket_ref, *, n_exp, n_ctx, k_ctx, k_lat, t):
    tok = lax.broadcasted_iota(I32, (n_exp, t), 1)
    is_ctx = tok < n_ctx
    min_normal = 0x00800000

    def choose(in_set, k):
        def vals():
            return jnp.where(in_set, afft_ref[0][:n_exp], -1.0)

        def bit_step(it, thr):
            cand = thr | jnp.left_shift(jnp.int32(1), 30 - it)
            cnt = jnp.sum(jnp.where(vals() >= pltpu.bitcast(cand, F32), 1, 0), axis=1, keepdims=True)
            return jnp.where(cnt >= k, cand, thr)
        thr = lax.fori_loop(0, 31, bit_step, jnp.zeros((n_exp, 1), I32))
        lo = pltpu.bitcast(thr, F32)
        hi = pltpu.bitcast(jnp.maximum(thr + 1, min_normal), F32)
        v = vals()
        above = v >= hi
        sel_ref[...] = jnp.where(above, 1.0, sel_ref[...])
        bucket_ref[...] = jnp.where((v >= lo) & jnp.logical_not(above), 1.0, 0.0)
        need0 = k - jnp.sum(jnp.where(above, 1, 0), axis=1, keepdims=True)

        def peel(need):
            in_bucket = bucket_ref[...] > 0.0
            v = vals()
            top = jnp.max(jnp.where(in_bucket, v, -1.0), axis=1, keepdims=True)
            first = jnp.min(jnp.where(in_bucket & (v == top), tok, t), axis=1, keepdims=True)
            pick = (tok == first) & (need > 0)
            sel_ref[...] = jnp.where(pick, 1.0, sel_ref[...])
            bucket_ref[...] = jnp.where(pick, 0.0, bucket_ref[...])
            return need - 1
        lax.while_loop(lambda need: jnp.max(need) > 0, peel, need0)

    sel_ref[...] = jnp.zeros_like(sel_ref)
    choose(jnp.logical_not(is_ctx), k_lat)
    if k_ctx > 0:
        choose(is_ctx, k_ctx)

    rr = lax.broadcasted_iota(I32, (LANES, LANES), 0)
    cc = lax.broadcasted_iota(I32, (LANES, LANES), 1)
    upper = jnp.where(rr <= cc, 1.0, 0.0).astype(BF16)
    carry = jnp.zeros((n_exp, 1), F32)
    for blk in range(t // LANES):
        c = _dot(sel_ref[:, blk * LANES:(blk + 1) * LANES].astype(BF16), upper) + carry
        cnt_ref[:, blk * LANES:(blk + 1) * LANES] = c
        carry = c[:, LANES - 1:LANES]
    posinc = cnt_ref[...].astype(I32)
    pos = jnp.where(sel_ref[...] > 0.0, posinc - 1, -1)
    pos_ref[0] = pos

    lane = lax.broadcasted_iota(I32, (n_exp, LANES), 1)
    cb = jnp.zeros((n_exp, LANES), I32)
    for i in range(1, t // TILE):
        cb = jnp.where(lane == i, posinc[:, i * TILE - 1:i * TILE], cb)
    cb_ref[0] = cb

    padded = jnp.concatenate([pos.astype(F32), jnp.full((LANES - n_exp, t), -1.0, F32)], axis=0)
    for i in range(t // TILE):
        post_ref[0, i * TILE:(i + 1) * TILE, :] = padded[:, i * TILE:(i + 1) * TILE].T.astype(I32)


def _select(afft, n_exp, n_ctx, k_ctx, k_lat):
    bsz, _, t = afft.shape
    assert n_ctx % LANES == 0 and t % TILE == 0 and t // TILE <= LANES
    return pl.pallas_call(
        functools.partial(_select_kernel, n_exp=n_exp, n_ctx=n_ctx, k_ctx=k_ctx, k_lat=k_lat, t=t),
        grid=(bsz,),
        in_specs=[pl.BlockSpec((1, LANES, t), lambda b: (b, 0, 0))],
        out_specs=[pl.BlockSpec((1, n_exp, t), lambda b: (b, 0, 0)),
                   pl.BlockSpec((1, t, LANES), lambda b: (b, 0, 0)),
                   pl.BlockSpec((1, n_exp, LANES), lambda b: (b, 0, 0))],
        out_shape=[jax.ShapeDtypeStruct((bsz, n_exp, t), I32),
                   jax.ShapeDtypeStruct((bsz, t, LANES), I32),
                   jax.ShapeDtypeStruct((bsz, n_exp, LANES), I32)],
        scratch_shapes=[pltpu.VMEM((n_exp, t), F32)] * 3,
        compiler_params=_cparams("arbitrary"),
        name="select",
    )(afft)


def _ffn_rows(cap_tot):
    for rc in range(min(cap_tot, 512), 7, -8):
        if cap_tot % rc == 0:
            return rc
    raise ValueError(cap_tot)


def _expert_kernel(cb_ref, pos_ref, a_ref, wg_ref, wu_ref, wd_ref, y_ref, x_ref, *, n_exp, ntiles, cap_tot):
    e, b, i = pl.program_id(0), pl.program_id(1), pl.program_id(2)

    @pl.when(i == 0)
    def _():
        x_ref[...] = jnp.zeros_like(x_ref)

    start = cb_ref[(b * n_exp + e) * ntiles + i]
    start_al = pl.multiple_of((start // LANES) * LANES, LANES)
    rel = pos_ref[0, 0, 0] - start_al
    row = lax.broadcasted_iota(I32, (WIN, TILE), 0)
    onehot = jnp.where(row == rel, 1.0, 0.0).astype(BF16)
    x_ref[pl.ds(start_al, WIN), :] += _dot(onehot, a_ref[0])

    @pl.when(i == ntiles - 1)
    def _():
        rc = _ffn_rows(cap_tot)
        wg = wg_ref[0].astype(BF16)
        wu = wu_ref[0].astype(BF16)
        wd = wd_ref[0].astype(BF16)
        for r0 in range(0, cap_tot, rc):
            xc = x_ref[r0:r0 + rc, :].astype(BF16)
            hid = _silu(_dot(xc, wg)) * _dot(xc, wu)
            y_ref[0, 0, r0:r0 + rc, :] = _dot(hid.astype(BF16), wd)
        y_ref[0, 0, cap_tot:, :] = jnp.zeros((WIN, y_ref.shape[-1]), F32)


def _experts(cb_flat, pos, a, wg, wu, wd, cap_tot):
    bsz, n_exp, t = pos.shape
    d = a.shape[-1]
    ff = wg.shape[-1]
    ntiles = t // TILE
    rows = cap_tot + WIN
    pos5 = pos.reshape(bsz, n_exp, ntiles, 1, TILE)
    grid_spec = pltpu.PrefetchScalarGridSpec(
        num_scalar_prefetch=1,
        grid=(n_exp, bsz, ntiles),
        in_specs=[pl.BlockSpec((1, 1, 1, 1, TILE), lambda e, b, i, cb: (b, e, i, 0, 0)),
                  pl.BlockSpec((1, TILE, d), lambda e, b, i, cb: (b, i, 0)),
                  pl.BlockSpec((1, d, ff), lambda e, b, i, cb: (e, 0, 0)),
                  pl.BlockSpec((1, d, ff), lambda e, b, i, cb: (e, 0, 0)),
                  pl.BlockSpec((1, ff, d), lambda e, b, i, cb: (e, 0, 0))],
        out_specs=pl.BlockSpec((1, 1, rows, d), lambda e, b, i, cb: (b, e, 0, 0)),
        scratch_shapes=[pltpu.VMEM((rows, d), F32)],
    )
    return pl.pallas_call(
        functools.partial(_expert_kernel, n_exp=n_exp, ntiles=ntiles, cap_tot=cap_tot),
        grid_spec=grid_spec,
        out_shape=jax.ShapeDtypeStruct((bsz, n_exp, rows, d), F32),
        compiler_params=_cparams("arbitrary", "arbitrary", "arbitrary"),
        name="experts",
    )(cb_flat, pos5, a, wg, wu, wd)


def _combine(cb_flat, h, mod, aff, post, y, g_final, nct, final_norm):
    bsz, t, d = h.shape
    n_exp = y.shape[1]
    ntiles = t // TILE
    first = nct if final_norm else 0
    grid_spec = pltpu.PrefetchScalarGridSpec(
        num_scalar_prefetch=1,
        grid=(bsz, ntiles - first),
        in_specs=[pl.BlockSpec((1, TILE, d), lambda b, i, cb: (b, i + first, 0)),
                  pl.BlockSpec((1, 1, 6, d), lambda b, i, cb: (b, jnp.where(i + first < nct, 0, 1), 0, 0)),
                  pl.BlockSpec((1, TILE, LANES), lambda b, i, cb: (b, i + first, 0)),
                  pl.BlockSpec((1, TILE, LANES), lambda b, i, cb: (b, i + first, 0)),
                  pl.BlockSpec(memory_space=pl.ANY),
                  pl.BlockSpec((1, d), lambda b, i, cb: (0, 0))],
        out_specs=pl.BlockSpec((1, TILE, d), lambda b, i, cb: (b, i, 0)),
        scratch_shapes=[pltpu.VMEM((2, WIN, d), F32), pltpu.SemaphoreType.DMA((2,))],
    )

    def body(cb_ref, h_ref, mod_ref, aff_ref, post_ref, y_hbm, g_ref, o_ref, ybuf, sem):
        _combine_tile(cb_ref, h_ref, mod_ref, aff_ref, post_ref, y_hbm, g_ref, o_ref, ybuf, sem,
                      n_exp=n_exp, ntiles=ntiles, first=first, final_norm=final_norm)

    return pl.pallas_call(
        body,
        grid_spec=grid_spec,
        out_shape=jax.ShapeDtypeStruct((bsz, t - first * TILE, d), F32),
        compiler_params=_cparams("arbitrary", "arbitrary"),
        name="combine",
    )(cb_flat, h, mod, aff, post, y, g_final.reshape(1, d))


def _combine_tile(cb_ref, h_ref, mod_ref, aff_ref, post_ref, y_hbm, g_ref, o_ref, ybuf, sem,
                  *, n_exp, ntiles, first, final_norm):
    b, i = pl.program_id(0), pl.program_id(1) + first

    def window(e, slot):
        start = cb_ref[(b * n_exp + e) * ntiles + i]
        start_al = pl.multiple_of((start // LANES) * LANES, LANES)
        copy = pltpu.make_async_copy(y_hbm.at[b, e, pl.ds(start_al, WIN), :], ybuf.at[slot], sem.at[slot])
        return copy, start_al

    window(0, 0)[0].start()
    acc = jnp.zeros(o_ref.shape[1:], F32)
    lane = lax.broadcasted_iota(I32, (TILE, WIN), 1)
    post = post_ref[0]
    aff = aff_ref[0]
    for e in range(n_exp):
        slot = e % 2
        if e + 1 < n_exp:
            window(e + 1, 1 - slot)[0].start()
        copy, start_al = window(e, slot)
        copy.wait()
        rel = post[:, e:e + 1] - start_al
        onehot = jnp.where(lane == rel, 1.0, 0.0).astype(BF16)
        acc = acc + aff[:, e:e + 1] * _dot(onehot, ybuf[slot].astype(BF16))
    mod = mod_ref[0, 0]
    out = h_ref[0] + mod[5:6] * acc
    if final_norm:
        ms = jnp.mean(out * out, axis=-1, keepdims=True)
        out = out * lax.rsqrt(ms + EPS) * g_ref[...]
    o_ref[0] = out


def _moe(h, mod, ln2_g, w_router, wg, wu, wd, g_final, *, n_ctx, route_ctx, final_norm):
    bsz, t, d = h.shape
    n_exp = w_router.shape[1]
    nct = n_ctx // TILE
    n_lat = t - n_ctx
    k_lat = max(1, EC_CAPACITY * n_lat // n_exp)
    k_ctx = max(1, EC_CAPACITY * n_ctx // n_exp) if route_ctx else 0
    a, aff, afft = _router(h, mod, ln2_g, w_router, nct)
    pos, post, cb = _select(afft, n_exp, n_ctx, k_ctx, k_lat)
    cb_flat = cb[:, :, :t // TILE].reshape(-1)
    y = _experts(cb_flat, pos, a, wg, wu, wd, k_ctx + k_lat)
    return _combine(cb_flat, h, mod, aff, post, y, g_final, nct, final_norm)


HEAD_SHIFT = int(math.log2(HEAD_DIM))


def _same_head(rows, cols):
    r = lax.shift_right_logical(lax.broadcasted_iota(I32, (rows, cols), 0), HEAD_SHIFT)
    c = lax.shift_right_logical(lax.broadcasted_iota(I32, (rows, cols), 1), HEAD_SHIFT)
    return jnp.where(r == c, 1.0, 0.0)


def _head_sum(x2):
    n = x2.shape[-1]
    return _dot_exact_rhs(x2, _same_head(n, n).astype(BF16))


def _outproj_even_kernel(h_ref, mod_ref, of_ref, ob_ref, z_ref, at_ref, g_ref, w_ref, o_ref, *, a_dim):
    mod = mod_ref[0, 0]
    o = of_ref[0] + ob_ref[0]
    ms = _head_sum(o * o) * (1.0 / HEAD_DIM)
    ya = o * lax.rsqrt(ms + EPS) * g_ref[...] * _silu(z_ref[0])
    y = _dot(ya.astype(BF16), w_ref[:a_dim, :]) + _dot(at_ref[0], w_ref[a_dim:, :])
    o_ref[0] = h_ref[0] + mod[2:3] * y


def _outproj_even(h, mod, o_delta, p, o_attn, out_norm_g, w_bf16, nct, *, a_dim):
    bsz, t, d = h.shape
    o_f, o_b = o_delta
    g = jnp.tile(out_norm_g, a_dim // HEAD_DIM).reshape(1, a_dim)
    tok = lambda b, i: (b, i, 0)
    return pl.pallas_call(
        functools.partial(_outproj_even_kernel, a_dim=a_dim),
        grid=(bsz, t // TILE),
        in_specs=[pl.BlockSpec((1, TILE, d), tok),
                  pl.BlockSpec((1, 1, 6, d), lambda b, i: (b, jnp.where(i < nct, 0, 1), 0, 0)),
                  pl.BlockSpec((1, TILE, a_dim), tok),
                  pl.BlockSpec((1, TILE, a_dim), tok),
                  pl.BlockSpec((1, TILE, a_dim), lambda b, i: (b, i, 3)),
                  pl.BlockSpec((1, TILE, a_dim), tok),
                  pl.BlockSpec((1, a_dim), lambda b, i: (0, 0)),
                  pl.BlockSpec((d, d), lambda b, i: (0, 0))],
        out_specs=pl.BlockSpec((1, TILE, d), tok),
        out_shape=jax.ShapeDtypeStruct((bsz, t, d), F32),
        compiler_params=_cparams("arbitrary", "arbitrary"),
        name="outproj_even",
    )(h, mod, o_f, o_b, p, o_attn, g, w_bf16)


def _outproj_odd_kernel(h_ref, mod_ref, a_ref, w_ref, o_ref):
    mod = mod_ref[0, 0]
    o_ref[0] = h_ref[0] + mod[2:3] * _dot(a_ref[0], w_ref[...])


def _outproj_odd(h, mod, o_na, w_bf16, nct):
    bsz, t, d = h.shape
    n_lat = o_na.shape[1]
    return pl.pallas_call(
        _outproj_odd_kernel,
        grid=(bsz, n_lat // TILE),
        in_specs=[pl.BlockSpec((1, TILE, d), lambda b, i: (b, i + nct, 0)),
                  pl.BlockSpec((1, 1, 6, d), lambda b, i: (b, 1, 0, 0)),
                  pl.BlockSpec((1, TILE, d), lambda b, i: (b, i, 0)),
                  pl.BlockSpec((d, d), lambda b, i: (0, 0))],
        out_specs=pl.BlockSpec((1, TILE, d), lambda b, i: (b, i, 0)),
        out_shape=jax.ShapeDtypeStruct((bsz, n_lat, d), F32),
        compiler_params=_cparams("arbitrary", "arbitrary"),
        name="outproj_odd",
    )(h, mod, o_na, w_bf16)


def _rope_tables(n_ctx, n_lat):
    tt = jnp.arange(n_lat)
    n_freq = HEAD_DIM // 4
    inv = ROPE_THETA ** (-jnp.arange(n_freq, dtype=F32) / n_freq)
    ang = jnp.concatenate([(tt // GRID_W).astype(F32)[:, None] * inv, (tt % GRID_W).astype(F32)[:, None] * inv], -1)
    cos = jnp.concatenate([jnp.ones((n_ctx, 2 * n_freq), F32), jnp.cos(ang)], axis=0)
    sin = jnp.concatenate([jnp.zeros((n_ctx, 2 * n_freq), F32), jnp.sin(ang)], axis=0)
    reps = LANES // HEAD_DIM
    return jnp.tile(cos, (1, 2 * reps)), jnp.tile(jnp.concatenate([-sin, sin], axis=1), (1, reps))


def _rope(x, cos, sin):
    n = x.shape[-1]
    half = HEAD_DIM // 2
    lane = lax.broadcasted_iota(I32, x.shape, 1)
    first = (lane & (HEAD_DIM - 1)) < half
    partner = jnp.where(first, pltpu.roll(x, n - half, 1), pltpu.roll(x, half, 1))
    reps = n // LANES
    cos_n = jnp.concatenate([cos] * reps, axis=1) if reps > 1 else cos
    sin_n = jnp.concatenate([sin] * reps, axis=1) if reps > 1 else sin
    return x * cos_n + partner * sin_n


def _gqa_prep_kernel(q_ref, k_ref, v_ref, cos_ref, sin_ref, gq_ref, gk_ref, qo_ref, ko_ref, vo_ref):
    cos, sin = cos_ref[...], sin_ref[...]
    q = q_ref[0]
    q = q * lax.rsqrt(_head_sum(q * q) * (1.0 / HEAD_DIM) + EPS) * gq_ref[...] * (HEAD_DIM ** -0.5 * LOG2_E)
    qo_ref[0] = _rope(q, cos, sin).astype(BF16)
    k = k_ref[0]
    k = k * lax.rsqrt(_head_sum(k * k) * (1.0 / HEAD_DIM) + EPS) * gk_ref[...]
    k = _rope(k, cos, sin).astype(BF16)
    v = v_ref[0].astype(BF16)
    for hh in range(k.shape[-1] // HEAD_DIM):
        ko_ref[0, hh] = k[:, hh * HEAD_DIM:(hh + 1) * HEAD_DIM]
        vo_ref[0, hh] = v[:, hh * HEAD_DIM:(hh + 1) * HEAD_DIM]


def _gqa_prep(p, cos, sin, q_norm_g, k_norm_g, *, a_dim, kv_dim):
    bsz, t, _ = p.shape
    q_dim = a_dim
    kvh = kv_dim // HEAD_DIM
    assert kv_dim == LANES
    c_q = (4 * a_dim) // q_dim
    c_k = (4 * a_dim + q_dim) // kv_dim
    gq = jnp.tile(q_norm_g, q_dim // HEAD_DIM).reshape(1, q_dim)
    gk = jnp.tile(k_norm_g, kvh).reshape(1, kv_dim)
    return pl.pallas_call(
        _gqa_prep_kernel,
        grid=(bsz, t // TILE),
        in_specs=[pl.BlockSpec((1, TILE, q_dim), lambda b, i: (b, i, c_q)),
                  pl.BlockSpec((1, TILE, kv_dim), lambda b, i: (b, i, c_k)),
                  pl.BlockSpec((1, TILE, kv_dim), lambda b, i: (b, i, c_k + 1)),
                  pl.BlockSpec((TILE, LANES), lambda b, i: (i, 0)),
                  pl.BlockSpec((TILE, LANES), lambda b, i: (i, 0)),
                  pl.BlockSpec((1, q_dim), lambda b, i: (0, 0)),
                  pl.BlockSpec((1, kv_dim), lambda b, i: (0, 0))],
        out_specs=[pl.BlockSpec((1, TILE, q_dim), lambda b, i: (b, i, 0)),
                   pl.BlockSpec((1, kvh, TILE, HEAD_DIM), lambda b, i: (b, 0, i, 0)),
                   pl.BlockSpec((1, kvh, TILE, HEAD_DIM), lambda b, i: (b, 0, i, 0))],
        out_shape=[jax.ShapeDtypeStruct((bsz, t, q_dim), BF16),
                   jax.ShapeDtypeStruct((bsz, kvh, t, HEAD_DIM), BF16),
                   jax.ShapeDtypeStruct((bsz, kvh, t, HEAD_DIM), BF16)],
        compiler_params=_cparams("arbitrary", "arbitrary"),
        name="gqa_prep",
    )(p, p, p, cos, sin, gq, gk)


SUB_K = 256
Q_SLAB = 256


def _gqa_kernel(q_ref, k_ref, v_ref, o_ref, qs_ref, m_ref, l_ref, acc_ref, *, n_ctx, nct, tq, tk, group):
    i, kv = pl.program_id(2), pl.program_id(3)

    @pl.when(kv == 0)
    def _():
        m_ref[...] = jnp.full_like(m_ref, -jnp.inf)
        l_ref[...] = jnp.zeros_like(l_ref)
        acc_ref[...] = jnp.zeros_like(acc_ref)
        q = q_ref[0]
        for g in range(group):
            qs_ref[g * tq:(g + 1) * tq, :] = q[:, g * HEAD_DIM:(g + 1) * HEAD_DIM]

    def attend(keys, vals, n_valid):
        for r0 in range(0, group * tq, Q_SLAB):
            rows = slice(r0, r0 + Q_SLAB)
            s = _dot_t(qs_ref[rows, :], keys)
            if n_valid is not None:
                s = jnp.where(lax.broadcasted_iota(I32, s.shape, 1) < n_valid, s, -jnp.inf)
            m_old = m_ref[rows, :]
            m_new = jnp.maximum(m_old, jnp.max(s, axis=-1, keepdims=True))
            alpha = jnp.exp2(m_old - m_new)
            p = jnp.exp2(s - m_new)
            l_ref[rows, :] = alpha * l_ref[rows, :] + jnp.sum(p, axis=-1, keepdims=True)
            acc_ref[rows, :] = alpha * acc_ref[rows, :] + _dot(p.astype(BF16), vals)
            m_ref[rows, :] = m_new

    @pl.when(i >= nct)
    def _():
        attend(k_ref[0, 0], v_ref[0, 0], None)

    @pl.when((i < nct) & (kv == 0))
    def _():
        n_keys = -(-n_ctx // SUB_K) * SUB_K
        attend(k_ref[0, 0, :n_keys, :], v_ref[0, 0, :n_keys, :], n_ctx if n_keys != n_ctx else None)

    @pl.when(kv == pl.num_programs(3) - 1)
    def _():
        out = acc_ref[...] / l_ref[...]
        for g in range(group):
            o_ref[0, :, g * HEAD_DIM:(g + 1) * HEAD_DIM] = out[g * tq:(g + 1) * tq, :].astype(o_ref.dtype)


def _gqa_attention(q, k, v, *, n_ctx):
    bsz, t, q_dim = q.shape
    kvh = k.shape[1]
    group = q_dim // HEAD_DIM // kvh
    tq = TILE
    nt = t // TILE
    tk = TILE * max(m for m in range(1, 13) if nt % m == 0)
    assert n_ctx <= tk and tk % SUB_K == 0
    return pl.pallas_call(
        functools.partial(_gqa_kernel, n_ctx=n_ctx, nct=n_ctx // tq, tq=tq, tk=tk, group=group),
        grid=(bsz, kvh, t // tq, t // tk),
        in_specs=[pl.BlockSpec((1, tq, group * HEAD_DIM), lambda b, h, i, j: (b, i, h)),
                  pl.BlockSpec((1, 1, tk, HEAD_DIM), lambda b, h, i, j: (b, h, j, 0)),
                  pl.BlockSpec((1, 1, tk, HEAD_DIM), lambda b, h, i, j: (b, h, j, 0))],
        out_specs=pl.BlockSpec((1, tq, group * HEAD_DIM), lambda b, h, i, j: (b, i, h)),
        out_shape=jax.ShapeDtypeStruct((bsz, t, q_dim), BF16),
        scratch_shapes=[pltpu.VMEM((group * tq, HEAD_DIM), BF16),
                        pltpu.VMEM((group * tq, 1), F32),
                        pltpu.VMEM((group * tq, 1), F32),
                        pltpu.VMEM((group * tq, HEAD_DIM), F32)],
        compiler_params=_cparams("arbitrary", "arbitrary", "arbitrary", "arbitrary"),
        name="gqa_attention",
    )(q, k, v)


def _natten_bias(rpb):
    n_heads, n_off, n_rel = rpb.shape
    col = np.arange(GRID_W)
    col_start = np.clip(col - NA_COLS // 2, 0, GRID_W - NA_COLS)
    valid = (col[None, :] >= col_start[:, None]) & (col[None, :] < col_start[:, None] + NA_COLS)
    rel = col[None, :] - col[:, None] + NA_COLS - 1
    onehot = ((rel[None] == np.arange(n_rel)[:, None, None]) & valid[None]).astype(np.float32)
    tab = jnp.einsum('hrk,kcd->hrcd', rpb, jnp.asarray(onehot), precision=lax.Precision.HIGHEST)
    tab = jnp.where(valid[None, None], tab, -jnp.inf)
    pairs = jnp.concatenate([tab[:, :-1], tab[:, 1:]], axis=-1)
    return pairs.reshape(n_heads // 2, 2, n_off - 1, GRID_W, 2 * GRID_W).astype(F32)


def _natten_kernel(q_ref, k_ref, v_ref, bias_ref, o_ref, *, n_ctx, n_rows, rblk):
    rb = pl.program_id(2)
    first_head = lax.broadcasted_iota(I32, (1, LANES), 1) < HEAD_DIM
    kc = k_ref[0, 0:n_ctx, :]
    vc = v_ref[0, 0:n_ctx, :]
    band = NA_ROWS * GRID_W
    for rr in range(rblk):
        rg = rb * rblk + rr
        rs = jnp.clip(rg - NA_ROWS // 2, 0, n_rows - NA_ROWS)
        variant = rs - rg + (NA_ROWS - 1)
        start = pl.multiple_of(n_ctx + rs * GRID_W, GRID_W)
        kb = k_ref[0, pl.ds(start, band), :]
        vb = v_ref[0, pl.ds(start, band), :]
        q = q_ref[0, rr * GRID_W:(rr + 1) * GRID_W, :] * (HEAD_DIM ** -0.5)
        outs = []
        for hh in range(2):
            qh = jnp.where(first_head if hh == 0 else jnp.logical_not(first_head), q, jnp.zeros_like(q))
            sc = _dot_t(qh, kc)
            bias = jnp.concatenate([bias_ref[0, hh, variant + j] for j in range(0, NA_ROWS, 2)], axis=1)
            sl = _dot_t(qh, kb) + bias
            m = jnp.maximum(jnp.max(sc, axis=-1, keepdims=True), jnp.max(sl, axis=-1, keepdims=True))
            pc = jnp.exp(sc - m)
            pb = jnp.exp(sl - m)
            l = jnp.sum(pc, axis=-1, keepdims=True) + jnp.sum(pb, axis=-1, keepdims=True)
            outs.append((_dot(pc.astype(BF16), vc) + _dot(pb.astype(BF16), vb)) / l)
        o_ref[0, rr * GRID_W:(rr + 1) * GRID_W, :] = jnp.where(first_head, outs[0], outs[1]).astype(o_ref.dtype)


def _natten(p, bias, *, n_ctx, d):
    bsz, t, _ = p.shape
    n_lat = t - n_ctx
    n_rows = n_lat // GRID_W
    rblk = TILE // GRID_W
    npairs = d // LANES
    assert n_rows >= NA_ROWS and n_ctx % TILE == 0
    return pl.pallas_call(
        functools.partial(_natten_kernel, n_ctx=n_ctx, n_rows=n_rows, rblk=rblk),
        grid=(bsz, npairs, n_lat // TILE),
        in_specs=[pl.BlockSpec((1, TILE, LANES), lambda b, hp, r: (b, n_ctx // TILE + r, hp)),
                  pl.BlockSpec((1, t, LANES), lambda b, hp, r: (b, 0, npairs + hp)),
                  pl.BlockSpec((1, t, LANES), lambda b, hp, r: (b, 0, 2 * npairs + hp)),
                  pl.BlockSpec((1, 2, 2 * NA_ROWS - 2, GRID_W, 2 * GRID_W), lambda b, hp, r: (hp, 0, 0, 0, 0))],
        out_specs=pl.BlockSpec((1, TILE, LANES), lambda b, hp, r: (b, r, hp)),
        out_shape=jax.ShapeDtypeStruct((bsz, n_lat, d), BF16),
        compiler_params=_cparams("arbitrary", "arbitrary", "arbitrary"),
        name="natten",
    )(p, p, p, bias)


def _dot_exact_lhs(a_bf16, b):
    b1, b2, b3 = _split3(b)
    return _dot(a_bf16, b1) + _dot(a_bf16, b2) + _dot(a_bf16, b3)


def _delta_prep_kernel(x_ref, prev_ref, next_ref, gate_ref, w_ref, alog_ref, dt_ref, xh_ref, gn_ref, gnt_ref, xs_ref,
                       *, nct, a_dim):
    i = pl.program_id(1)
    last = pl.num_programs(1) - 1
    x = x_ref[0]
    rows = x.shape[0]
    xs_ref[0:8, :] = prev_ref[0]
    xs_ref[8:8 + rows, :] = x
    xs_ref[8 + rows:16 + rows, :] = next_ref[0]
    r = lax.broadcasted_iota(I32, (rows, 1), 0)
    at_start = (i == 0) | (i == nct)
    at_end = (i == nct - 1) | (i == last)
    xm = jnp.where((r == 0) & at_start, 0.0, xs_ref[7:7 + rows, :])
    xp = jnp.where((r == rows - 1) & at_end, 0.0, xs_ref[9:9 + rows, :])
    w = w_ref[...]
    y = _silu(w[0:1] * xm + w[1:2] * x + w[2:3] * xp)
    q, k, v = y[:, :a_dim], y[:, a_dim:2 * a_dim], y[:, 2 * a_dim:]
    nh = a_dim // HEAD_DIM
    q = (q * lax.rsqrt(_head_sum(q * q) + EPS) * (HEAD_DIM ** -0.5)).astype(BF16)
    k = (k * lax.rsqrt(_head_sum(k * k) + EPS)).astype(BF16)
    v = v.astype(BF16)
    for kind, arr in enumerate((q, k, v)):
        for hh in range(nh):
            xh_ref[0, kind, hh] = arr[:, hh * HEAD_DIM:(hh + 1) * HEAD_DIM]

    g = gate_ref[0]
    z = g + dt_ref[...]
    softplus = jnp.maximum(z, 0.0) + jnp.log1p(jnp.exp(-jnp.abs(z)))
    log_a = -jnp.exp(alog_ref[...]) * softplus
    beta = _sigmoid(g)
    ri = lax.broadcasted_iota(I32, (rows, rows), 0)
    ci = lax.broadcasted_iota(I32, (rows, rows), 1)
    chunk_shift = int(math.log2(DELTA_CHUNK))
    same_chunk = lax.shift_right_logical(ri, chunk_shift) == lax.shift_right_logical(ci, chunk_shift)
    prefix = jnp.where(same_chunk & (ci <= ri), 1.0, 0.0).astype(BF16)
    suffix = jnp.where(same_chunk & (ci >= ri), 1.0, 0.0).astype(BF16)
    g_f = _dot_exact_lhs(prefix, log_a)
    g_b = _dot_exact_lhs(suffix, log_a)
    lane = lax.broadcasted_iota(I32, g.shape, 1)
    narrow = jnp.where(lane < nh, g_f, jnp.where(lane < 2 * nh, beta, jnp.where(lane < 3 * nh, g_b, beta)))
    gn_ref[0] = narrow
    gnt_ref[0] = narrow.T


def _delta_prep(p, conv_w, a_log, dt_bias, *, n_ctx, a_dim):
    bsz, t, _ = p.shape
    nh = a_dim // HEAD_DIM
    assert DELTA_CHUNK == HEAD_DIM and TILE % DELTA_CHUNK == 0 and 4 * nh <= LANES
    nct = n_ctx // TILE
    c_gate = (4 * a_dim + a_dim + 2 * LANES) // LANES
    alog = jnp.zeros((1, LANES), F32).at[0, :nh].set(a_log[0]).at[0, 2 * nh:3 * nh].set(a_log[1])
    dtb = jnp.zeros((1, LANES), F32).at[0, :nh].set(dt_bias[0]).at[0, 2 * nh:3 * nh].set(dt_bias[1])
    nblk8 = t // 8
    per = TILE // 8
    return pl.pallas_call(
        functools.partial(_delta_prep_kernel, nct=nct, a_dim=a_dim),
        grid=(bsz, t // TILE),
        in_specs=[pl.BlockSpec((1, TILE, 3 * a_dim), lambda b, i: (b, i, 0)),
                  pl.BlockSpec((1, 8, 3 * a_dim), lambda b, i: (b, jnp.maximum(i * per - 1, 0), 0)),
                  pl.BlockSpec((1, 8, 3 * a_dim), lambda b, i: (b, jnp.minimum((i + 1) * per, nblk8 - 1), 0)),
                  pl.BlockSpec((1, TILE, LANES), lambda b, i: (b, i, c_gate)),
                  pl.BlockSpec((3, 3 * a_dim), lambda b, i: (0, 0)),
                  pl.BlockSpec((1, LANES), lambda b, i: (0, 0)),
                  pl.BlockSpec((1, LANES), lambda b, i: (0, 0))],
        out_specs=[pl.BlockSpec((1, 3, nh, TILE, HEAD_DIM), lambda b, i: (b, 0, 0, i, 0)),
                   pl.BlockSpec((1, TILE, LANES), lambda b, i: (b, i, 0)),
                   pl.BlockSpec((1, LANES, TILE), lambda b, i: (b, 0, i))],
        out_shape=[jax.ShapeDtypeStruct((bsz, 3, nh, t, HEAD_DIM), BF16),
                   jax.ShapeDtypeStruct((bsz, t, LANES), F32),
                   jax.ShapeDtypeStruct((bsz, LANES, t), F32)],
        scratch_shapes=[pltpu.VMEM((TILE + 16, 3 * a_dim), F32)],
        compiler_params=_cparams("arbitrary", "arbitrary"),
        name="delta_prep",
    )(p, p, p, p, conv_w, alog, dtb)


def _bdot(a, b, ca, cb, hi=False):
    dims = (((ca,), (cb,)), ((0,), (0,)))
    dot = lambda x, y: lax.dot_general(x, y, dims, preferred_element_type=F32)
    if not hi:
        return dot(a.astype(BF16), b.astype(BF16))
    a1 = a.astype(BF16)
    a2 = (a - a1.astype(F32)).astype(BF16)
    b1 = b.astype(BF16)
    b2 = (b - b1.astype(F32)).astype(BF16)
    return dot(a1, b1) + dot(a1, b2) + dot(a2, b1)


def _delta_heads(q, k, v, g_col, beta_col, g_row, s, upper):
    nb, lc, _ = q.shape
    ri = lax.broadcasted_iota(I32, (nb, lc, lc), 1)
    ci = lax.broadcasted_iota(I32, (nb, lc, lc), 2)
    ahead = jnp.where(upper, ci - ri, ri - ci)
    tri_incl = ahead >= 0
    tri_strict = ahead > 0
    eye = jnp.where(ri == ci, 1.0, 0.0)
    k32, q32, v32 = k.astype(F32), q.astype(F32), v.astype(F32)
    eg = jnp.exp(g_col)
    g_last = jnp.where(upper, g_col[:, 0:1], g_col[:, lc - 1:lc])
    gam = jnp.exp(jnp.where(tri_incl, g_col - g_row, -jnp.inf))
    kq = _bdot(jnp.concatenate([k, q], axis=1), k, 2, 2)
    m = jnp.where(tri_strict, kq[:, :lc] * beta_col * gam, 0.0)
    a_qk = kq[:, lc:] * gam
    n = -m
    x = eye + n
    p = _bdot(n, n, 2, 1, hi=True)
    for _ in range(int(math.log2(lc)) - 2):
        x, p = x + _bdot(p, x, 2, 1, hi=True), _bdot(p, p, 2, 1, hi=True)
    x = x + _bdot(p, x, 2, 1, hi=True)
    u = _bdot(x, v32 * beta_col, 2, 1, hi=True)
    w = _bdot(x, k32 * (beta_col * eg), 2, 1, hi=True)
    ws_qs = _bdot(jnp.concatenate([w, q32 * eg], axis=1), s, 2, 1)
    v_new = u - ws_qs[:, :lc]
    o = ws_qs[:, lc:] + _bdot(a_qk, v_new, 2, 1)
    kv = _bdot(k32 * jnp.exp(g_last - g_col), v_new, 1, 1)
    return o, s * jnp.exp(g_last) + kv


def _delta_scan_kernel(xf_ref, xb_ref, gf_ref, gb_ref, gtf_ref, gtb_ref, of_ref, ob_ref, s_ref, *, nh):
    c = pl.program_id(1)

    @pl.when(c == 0)
    def _():
        s_ref[...] = jnp.zeros_like(s_ref)

    both = lambda f, b: jnp.concatenate([f, b], axis=0)
    q, k, v = (both(xf_ref[0, i], xb_ref[0, i]) for i in range(3))
    g_cols, b_cols, g_rows = [], [], []
    for d_i, (g_ref, gt_ref) in enumerate(((gf_ref, gtf_ref), (gb_ref, gtb_ref))):
        gn = g_ref[0]
        gt = gt_ref[0, 0]
        for hh in range(nh):
            lane_g = 2 * d_i * nh + hh
            g_cols.append(gn[:, lane_g:lane_g + 1])
            b_cols.append(gn[:, lane_g + nh:lane_g + nh + 1])
            g_rows.append(gt[lane_g:lane_g + 1, :])
    upper = lax.broadcasted_iota(I32, (2 * nh, 1, 1), 0) >= nh
    o, s_new = _delta_heads(q, k, v, jnp.stack(g_cols), jnp.stack(b_cols), jnp.stack(g_rows), s_ref[...], upper)
    s_ref[...] = s_new
    for hh in range(nh):
        of_ref[0, :, hh * HEAD_DIM:(hh + 1) * HEAD_DIM] = o[hh]
        ob_ref[0, :, hh * HEAD_DIM:(hh + 1) * HEAD_DIM] = o[nh + hh]


def _delta_scan(xh, gn, gnt, *, n_ctx):
    bsz, _, nh, t, _ = xh.shape
    lc = DELTA_CHUNK
    nc, ncc = t // lc, n_ctx // lc
    gnt4 = gnt.reshape(bsz, LANES, nc, lc).transpose(0, 2, 1, 3)
    rev = lambda c: jnp.where(c < ncc, ncc - 1 - c, nc - 1 - (c - ncc))
    return pl.pallas_call(
        functools.partial(_delta_scan_kernel, nh=nh),
        grid=(bsz, nc),
        in_specs=[pl.BlockSpec((1, 3, nh, lc, HEAD_DIM), lambda b, c: (b, 0, 0, c, 0)),
                  pl.BlockSpec((1, 3, nh, lc, HEAD_DIM), lambda b, c: (b, 0, 0, rev(c), 0)),
                  pl.BlockSpec((1, lc, LANES), lambda b, c: (b, c, 0)),
                  pl.BlockSpec((1, lc, LANES), lambda b, c: (b, rev(c), 0)),
                  pl.BlockSpec((1, 1, LANES, lc), lambda b, c: (b, c, 0, 0)),
                  pl.BlockSpec((1, 1, LANES, lc), lambda b, c: (b, rev(c), 0, 0))],
        out_specs=[pl.BlockSpec((1, lc, nh * HEAD_DIM), lambda b, c: (b, c, 0)),
                   pl.BlockSpec((1, lc, nh * HEAD_DIM), lambda b, c: (b, rev(c), 0))],
        out_shape=[jax.ShapeDtypeStruct((bsz, t, nh * HEAD_DIM), F32),
                   jax.ShapeDtypeStruct((bsz, t, nh * HEAD_DIM), F32)],
        scratch_shapes=[pltpu.VMEM((2 * nh, HEAD_DIM, HEAD_DIM), F32)],
        compiler_params=_cparams("arbitrary", "arbitrary"),
        name="delta_scan",
    )(xh, xh, gn, gn, gnt4, gnt4)


def kernel(x, c, ctx, c_ctx, ada_w, ada_b, ln1_g, ln2_g, even_w_in, even_conv_w, even_a_log, even_dt_bias, even_out_norm_g, even_q_norm_g, even_k_norm_g, even_w_out, odd_w_in, odd_rpb, odd_w_out, router_w, expert_w_gate, expert_w_up, expert_w_down, final_norm_g):
    bsz, n_lat, d = x.shape
    n_ctx = ctx.shape[1]
    t = n_ctx + n_lat
    nct = n_ctx // TILE
    h = jnp.concatenate([ctx, x], axis=1)
    rows = jnp.zeros((8, d), F32).at[0].set(c_ctx).at[1:1 + bsz].set(c)

    def mod_of(layer):
        m = _adaln(rows, ada_w[layer], ada_b[layer]).reshape(8, 6, d)
        return jnp.stack([jnp.broadcast_to(m[0], (bsz, 6, d)), m[1:1 + bsz]], axis=1)

    mod = mod_of(0)
    a_dim = d // 2
    a_heads = a_dim // HEAD_DIM
    w_in = even_w_in[0]
    o_qkv, o_z, o_g = 0, 3 * a_dim, 4 * a_dim
    o_bq = o_g + 4 * a_heads
    kv_dim = (w_in.shape[1] - o_bq - a_dim) // 2
    o_bk, o_bv = o_bq + a_dim, o_bq + a_dim + kv_dim
    gates_w = jnp.zeros((d, LANES), F32).at[:, :4 * a_heads].set(w_in[:, o_g:o_bq])
    w_perm = jnp.concatenate([w_in[:, :o_g], w_in[:, o_bq:], gates_w], axis=1).astype(BF16)
    p = _modproj(h, mod, ln1_g[0], w_perm, nct, F32)
    xh, gn, gnt = _delta_prep(p, even_conv_w[0], even_a_log[0], even_dt_bias[0], n_ctx=n_ctx, a_dim=a_dim)
    o_delta = _delta_scan(xh, gn, gnt, n_ctx=n_ctx)
    cos, sin = _rope_tables(n_ctx, n_lat)
    qb, kb, vb = _gqa_prep(p, cos, sin, even_q_norm_g[0], even_k_norm_g[0], a_dim=a_dim, kv_dim=kv_dim)
    o_attn = _gqa_attention(qb, kb, vb, n_ctx=n_ctx)
    h = _outproj_even(h, mod, o_delta, p, o_attn, even_out_norm_g[0], even_w_out[0].astype(BF16), nct, a_dim=a_dim)
    h = _moe(h, mod, ln2_g[0], router_w[0], expert_w_gate[0], expert_w_up[0], expert_w_down[0], final_norm_g,
             n_ctx=n_ctx, route_ctx=True, final_norm=False)

    mod = mod_of(1)
    p = _modproj(h, mod, ln1_g[1], odd_w_in[0].astype(BF16), nct, BF16)
    o_na = _natten(p, _natten_bias(odd_rpb[0]), n_ctx=n_ctx, d=d)
    h_lat = _outproj_odd(h, mod, o_na, odd_w_out[0].astype(BF16), nct)
    return _moe(h_lat, mod, ln2_g[1], router_w[1], expert_w_gate[1], expert_w_up[1], expert_w_down[1], final_norm_g,
                n_ctx=0, route_ctx=False, final_norm=True)
```

```python
import functools
import math

import numpy as np
import jax
import jax.numpy as jnp
from jax import lax
from jax.experimental import pallas as pl
from jax.experimental.pallas import tpu as pltpu

F32, BF16, I32 = jnp.float32, jnp.bfloat16, jnp.int32

HEAD_DIM = 64
GRID_W = 64
DELTA_CHUNK = 64
NA_ROWS = 8
NA_COLS = 16
ROPE_THETA = 10000.0
EC_CAPACITY = 2
EPS = 1e-6
LOG2_E = math.log2(math.e)

LANES = 128
TILE = 256
SUBT = 128
WIN_X = SUBT + 8
WIN_Y = SUBT + 16
VMEM_LIMIT = 56 * 1024 * 1024


def _cparams(*sem):
    return pltpu.CompilerParams(dimension_semantics=sem, vmem_limit_bytes=VMEM_LIMIT)


def _split3(a):
    a1 = a.astype(BF16)
    r1 = a - a1.astype(F32)
    a2 = r1.astype(BF16)
    a3 = (r1 - a2.astype(F32)).astype(BF16)
    return a1, a2, a3


def _dot(a, b):
    return jnp.dot(a, b, preferred_element_type=F32)


def _dot_t(a, b):
    return lax.dot_general(a, b, (((1,), (1,)), ((), ())), preferred_element_type=F32)


def _dot_exact_rhs(a, b_bf16):
    a1, a2, a3 = _split3(a)
    return _dot(a1, b_bf16) + _dot(a2, b_bf16) + _dot(a3, b_bf16)


def _dot_hi(a, b):
    a1 = a.astype(BF16)
    a2 = (a - a1.astype(F32)).astype(BF16)
    b1 = b.astype(BF16)
    b2 = (b - b1.astype(F32)).astype(BF16)
    return _dot(a1, b1) + _dot(a1, b2) + _dot(a2, b1)


def _sigmoid(x):
    return 1.0 / (1.0 + jnp.exp(-x))


def _silu(x):
    return x * _sigmoid(x)


def _modulate(x, g, shift, scale):
    ms = jnp.mean(x * x, axis=-1, keepdims=True)
    return x * lax.rsqrt(ms + EPS) * g * (1.0 + scale) + shift


def _adaln_kernel(s_ref, w_ref, b_ref, o_ref):
    s = _silu(s_ref[...])
    o_ref[...] = _dot_hi(s, w_ref[...]) + b_ref[...]


def _adaln(rows, w, b):
    d, n = w.shape
    tn = n // 4
    return pl.pallas_call(
        _adaln_kernel,
        grid=(n // tn,),
        in_specs=[pl.BlockSpec((8, d), lambda j: (0, 0)),
                  pl.BlockSpec((d, tn), lambda j: (0, j)),
                  pl.BlockSpec((1, tn), lambda j: (0, j))],
        out_specs=pl.BlockSpec((8, tn), lambda j: (0, j)),
        out_shape=jax.ShapeDtypeStruct((8, n), F32),
        compiler_params=_cparams("arbitrary"),
        name="adaln",
    )(rows, w, b.reshape(1, n))


def _modproj_kernel(h_ref, mod_ref, g_ref, w_ref, o_ref):
    mod = mod_ref[0, 0]
    a = _modulate(h_ref[0], g_ref[...], mod[0:1], mod[1:2])
    o_ref[0] = _dot(a.astype(BF16), w_ref[...]).astype(o_ref.dtype)


def _modproj(h, mod, g, w_bf16, nct, out_dtype):
    bsz, t, d = h.shape
    n = w_bf16.shape[1]
    return pl.pallas_call(
        _modproj_kernel,
        grid=(bsz, t // TILE),
        in_specs=[pl.BlockSpec((1, TILE, d), lambda b, i: (b, i, 0)),
                  pl.BlockSpec((1, 1, 6, d), lambda b, i: (b, jnp.where(i < nct, 0, 1), 0, 0)),
                  pl.BlockSpec((1, d), lambda b, i: (0, 0)),
                  pl.BlockSpec((d, n), lambda b, i: (0, 0))],
        out_specs=pl.BlockSpec((1, TILE, n), lambda b, i: (b, i, 0)),
        out_shape=jax.ShapeDtypeStruct((bsz, t, n), out_dtype),
        compiler_params=_cparams("arbitrary", "arbitrary"),
        name="modproj",
    )(h, mod, g.reshape(1, d), w_bf16)


def _router_kernel(h_ref, mod_ref, g_ref, wr_ref, a_ref, aff_ref, afft_ref, *, n_exp):
    mod = mod_ref[0, 0]
    a = _modulate(h_ref[0], g_ref[...], mod[3:4], mod[4:5])
    a_ref[0] = a.astype(BF16)
    logits = _dot_hi(a, wr_ref[...])
    lane = lax.broadcasted_iota(I32, logits.shape, 1)
    logits = jnp.where(lane < n_exp, logits, -jnp.inf)
    p = jnp.exp(logits - jnp.max(logits, axis=-1, keepdims=True))
    aff = p / jnp.sum(p, axis=-1, keepdims=True)
    aff_ref[0] = aff
    afft_ref[0] = aff.T


def _router(h, mod, g, w_router, nct):
    bsz, t, d = h.shape
    n_exp = w_router.shape[1]
    wr = jnp.zeros((d, LANES), F32).at[:, :n_exp].set(w_router)
    return pl.pallas_call(
        functools.partial(_router_kernel, n_exp=n_exp),
        grid=(bsz, t // TILE),
        in_specs=[pl.BlockSpec((1, TILE, d), lambda b, i: (b, i, 0)),
                  pl.BlockSpec((1, 1, 6, d), lambda b, i: (b, jnp.where(i < nct, 0, 1), 0, 0)),
                  pl.BlockSpec((1, d), lambda b, i: (0, 0)),
                  pl.BlockSpec((d, LANES), lambda b, i: (0, 0))],
        out_specs=[pl.BlockSpec((1, TILE, d), lambda b, i: (b, i, 0)),
                   pl.BlockSpec((1, TILE, LANES), lambda b, i: (b, i, 0)),
                   pl.BlockSpec((1, LANES, TILE), lambda b, i: (b, 0, i))],
        out_shape=[jax.ShapeDtypeStruct((bsz, t, d), BF16),
                   jax.ShapeDtypeStruct((bsz, t, LANES), F32),
                   jax.ShapeDtypeStruct((bsz, LANES, t), F32)],
        compiler_params=_cparams("arbitrary", "arbitrary"),
        name="router",
    )(h, mod, g.reshape(1, d), wr)


def _select_kernel(afft_ref, pos_ref, post_ref, cb_ref, cnt_ref, sel_ref, bucket_ref, *, n_exp, n_ctx, k_ctx, k_lat, t):
    tok = lax.broadcasted_iota(I32, (n_exp, t), 1)
    is_ctx = tok < n_ctx
    min_normal = 0x00800000

    def choose(in_set, k):
        def vals():
            return jnp.where(in_set, afft_ref[0][:n_exp], -1.0)

        def bit_step(it, thr):
            cand = thr | jnp.left_shift(jnp.int32(1), 30 - it)
            cnt = jnp.sum(jnp.where(vals() >= pltpu.bitcast(cand, F32), 1, 0), axis=1, keepdims=True)
            return jnp.where(cnt >= k, cand, thr)
        thr = lax.fori_loop(0, 31, bit_step, jnp.zeros((n_exp, 1), I32))
        lo = pltpu.bitcast(thr, F32)
        hi = pltpu.bitcast(jnp.maximum(thr + 1, min_normal), F32)
        v = vals()
        above = v >= hi
        sel_ref[...] = jnp.where(above, 1.0, sel_ref[...])
        bucket_ref[...] = jnp.where((v >= lo) & jnp.logical_not(above), 1.0, 0.0)
        need0 = k - jnp.sum(jnp.where(above, 1, 0), axis=1, keepdims=True)

        def peel(need):
            in_bucket = bucket_ref[...] > 0.0
            v = vals()
            top = jnp.max(jnp.where(in_bucket, v, -1.0), axis=1, keepdims=True)
            first = jnp.min(jnp.where(in_bucket & (v == top), tok, t), axis=1, keepdims=True)
            pick = (tok == first) & (need > 0)
            sel_ref[...] = jnp.where(pick, 1.0, sel_ref[...])
            bucket_ref[...] = jnp.where(pick, 0.0, bucket_ref[...])
            return need - 1
        lax.while_loop(lambda need: jnp.max(need) > 0, peel, need0)

    sel_ref[...] = jnp.zeros_like(sel_ref)
    choose(jnp.logical_not(is_ctx), k_lat)
    if k_ctx > 0:
        choose(is_ctx, k_ctx)

    rr = lax.broadcasted_iota(I32, (LANES, LANES), 0)
    cc = lax.broadcasted_iota(I32, (LANES, LANES), 1)
    upper = jnp.where(rr <= cc, 1.0, 0.0).astype(BF16)
    carry = jnp.zeros((n_exp, 1), F32)
    for blk in range(t // LANES):
        c = _dot(sel_ref[:, blk * LANES:(blk + 1) * LANES].astype(BF16), upper) + carry
        cnt_ref[:, blk * LANES:(blk + 1) * LANES] = c
        carry = c[:, LANES - 1:LANES]
    posinc = cnt_ref[...].astype(I32)
    pos = jnp.where(sel_ref[...] > 0.0, posinc - 1, -1)
    pos_ref[0] = pos

    lane = lax.broadcasted_iota(I32, (n_exp, LANES), 1)
    cb = jnp.zeros((n_exp, LANES), I32)
    for i in range(1, t // SUBT + 1):
        cb = jnp.where(lane == i, posinc[:, i * SUBT - 1:i * SUBT], cb)
    cb_ref[0] = cb

    padded = jnp.concatenate([pos.astype(F32), jnp.full((LANES - n_exp, t), -1.0, F32)], axis=0)
    for i in range(t // TILE):
        post_ref[0, i * TILE:(i + 1) * TILE, :] = padded[:, i * TILE:(i + 1) * TILE].T.astype(I32)


def _select(afft, n_exp, n_ctx, k_ctx, k_lat):
    bsz, _, t = afft.shape
    assert n_ctx % LANES == 0 and t % TILE == 0 and t // SUBT < LANES
    return pl.pallas_call(
        functools.partial(_select_kernel, n_exp=n_exp, n_ctx=n_ctx, k_ctx=k_ctx, k_lat=k_lat, t=t),
        grid=(bsz,),
        in_specs=[pl.BlockSpec((1, LANES, t), lambda b: (b, 0, 0))],
        out_specs=[pl.BlockSpec((1, n_exp, t), lambda b: (b, 0, 0)),
                   pl.BlockSpec((1, t, LANES), lambda b: (b, 0, 0)),
                   pl.BlockSpec((1, n_exp, LANES), lambda b: (b, 0, 0))],
        out_shape=[jax.ShapeDtypeStruct((bsz, n_exp, t), I32),
                   jax.ShapeDtypeStruct((bsz, t, LANES), I32),
                   jax.ShapeDtypeStruct((bsz, n_exp, LANES), I32)],
        scratch_shapes=[pltpu.VMEM((n_exp, t), F32)] * 3,
        compiler_params=_cparams("arbitrary"),
        name="select",
    )(afft)


def _ffn_rows(cap_tot):
    for rc in range(min(cap_tot, 512), 7, -8):
        if cap_tot % rc == 0:
            return rc
    raise ValueError(cap_tot)


def _expert_kernel(cb_ref, pos_ref, a_ref, wg_ref, wu_ref, wd_ref, y_ref, x_ref, *, n_exp, nsub, sub_per, cap_tot):
    e, b, ch = pl.program_id(0), pl.program_id(1), pl.program_id(2)

    @pl.when(ch == 0)
    def _():
        x_ref[...] = jnp.zeros_like(x_ref)

    base = (b * n_exp + e) * (nsub + 1) + ch * sub_per
    row = lax.broadcasted_iota(I32, (WIN_X, SUBT), 0)

    def gather(j, carry):
        start = cb_ref[base + j]
        start_al = pl.multiple_of((start // 8) * 8, 8)
        rel = pos_ref[0, 0, j] - start_al
        onehot = jnp.where(row == rel, 1.0, 0.0).astype(BF16)
        tok0 = pl.multiple_of(j * SUBT, SUBT)
        x_ref[pl.ds(start_al, WIN_X), :] += _dot(onehot, a_ref[0, pl.ds(tok0, SUBT), :])
        return carry
    lax.fori_loop(0, sub_per, gather, 0, unroll=4)

    @pl.when(ch == pl.num_programs(2) - 1)
    def _():
        rc = _ffn_rows(cap_tot)
        wg = wg_ref[0].astype(BF16)
        wu = wu_ref[0].astype(BF16)
        wd = wd_ref[0].astype(BF16)
        for r0 in range(0, cap_tot, rc):
            xc = x_ref[r0:r0 + rc, :].astype(BF16)
            hid = _silu(_dot(xc, wg)) * _dot(xc, wu)
            y_ref[0, 0, r0:r0 + rc, :] = _dot(hid.astype(BF16), wd).astype(y_ref.dtype)
        y_ref[0, 0, cap_tot:, :] = jnp.zeros((y_ref.shape[2] - cap_tot, y_ref.shape[-1]), y_ref.dtype)


def _experts(cb_flat, pos, a, wg, wu, wd, cap_tot):
    bsz, n_exp, t = pos.shape
    d = a.shape[-1]
    ff = wg.shape[-1]
    nsub = t // SUBT
    ntiles = t // TILE
    chunk = TILE * max(m for m in range(1, 13) if ntiles % m == 0)
    sub_per = chunk // SUBT
    assert cap_tot % 16 == 0
    rows_x = cap_tot + WIN_X
    rows_y = cap_tot + WIN_Y
    pos5 = pos.reshape(bsz, n_exp, nsub, 1, SUBT)
    grid_spec = pltpu.PrefetchScalarGridSpec(
        num_scalar_prefetch=1,
        grid=(n_exp, bsz, t // chunk),
        in_specs=[pl.BlockSpec((1, 1, sub_per, 1, SUBT), lambda e, b, c, cb: (b, e, c, 0, 0)),
                  pl.BlockSpec((1, chunk, d), lambda e, b, c, cb: (b, c, 0)),
                  pl.BlockSpec((1, d, ff), lambda e, b, c, cb: (e, 0, 0)),
                  pl.BlockSpec((1, d, ff), lambda e, b, c, cb: (e, 0, 0)),
                  pl.BlockSpec((1, ff, d), lambda e, b, c, cb: (e, 0, 0))],
        out_specs=pl.BlockSpec((1, 1, rows_y, d), lambda e, b, c, cb: (b, e, 0, 0)),
        scratch_shapes=[pltpu.VMEM((rows_x, d), F32)],
    )
    return pl.pallas_call(
        functools.partial(_expert_kernel, n_exp=n_exp, nsub=nsub, sub_per=sub_per, cap_tot=cap_tot),
        grid_spec=grid_spec,
        out_shape=jax.ShapeDtypeStruct((bsz, n_exp, rows_y, d), BF16),
        compiler_params=_cparams("arbitrary", "arbitrary", "arbitrary"),
        name="experts",
    )(cb_flat, pos5, a, wg, wu, wd)


def _combine(cb_flat, h, mod, aff, post, y, g_final, nct, final_norm):
    bsz, t, d = h.shape
    n_exp = y.shape[1]
    ntiles = t // TILE
    first = nct if final_norm else 0
    n_win = n_exp * (TILE // SUBT)
    grid_spec = pltpu.PrefetchScalarGridSpec(
        num_scalar_prefetch=1,
        grid=(bsz, ntiles - first),
        in_specs=[pl.BlockSpec((1, TILE, d), lambda b, i, cb: (b, i + first, 0)),
                  pl.BlockSpec((1, 1, 6, d), lambda b, i, cb: (b, jnp.where(i + first < nct, 0, 1), 0, 0)),
                  pl.BlockSpec((1, TILE, LANES), lambda b, i, cb: (b, i + first, 0)),
                  pl.BlockSpec((1, TILE, LANES), lambda b, i, cb: (b, i + first, 0)),
                  pl.BlockSpec(memory_space=pl.ANY),
                  pl.BlockSpec((1, d), lambda b, i, cb: (0, 0))],
        out_specs=pl.BlockSpec((1, TILE, d), lambda b, i, cb: (b, i, 0)),
        scratch_shapes=[pltpu.VMEM((n_win, WIN_Y, d), y.dtype), pltpu.SemaphoreType.DMA((n_win,))],
    )
    return pl.pallas_call(
        functools.partial(_combine_kernel, n_exp=n_exp, nsub=t // SUBT, first=first, final_norm=final_norm),
        grid_spec=grid_spec,
        out_shape=jax.ShapeDtypeStruct((bsz, t - first * TILE, d), F32),
        compiler_params=_cparams("arbitrary", "arbitrary"),
        name="combine",
    )(cb_flat, h, mod, aff, post, y, g_final.reshape(1, d))


def _combine_kernel(cb_ref, h_ref, mod_ref, aff_ref, post_ref, y_hbm, g_ref, o_ref, ybuf, sem,
                    *, n_exp, nsub, first, final_norm):
    b, i = pl.program_id(0), pl.program_id(1) + first
    per_tile = TILE // SUBT

    def window(sub, e):
        start = cb_ref[(b * n_exp + e) * (nsub + 1) + i * per_tile + sub]
        start_al = pl.multiple_of((start // 16) * 16, 16)
        slot = sub * n_exp + e
        copy = pltpu.make_async_copy(y_hbm.at[b, e, pl.ds(start_al, WIN_Y), :], ybuf.at[slot], sem.at[slot])
        return copy, start_al

    for sub in range(per_tile):
        for e in range(n_exp):
            window(sub, e)[0].start()
    lane = lax.broadcasted_iota(I32, (SUBT, WIN_Y), 1)
    mod = mod_ref[0, 0]
    for sub in range(per_tile):
        rows = slice(sub * SUBT, (sub + 1) * SUBT)
        post = post_ref[0, rows, :]
        aff = aff_ref[0, rows, :]
        acc = jnp.zeros((SUBT, o_ref.shape[-1]), F32)
        starts = []
        for e in range(n_exp):
            copy, start_al = window(sub, e)
            copy.wait()
            starts.append(start_al)
        for e, start_al in enumerate(starts):
            rel = post[:, e:e + 1] - start_al
            onehot = jnp.where(lane == rel, 1.0, 0.0).astype(BF16)
            acc = acc + aff[:, e:e + 1] * _dot(onehot, ybuf[sub * n_exp + e])
        out = h_ref[0, rows, :] + mod[5:6] * acc
        if final_norm:
            ms = jnp.mean(out * out, axis=-1, keepdims=True)
            out = out * lax.rsqrt(ms + EPS) * g_ref[...]
        o_ref[0, rows, :] = out


def _moe(h, mod, ln2_g, w_router, wg, wu, wd, g_final, *, n_ctx, route_ctx, final_norm):
    bsz, t, d = h.shape
    n_exp = w_router.shape[1]
    nct = n_ctx // TILE
    n_lat = t - n_ctx
    k_lat = max(1, EC_CAPACITY * n_lat // n_exp)
    k_ctx = max(1, EC_CAPACITY * n_ctx // n_exp) if route_ctx else 0
    a, aff, afft = _router(h, mod, ln2_g, w_router, nct)
    pos, post, cb = _select(afft, n_exp, n_ctx, k_ctx, k_lat)
    cb_flat = cb[:, :, :t // SUBT + 1].reshape(-1)
    y = _experts(cb_flat, pos, a, wg, wu, wd, k_ctx + k_lat)
    return _combine(cb_flat, h, mod, aff, post, y, g_final, nct, final_norm)


HEAD_SHIFT = int(math.log2(HEAD_DIM))


def _same_head(rows, cols):
    r = lax.shift_right_logical(lax.broadcasted_iota(I32, (rows, cols), 0), HEAD_SHIFT)
    c = lax.shift_right_logical(lax.broadcasted_iota(I32, (rows, cols), 1), HEAD_SHIFT)
    return jnp.where(r == c, 1.0, 0.0)


def _head_sum(x2):
    n = x2.shape[-1]
    return _dot_exact_rhs(x2, _same_head(n, n).astype(BF16))


def _outproj_even_kernel(h_ref, mod_ref, of_ref, ob_ref, z_ref, at_ref, g_ref, w_ref, o_ref, *, a_dim):
    mod = mod_ref[0, 0]
    o = of_ref[0] + ob_ref[0]
    ms = _head_sum(o * o) * (1.0 / HEAD_DIM)
    ya = o * lax.rsqrt(ms + EPS) * g_ref[...] * _silu(z_ref[0])
    y = _dot(ya.astype(BF16), w_ref[:a_dim, :]) + _dot(at_ref[0], w_ref[a_dim:, :])
    o_ref[0] = h_ref[0] + mod[2:3] * y


def _outproj_even(h, mod, o_delta, p, o_attn, out_norm_g, w_bf16, nct, *, a_dim):
    bsz, t, d = h.shape
    o_f, o_b = o_delta
    g = jnp.tile(out_norm_g, a_dim // HEAD_DIM).reshape(1, a_dim)
    tok = lambda b, i: (b, i, 0)
    return pl.pallas_call(
        functools.partial(_outproj_even_kernel, a_dim=a_dim),
        grid=(bsz, t // TILE),
        in_specs=[pl.BlockSpec((1, TILE, d), tok),
                  pl.BlockSpec((1, 1, 6, d), lambda b, i: (b, jnp.where(i < nct, 0, 1), 0, 0)),
                  pl.BlockSpec((1, TILE, a_dim), tok),
                  pl.BlockSpec((1, TILE, a_dim), tok),
                  pl.BlockSpec((1, TILE, a_dim), lambda b, i: (b, i, 3)),
                  pl.BlockSpec((1, TILE, a_dim), tok),
                  pl.BlockSpec((1, a_dim), lambda b, i: (0, 0)),
                  pl.BlockSpec((d, d), lambda b, i: (0, 0))],
        out_specs=pl.BlockSpec((1, TILE, d), tok),
        out_shape=jax.ShapeDtypeStruct((bsz, t, d), F32),
        compiler_params=_cparams("arbitrary", "arbitrary"),
        name="outproj_even",
    )(h, mod, o_f, o_b, p, o_attn, g, w_bf16)


def _outproj_odd_kernel(h_ref, mod_ref, a_ref, w_ref, o_ref):
    mod = mod_ref[0, 0]
    o_ref[0] = h_ref[0] + mod[2:3] * _dot(a_ref[0], w_ref[...])


def _outproj_odd(h, mod, o_na, w_bf16, nct):
    bsz, t, d = h.shape
    n_lat = o_na.shape[1]
    return pl.pallas_call(
        _outproj_odd_kernel,
        grid=(bsz, n_lat // TILE),
        in_specs=[pl.BlockSpec((1, TILE, d), lambda b, i: (b, i + nct, 0)),
                  pl.BlockSpec((1, 1, 6, d), lambda b, i: (b, 1, 0, 0)),
                  pl.BlockSpec((1, TILE, d), lambda b, i: (b, i, 0)),
                  pl.BlockSpec((d, d), lambda b, i: (0, 0))],
        out_specs=pl.BlockSpec((1, TILE, d), lambda b, i: (b, i, 0)),
        out_shape=jax.ShapeDtypeStruct((bsz, n_lat, d), F32),
        compiler_params=_cparams("arbitrary", "arbitrary"),
        name="outproj_odd",
    )(h, mod, o_na, w_bf16)


def _rope_tables(n_ctx, n_lat):
    tt = jnp.arange(n_lat)
    n_freq = HEAD_DIM // 4
    inv = ROPE_THETA ** (-jnp.arange(n_freq, dtype=F32) / n_freq)
    ang = jnp.concatenate([(tt // GRID_W).astype(F32)[:, None] * inv, (tt % GRID_W).astype(F32)[:, None] * inv], -1)
    cos = jnp.concatenate([jnp.ones((n_ctx, 2 * n_freq), F32), jnp.cos(ang)], axis=0)
    sin = jnp.concatenate([jnp.zeros((n_ctx, 2 * n_freq), F32), jnp.sin(ang)], axis=0)
    reps = LANES // HEAD_DIM
    return jnp.tile(cos, (1, 2 * reps)), jnp.tile(jnp.concatenate([-sin, sin], axis=1), (1, reps))


def _rope(x, cos, sin):
    n = x.shape[-1]
    half = HEAD_DIM // 2
    lane = lax.broadcasted_iota(I32, x.shape, 1)
    first = (lane & (HEAD_DIM - 1)) < half
    partner = jnp.where(first, pltpu.roll(x, n - half, 1), pltpu.roll(x, half, 1))
    reps = n // LANES
    cos_n = jnp.concatenate([cos] * reps, axis=1) if reps > 1 else cos
    sin_n = jnp.concatenate([sin] * reps, axis=1) if reps > 1 else sin
    return x * cos_n + partner * sin_n


def _gqa_prep_kernel(q_ref, k_ref, v_ref, cos_ref, sin_ref, gq_ref, gk_ref, qo_ref, ko_ref, vo_ref):
    cos, sin = cos_ref[...], sin_ref[...]
    q = q_ref[0]
    q = q * lax.rsqrt(_head_sum(q * q) * (1.0 / HEAD_DIM) + EPS) * gq_ref[...] * (HEAD_DIM ** -0.5 * LOG2_E)
    qo_ref[0] = _rope(q, cos, sin).astype(BF16)
    k = k_ref[0]
    k = k * lax.rsqrt(_head_sum(k * k) * (1.0 / HEAD_DIM) + EPS) * gk_ref[...]
    k = _rope(k, cos, sin).astype(BF16)
    v = v_ref[0].astype(BF16)
    for hh in range(k.shape[-1] // HEAD_DIM):
        ko_ref[0, hh] = k[:, hh * HEAD_DIM:(hh + 1) * HEAD_DIM]
        vo_ref[0, hh] = v[:, hh * HEAD_DIM:(hh + 1) * HEAD_DIM]


def _gqa_prep(p, cos, sin, q_norm_g, k_norm_g, *, a_dim, kv_dim):
    bsz, t, _ = p.shape
    q_dim = a_dim
    kvh = kv_dim // HEAD_DIM
    assert kv_dim == LANES
    c_q = (4 * a_dim) // q_dim
    c_k = (4 * a_dim + q_dim) // kv_dim
    gq = jnp.tile(q_norm_g, q_dim // HEAD_DIM).reshape(1, q_dim)
    gk = jnp.tile(k_norm_g, kvh).reshape(1, kv_dim)
    return pl.pallas_call(
        _gqa_prep_kernel,
        grid=(bsz, t // TILE),
        in_specs=[pl.BlockSpec((1, TILE, q_dim), lambda b, i: (b, i, c_q)),
                  pl.BlockSpec((1, TILE, kv_dim), lambda b, i: (b, i, c_k)),
                  pl.BlockSpec((1, TILE, kv_dim), lambda b, i: (b, i, c_k + 1)),
                  pl.BlockSpec((TILE, LANES), lambda b, i: (i, 0)),
                  pl.BlockSpec((TILE, LANES), lambda b, i: (i, 0)),
                  pl.BlockSpec((1, q_dim), lambda b, i: (0, 0)),
                  pl.BlockSpec((1, kv_dim), lambda b, i: (0, 0))],
        out_specs=[pl.BlockSpec((1, TILE, q_dim), lambda b, i: (b, i, 0)),
                   pl.BlockSpec((1, kvh, TILE, HEAD_DIM), lambda b, i: (b, 0, i, 0)),
                   pl.BlockSpec((1, kvh, TILE, HEAD_DIM), lambda b, i: (b, 0, i, 0))],
        out_shape=[jax.ShapeDtypeStruct((bsz, t, q_dim), BF16),
                   jax.ShapeDtypeStruct((bsz, kvh, t, HEAD_DIM), BF16),
                   jax.ShapeDtypeStruct((bsz, kvh, t, HEAD_DIM), BF16)],
        compiler_params=_cparams("arbitrary", "arbitrary"),
        name="gqa_prep",
    )(p, p, p, cos, sin, gq, gk)


SUB_K = 256
Q_SLAB = 256


def _gqa_kernel(q_ref, k_ref, v_ref, o_ref, qs_ref, m_ref, l_ref, acc_ref, *, n_ctx, nct, tq, tk, group):
    i, kv = pl.program_id(2), pl.program_id(3)

    @pl.when(kv == 0)
    def _():
        m_ref[...] = jnp.full_like(m_ref, -jnp.inf)
        l_ref[...] = jnp.zeros_like(l_ref)
        acc_ref[...] = jnp.zeros_like(acc_ref)
        q = q_ref[0]
        for g in range(group):
            qs_ref[g * tq:(g + 1) * tq, :] = q[:, g * HEAD_DIM:(g + 1) * HEAD_DIM]

    def attend(keys, vals, n_valid):
        for r0 in range(0, group * tq, Q_SLAB):
            rows = slice(r0, r0 + Q_SLAB)
            s = _dot_t(qs_ref[rows, :], keys)
            if n_valid is not None:
                s = jnp.where(lax.broadcasted_iota(I32, s.shape, 1) < n_valid, s, -jnp.inf)
            m_old = m_ref[rows, :]
            m_new = jnp.maximum(m_old, jnp.max(s, axis=-1, keepdims=True))
            alpha = jnp.exp2(m_old - m_new)
            p = jnp.exp2(s - m_new)
            l_ref[rows, :] = alpha * l_ref[rows, :] + jnp.sum(p, axis=-1, keepdims=True)
            acc_ref[rows, :] = alpha * acc_ref[rows, :] + _dot(p.astype(BF16), vals)
            m_ref[rows, :] = m_new

    @pl.when(i >= nct)
    def _():
        attend(k_ref[0, 0], v_ref[0, 0], None)

    @pl.when((i < nct) & (kv == 0))
    def _():
        n_keys = -(-n_ctx // SUB_K) * SUB_K
        attend(k_ref[0, 0, :n_keys, :], v_ref[0, 0, :n_keys, :], n_ctx if n_keys != n_ctx else None)

    @pl.when(kv == pl.num_programs(3) - 1)
    def _():
        out = acc_ref[...] / l_ref[...]
        for g in range(group):
            o_ref[0, :, g * HEAD_DIM:(g + 1) * HEAD_DIM] = out[g * tq:(g + 1) * tq, :].astype(o_ref.dtype)


def _gqa_attention(q, k, v, *, n_ctx):
    bsz, t, q_dim = q.shape
    kvh = k.shape[1]
    group = q_dim // HEAD_DIM // kvh
    tq = TILE
    nt = t // TILE
    tk = TILE * max(m for m in range(1, 13) if nt % m == 0)
    assert n_ctx <= tk and tk % SUB_K == 0
    return pl.pallas_call(
        functools.partial(_gqa_kernel, n_ctx=n_ctx, nct=n_ctx // tq, tq=tq, tk=tk, group=group),
        grid=(bsz, kvh, t // tq, t // tk),
        in_specs=[pl.BlockSpec((1, tq, group * HEAD_DIM), lambda b, h, i, j: (b, i, h)),
                  pl.BlockSpec((1, 1, tk, HEAD_DIM), lambda b, h, i, j: (b, h, j, 0)),
                  pl.BlockSpec((1, 1, tk, HEAD_DIM), lambda b, h, i, j: (b, h, j, 0))],
        out_specs=pl.BlockSpec((1, tq, group * HEAD_DIM), lambda b, h, i, j: (b, i, h)),
        out_shape=jax.ShapeDtypeStruct((bsz, t, q_dim), BF16),
        scratch_shapes=[pltpu.VMEM((group * tq, HEAD_DIM), BF16),
                        pltpu.VMEM((group * tq, 1), F32),
                        pltpu.VMEM((group * tq, 1), F32),
                        pltpu.VMEM((group * tq, HEAD_DIM), F32)],
        compiler_params=_cparams("arbitrary", "arbitrary", "arbitrary", "arbitrary"),
        name="gqa_attention",
    )(q, k, v)


def _natten_bias(rpb):
    n_heads, n_off, n_rel = rpb.shape
    col = np.arange(GRID_W)
    col_start = np.clip(col - NA_COLS // 2, 0, GRID_W - NA_COLS)
    valid = (col[None, :] >= col_start[:, None]) & (col[None, :] < col_start[:, None] + NA_COLS)
    rel = col[None, :] - col[:, None] + NA_COLS - 1
    onehot = ((rel[None] == np.arange(n_rel)[:, None, None]) & valid[None]).astype(np.float32)
    tab = jnp.einsum('hrk,kcd->hrcd', rpb, jnp.asarray(onehot), precision=lax.Precision.HIGHEST)
    tab = jnp.where(valid[None, None], tab, -jnp.inf)
    pairs = jnp.concatenate([tab[:, :-1], tab[:, 1:]], axis=-1)
    return pairs.reshape(n_heads // 2, 2, n_off - 1, GRID_W, 2 * GRID_W).astype(F32)


def _natten_kernel(q_ref, k_ref, v_ref, bias_ref, o_ref, *, n_ctx, n_rows, rblk):
    rb = pl.program_id(2)
    first_head = lax.broadcasted_iota(I32, (1, LANES), 1) < HEAD_DIM
    kc = k_ref[0, 0:n_ctx, :]
    vc = v_ref[0, 0:n_ctx, :]
    band = NA_ROWS * GRID_W
    qs, ks, vs, biases = [], [], [], []
    for rr in range(rblk):
        rg = rb * rblk + rr
        rs = jnp.clip(rg - NA_ROWS // 2, 0, n_rows - NA_ROWS)
        variant = rs - rg + (NA_ROWS - 1)
        start = pl.multiple_of(n_ctx + rs * GRID_W, GRID_W)
        keys = jnp.concatenate([kc, k_ref[0, pl.ds(start, band), :]], axis=0)
        vals = jnp.concatenate([vc, v_ref[0, pl.ds(start, band), :]], axis=0)
        q = q_ref[0, rr * GRID_W:(rr + 1) * GRID_W, :] * (HEAD_DIM ** -0.5)
        for hh in range(2):
            qs.append(jnp.where(first_head if hh == 0 else jnp.logical_not(first_head), q, jnp.zeros_like(q)))
            ks.append(keys)
            vs.append(vals)
            biases.append(jnp.concatenate([bias_ref[0, hh, variant + j] for j in range(0, NA_ROWS, 2)], axis=1))
    batch = (((2,), (2,)), ((0,), (0,)))
    s = lax.dot_general(jnp.stack(qs), jnp.stack(ks), batch, preferred_element_type=F32)
    sc = s[:, :, :n_ctx]
    sb = s[:, :, n_ctx:] + jnp.stack(biases)
    m = jnp.maximum(jnp.max(sc, axis=-1, keepdims=True), jnp.max(sb, axis=-1, keepdims=True))
    pc = jnp.exp(sc - m)
    pb = jnp.exp(sb - m)
    l = jnp.sum(pc, axis=-1, keepdims=True) + jnp.sum(pb, axis=-1, keepdims=True)
    p = jnp.concatenate([pc, pb], axis=-1).astype(BF16)
    o = lax.dot_general(p, jnp.stack(vs), (((2,), (1,)), ((0,), (0,))), preferred_element_type=F32) / l
    for rr in range(rblk):
        o_ref[0, rr * GRID_W:(rr + 1) * GRID_W, :] = jnp.where(first_head, o[2 * rr], o[2 * rr + 1]).astype(o_ref.dtype)


def _natten(p, bias, *, n_ctx, d):
    bsz, t, _ = p.shape
    n_lat = t - n_ctx
    n_rows = n_lat // GRID_W
    rblk = TILE // GRID_W
    npairs = d // LANES
    assert n_rows >= NA_ROWS and n_ctx % TILE == 0
    return pl.pallas_call(
        functools.partial(_natten_kernel, n_ctx=n_ctx, n_rows=n_rows, rblk=rblk),
        grid=(bsz, npairs, n_lat // TILE),
        in_specs=[pl.BlockSpec((1, TILE, LANES), lambda b, hp, r: (b, n_ctx // TILE + r, hp)),
                  pl.BlockSpec((1, t, LANES), lambda b, hp, r: (b, 0, npairs + hp)),
                  pl.BlockSpec((1, t, LANES), lambda b, hp, r: (b, 0, 2 * npairs + hp)),
                  pl.BlockSpec((1, 2, 2 * NA_ROWS - 2, GRID_W, 2 * GRID_W), lambda b, hp, r: (hp, 0, 0, 0, 0))],
        out_specs=pl.BlockSpec((1, TILE, LANES), lambda b, hp, r: (b, r, hp)),
        out_shape=jax.ShapeDtypeStruct((bsz, n_lat, d), BF16),
        compiler_params=_cparams("arbitrary", "arbitrary", "arbitrary"),
        name="natten",
    )(p, p, p, bias)


def _dot_exact_lhs(a_bf16, b):
    b1, b2, b3 = _split3(b)
    return _dot(a_bf16, b1) + _dot(a_bf16, b2) + _dot(a_bf16, b3)


def _delta_prep_kernel(x_ref, prev_ref, next_ref, gate_ref, w_ref, alog_ref, dt_ref, xh_ref, gn_ref, gnt_ref, xs_ref,
                       *, nct, a_dim):
    i = pl.program_id(1)
    last = pl.num_programs(1) - 1
    x = x_ref[0]
    rows = x.shape[0]
    xs_ref[0:8, :] = prev_ref[0]
    xs_ref[8:8 + rows, :] = x
    xs_ref[8 + rows:16 + rows, :] = next_ref[0]
    r = lax.broadcasted_iota(I32, (rows, 1), 0)
    at_start = (i == 0) | (i == nct)
    at_end = (i == nct - 1) | (i == last)
    xm = jnp.where((r == 0) & at_start, 0.0, xs_ref[7:7 + rows, :])
    xp = jnp.where((r == rows - 1) & at_end, 0.0, xs_ref[9:9 + rows, :])
    w = w_ref[...]
    y = _silu(w[0:1] * xm + w[1:2] * x + w[2:3] * xp)
    q, k, v = y[:, :a_dim], y[:, a_dim:2 * a_dim], y[:, 2 * a_dim:]
    nh = a_dim // HEAD_DIM
    q = (q * lax.rsqrt(_head_sum(q * q) + EPS) * (HEAD_DIM ** -0.5)).astype(BF16)
    k = (k * lax.rsqrt(_head_sum(k * k) + EPS)).astype(BF16)
    v = v.astype(BF16)
    for kind, arr in enumerate((q, k, v)):
        for hh in range(nh):
            xh_ref[0, kind, hh] = arr[:, hh * HEAD_DIM:(hh + 1) * HEAD_DIM]

    g = gate_ref[0]
    z = g + dt_ref[...]
    softplus = jnp.maximum(z, 0.0) + jnp.log1p(jnp.exp(-jnp.abs(z)))
    log_a = -jnp.exp(alog_ref[...]) * softplus
    beta = _sigmoid(g)
    ri = lax.broadcasted_iota(I32, (rows, rows), 0)
    ci = lax.broadcasted_iota(I32, (rows, rows), 1)
    chunk_shift = int(math.log2(DELTA_CHUNK))
    same_chunk = lax.shift_right_logical(ri, chunk_shift) == lax.shift_right_logical(ci, chunk_shift)
    prefix = jnp.where(same_chunk & (ci <= ri), 1.0, 0.0).astype(BF16)
    suffix = jnp.where(same_chunk & (ci >= ri), 1.0, 0.0).astype(BF16)
    g_f = _dot_exact_lhs(prefix, log_a)
    g_b = _dot_exact_lhs(suffix, log_a)
    lane = lax.broadcasted_iota(I32, g.shape, 1)
    narrow = jnp.where(lane < nh, g_f, jnp.where(lane < 2 * nh, beta, jnp.where(lane < 3 * nh, g_b, beta)))
    gn_ref[0] = narrow
    gnt_ref[0] = narrow.T


def _delta_prep(p, conv_w, a_log, dt_bias, *, n_ctx, a_dim):
    bsz, t, _ = p.shape
    nh = a_dim // HEAD_DIM
    assert DELTA_CHUNK == HEAD_DIM and TILE % DELTA_CHUNK == 0 and 4 * nh <= LANES
    nct = n_ctx // TILE
    c_gate = (4 * a_dim + a_dim + 2 * LANES) // LANES
    alog = jnp.zeros((1, LANES), F32).at[0, :nh].set(a_log[0]).at[0, 2 * nh:3 * nh].set(a_log[1])
    dtb = jnp.zeros((1, LANES), F32).at[0, :nh].set(dt_bias[0]).at[0, 2 * nh:3 * nh].set(dt_bias[1])
    nblk8 = t // 8
    per = TILE // 8
    return pl.pallas_call(
        functools.partial(_delta_prep_kernel, nct=nct, a_dim=a_dim),
        grid=(bsz, t // TILE),
        in_specs=[pl.BlockSpec((1, TILE, 3 * a_dim), lambda b, i: (b, i, 0)),
                  pl.BlockSpec((1, 8, 3 * a_dim), lambda b, i: (b, jnp.maximum(i * per - 1, 0), 0)),
                  pl.BlockSpec((1, 8, 3 * a_dim), lambda b, i: (b, jnp.minimum((i + 1) * per, nblk8 - 1), 0)),
                  pl.BlockSpec((1, TILE, LANES), lambda b, i: (b, i, c_gate)),
                  pl.BlockSpec((3, 3 * a_dim), lambda b, i: (0, 0)),
                  pl.BlockSpec((1, LANES), lambda b, i: (0, 0)),
                  pl.BlockSpec((1, LANES), lambda b, i: (0, 0))],
        out_specs=[pl.BlockSpec((1, 3, nh, TILE, HEAD_DIM), lambda b, i: (b, 0, 0, i, 0)),
                   pl.BlockSpec((1, TILE, LANES), lambda b, i: (b, i, 0)),
                   pl.BlockSpec((1, LANES, TILE), lambda b, i: (b, 0, i))],
        out_shape=[jax.ShapeDtypeStruct((bsz, 3, nh, t, HEAD_DIM), BF16),
                   jax.ShapeDtypeStruct((bsz, t, LANES), F32),
                   jax.ShapeDtypeStruct((bsz, LANES, t), F32)],
        scratch_shapes=[pltpu.VMEM((TILE + 16, 3 * a_dim), F32)],
        compiler_params=_cparams("arbitrary", "arbitrary"),
        name="delta_prep",
    )(p, p, p, p, conv_w, alog, dtb)


def _bdot(a, b, ca, cb, hi=False):
    dims = (((ca,), (cb,)), ((0,), (0,)))
    dot = lambda x, y: lax.dot_general(x, y, dims, preferred_element_type=F32)
    if not hi:
        return dot(a.astype(BF16), b.astype(BF16))
    a1 = a.astype(BF16)
    a2 = (a - a1.astype(F32)).astype(BF16)
    b1 = b.astype(BF16)
    b2 = (b - b1.astype(F32)).astype(BF16)
    return dot(a1, b1) + dot(a1, b2) + dot(a2, b1)


def _delta_heads(q, k, v, g_col, beta_col, g_row, s, upper):
    nb, lc, _ = q.shape
    ri = lax.broadcasted_iota(I32, (nb, lc, lc), 1)
    ci = lax.broadcasted_iota(I32, (nb, lc, lc), 2)
    ahead = jnp.where(upper, ci - ri, ri - ci)
    tri_incl = ahead >= 0
    tri_strict = ahead > 0
    eye = jnp.where(ri == ci, 1.0, 0.0)
    k32, q32, v32 = k.astype(F32), q.astype(F32), v.astype(F32)
    eg = jnp.exp(g_col)
    g_last = jnp.where(upper, g_col[:, 0:1], g_col[:, lc - 1:lc])
    gam = jnp.exp(jnp.where(tri_incl, g_col - g_row, -jnp.inf))
    kq = _bdot(jnp.concatenate([k, q], axis=1), k, 2, 2)
    m = jnp.where(tri_strict, kq[:, :lc] * beta_col * gam, 0.0)
    a_qk = kq[:, lc:] * gam
    n = -m
    x = eye + n
    p = _bdot(n, n, 2, 1, hi=True)
    for _ in range(int(math.log2(lc)) - 2):
        x, p = x + _bdot(p, x, 2, 1, hi=True), _bdot(p, p, 2, 1, hi=True)
    x = x + _bdot(p, x, 2, 1, hi=True)
    u = _bdot(x, v32 * beta_col, 2, 1, hi=True)
    w = _bdot(x, k32 * (beta_col * eg), 2, 1, hi=True)
    ws_qs = _bdot(jnp.concatenate([w, q32 * eg], axis=1), s, 2, 1)
    v_new = u - ws_qs[:, :lc]
    o = ws_qs[:, lc:] + _bdot(a_qk, v_new, 2, 1)
    kv = _bdot(k32 * jnp.exp(g_last - g_col), v_new, 1, 1)
    return o, s * jnp.exp(g_last) + kv


def _delta_scan_kernel(xf_ref, xb_ref, gf_ref, gb_ref, gtf_ref, gtb_ref, of_ref, ob_ref, s_ref, *, nh):
    c = pl.program_id(1)

    @pl.when(c == 0)
    def _():
        s_ref[...] = jnp.zeros_like(s_ref)

    both = lambda f, b: jnp.concatenate([f, b], axis=0)
    q, k, v = (both(xf_ref[0, i], xb_ref[0, i]) for i in range(3))
    g_cols, b_cols, g_rows = [], [], []
    for d_i, (g_ref, gt_ref) in enumerate(((gf_ref, gtf_ref), (gb_ref, gtb_ref))):
        gn = g_ref[0]
        gt = gt_ref[0, 0]
        for hh in range(nh):
            lane_g = 2 * d_i * nh + hh
            g_cols.append(gn[:, lane_g:lane_g + 1])
            b_cols.append(gn[:, lane_g + nh:lane_g + nh + 1])
            g_rows.append(gt[lane_g:lane_g + 1, :])
    upper = lax.broadcasted_iota(I32, (2 * nh, 1, 1), 0) >= nh
    o, s_new = _delta_heads(q, k, v, jnp.stack(g_cols), jnp.stack(b_cols), jnp.stack(g_rows), s_ref[...], upper)
    s_ref[...] = s_new
    for hh in range(nh):
        of_ref[0, :, hh * HEAD_DIM:(hh + 1) * HEAD_DIM] = o[hh]
        ob_ref[0, :, hh * HEAD_DIM:(hh + 1) * HEAD_DIM] = o[nh + hh]


def _delta_scan(xh, gn, gnt, *, n_ctx):
    bsz, _, nh, t, _ = xh.shape
    lc = DELTA_CHUNK
    nc, ncc = t // lc, n_ctx // lc
    gnt4 = gnt.reshape(bsz, LANES, nc, lc).transpose(0, 2, 1, 3)
    rev = lambda c: jnp.where(c < ncc, ncc - 1 - c, nc - 1 - (c - ncc))
    return pl.pallas_call(
        functools.partial(_delta_scan_kernel, nh=nh),
        grid=(bsz, nc),
        in_specs=[pl.BlockSpec((1, 3, nh, lc, HEAD_DIM), lambda b, c: (b, 0, 0, c, 0)),
                  pl.BlockSpec((1, 3, nh, lc, HEAD_DIM), lambda b, c: (b, 0, 0, rev(c), 0)),
                  pl.BlockSpec((1, lc, LANES), lambda b, c: (b, c, 0)),
                  pl.BlockSpec((1, lc, LANES), lambda b, c: (b, rev(c), 0)),
                  pl.BlockSpec((1, 1, LANES, lc), lambda b, c: (b, c, 0, 0)),
                  pl.BlockSpec((1, 1, LANES, lc), lambda b, c: (b, rev(c), 0, 0))],
        out_specs=[pl.BlockSpec((1, lc, nh * HEAD_DIM), lambda b, c: (b, c, 0)),
                   pl.BlockSpec((1, lc, nh * HEAD_DIM), lambda b, c: (b, rev(c), 0))],
        out_shape=[jax.ShapeDtypeStruct((bsz, t, nh * HEAD_DIM), F32),
                   jax.ShapeDtypeStruct((bsz, t, nh * HEAD_DIM), F32)],
        scratch_shapes=[pltpu.VMEM((2 * nh, HEAD_DIM, HEAD_DIM), F32)],
        compiler_params=_cparams("arbitrary", "arbitrary"),
        name="delta_scan",
    )(xh, xh, gn, gn, gnt4, gnt4)


def kernel(x, c, ctx, c_ctx, ada_w, ada_b, ln1_g, ln2_g, even_w_in, even_conv_w, even_a_log, even_dt_bias, even_out_norm_g, even_q_norm_g, even_k_norm_g, even_w_out, odd_w_in, odd_rpb, odd_w_out, router_w, expert_w_gate, expert_w_up, expert_w_down, final_norm_g):
    bsz, n_lat, d = x.shape
    n_ctx = ctx.shape[1]
    t = n_ctx + n_lat
    nct = n_ctx // TILE
    h = jnp.concatenate([ctx, x], axis=1)
    rows = jnp.zeros((8, d), F32).at[0].set(c_ctx).at[1:1 + bsz].set(c)

    def mod_of(layer):
        m = _adaln(rows, ada_w[layer], ada_b[layer]).reshape(8, 6, d)
        return jnp.stack([jnp.broadcast_to(m[0], (bsz, 6, d)), m[1:1 + bsz]], axis=1)

    mod = mod_of(0)
    a_dim = d // 2
    a_heads = a_dim // HEAD_DIM
    w_in = even_w_in[0]
    o_qkv, o_z, o_g = 0, 3 * a_dim, 4 * a_dim
    o_bq = o_g + 4 * a_heads
    kv_dim = (w_in.shape[1] - o_bq - a_dim) // 2
    o_bk, o_bv = o_bq + a_dim, o_bq + a_dim + kv_dim
    gates_w = jnp.zeros((d, LANES), F32).at[:, :4 * a_heads].set(w_in[:, o_g:o_bq])
    w_perm = jnp.concatenate([w_in[:, :o_g], w_in[:, o_bq:], gates_w], axis=1).astype(BF16)
    p = _modproj(h, mod, ln1_g[0], w_perm, nct, F32)
    xh, gn, gnt = _delta_prep(p, even_conv_w[0], even_a_log[0], even_dt_bias[0], n_ctx=n_ctx, a_dim=a_dim)
    o_delta = _delta_scan(xh, gn, gnt, n_ctx=n_ctx)
    cos, sin = _rope_tables(n_ctx, n_lat)
    qb, kb, vb = _gqa_prep(p, cos, sin, even_q_norm_g[0], even_k_norm_g[0], a_dim=a_dim, kv_dim=kv_dim)
    o_attn = _gqa_attention(qb, kb, vb, n_ctx=n_ctx)
    h = _outproj_even(h, mod, o_delta, p, o_attn, even_out_norm_g[0], even_w_out[0].astype(BF16), nct, a_dim=a_dim)
    h = _moe(h, mod, ln2_g[0], router_w[0], expert_w_gate[0], expert_w_up[0], expert_w_down[0], final_norm_g,
             n_ctx=n_ctx, route_ctx=True, final_norm=False)

    mod = mod_of(1)
    p = _modproj(h, mod, ln1_g[1], odd_w_in[0].astype(BF16), nct, BF16)
    o_na = _natten(p, _natten_bias(odd_rpb[0]), n_ctx=n_ctx, d=d)
    h_lat = _outproj_odd(h, mod, o_na, odd_w_out[0].astype(BF16), nct)
    return _moe(h_lat, mod, ln2_g[1], router_w[1], expert_w_gate[1], expert_w_up[1], expert_w_down[1], final_norm_g,
                n_ctx=0, route_ctx=False, final_norm=True)
```

```python
import functools
import math

import numpy as np
import jax
import jax.numpy as jnp
from jax import lax
from jax.experimental import pallas as pl
from jax.experimental.pallas import tpu as pltpu

F32, BF16, I32 = jnp.float32, jnp.bfloat16, jnp.int32

HEAD_DIM = 64
GRID_W = 64
DELTA_CHUNK = 64
NA_ROWS = 8
NA_COLS = 16
ROPE_THETA = 10000.0
EC_CAPACITY = 2
EPS = 1e-6
LOG2_E = math.log2(math.e)

LANES = 128
TILE = 256
SUBT = 128
WIN_X = SUBT + 8
WIN_Y = SUBT + 16
VMEM_LIMIT = 56 * 1024 * 1024


def _cparams(*sem):
    return pltpu.CompilerParams(dimension_semantics=sem, vmem_limit_bytes=VMEM_LIMIT)


def _split3(a):
    a1 = a.astype(BF16)
    r1 = a - a1.astype(F32)
    a2 = r1.astype(BF16)
    a3 = (r1 - a2.astype(F32)).astype(BF16)
    return a1, a2, a3


def _dot(a, b):
    return jnp.dot(a, b, preferred_element_type=F32)


def _dot_t(a, b):
    return lax.dot_general(a, b, (((1,), (1,)), ((), ())), preferred_element_type=F32)


def _dot_exact_rhs(a, b_bf16):
    a1, a2, a3 = _split3(a)
    return _dot(a1, b_bf16) + _dot(a2, b_bf16) + _dot(a3, b_bf16)


def _dot_hi(a, b):
    a1 = a.astype(BF16)
    a2 = (a - a1.astype(F32)).astype(BF16)
    b1 = b.astype(BF16)
    b2 = (b - b1.astype(F32)).astype(BF16)
    return _dot(a1, b1) + _dot(a1, b2) + _dot(a2, b1)


def _sigmoid(x):
    return 1.0 / (1.0 + jnp.exp(-x))


def _silu(x):
    return x * _sigmoid(x)


def _modulate(x, g, shift, scale):
    ms = jnp.mean(x * x, axis=-1, keepdims=True)
    return x * lax.rsqrt(ms + EPS) * g * (1.0 + scale) + shift


def _adaln_kernel(s_ref, w_ref, b_ref, o_ref):
    s = _silu(s_ref[...])
    o_ref[...] = _dot_hi(s, w_ref[0]) + b_ref[0]


def _adaln(rows, w, b, layer):
    depth, d, n = w.shape
    tn = n // 4
    return pl.pallas_call(
        _adaln_kernel,
        grid=(n // tn,),
        in_specs=[pl.BlockSpec((8, d), lambda j: (0, 0)),
                  pl.BlockSpec((1, d, tn), lambda j: (layer, 0, j)),
                  pl.BlockSpec((1, 1, tn), lambda j: (layer, 0, j))],
        out_specs=pl.BlockSpec((8, tn), lambda j: (0, j)),
        out_shape=jax.ShapeDtypeStruct((8, n), F32),
        compiler_params=_cparams("arbitrary"),
        name="adaln",
    )(rows, w, b.reshape(depth, 1, n))


def _hidden_operands(h, nct):
    if isinstance(h, tuple):
        hc, hl = h
        d = hc.shape[-1]
        specs = [pl.BlockSpec((1, TILE, d), lambda b, i: (b, jnp.minimum(i, nct - 1), 0)),
                 pl.BlockSpec((1, TILE, d), lambda b, i: (b, jnp.maximum(i - nct, 0), 0))]
        return [hc, hl], specs, hc.shape[1] + hl.shape[1]
    return [h], [pl.BlockSpec((1, TILE, h.shape[-1]), lambda b, i: (b, i, 0))], h.shape[1]


def _hidden_tile(h_refs, nct):
    if len(h_refs) == 2:
        return jnp.where(pl.program_id(1) < nct, h_refs[0][0], h_refs[1][0])
    return h_refs[0][0]


def _modproj_kernel(*refs, n_h, nct):
    mod_ref, g_ref, w_ref, o_ref = refs[n_h:]
    mod = mod_ref[0, 0]
    a = _modulate(_hidden_tile(refs[:n_h], nct), g_ref[...], mod[0:1], mod[1:2])
    o_ref[0] = _dot(a.astype(BF16), w_ref[...]).astype(o_ref.dtype)


def _modproj(h, mod, g, w_bf16, nct, out_dtype):
    h_ops, h_specs, t = _hidden_operands(h, nct)
    bsz, _, d = h_ops[0].shape
    n = w_bf16.shape[1]
    return pl.pallas_call(
        functools.partial(_modproj_kernel, n_h=len(h_ops), nct=nct),
        grid=(bsz, t // TILE),
        in_specs=h_specs + [pl.BlockSpec((1, 1, 6, d), lambda b, i: (b, jnp.where(i < nct, 0, 1), 0, 0)),
                            pl.BlockSpec((1, d), lambda b, i: (0, 0)),
                            pl.BlockSpec((d, n), lambda b, i: (0, 0))],
        out_specs=pl.BlockSpec((1, TILE, n), lambda b, i: (b, i, 0)),
        out_shape=jax.ShapeDtypeStruct((bsz, t, n), out_dtype),
        compiler_params=_cparams("arbitrary", "arbitrary"),
        name="modproj",
    )(*h_ops, mod, g.reshape(1, d), w_bf16)


def _router_kernel(h_ref, mod_ref, g_ref, wr_ref, a_ref, aff_ref, afft_ref, *, n_exp):
    mod = mod_ref[0, 0]
    a = _modulate(h_ref[0], g_ref[...], mod[3:4], mod[4:5])
    a_ref[0] = a.astype(BF16)
    logits = _dot_hi(a, wr_ref[...])
    lane = lax.broadcasted_iota(I32, logits.shape, 1)
    logits = jnp.where(lane < n_exp, logits, -jnp.inf)
    p = jnp.exp(logits - jnp.max(logits, axis=-1, keepdims=True))
    aff = p / jnp.sum(p, axis=-1, keepdims=True)
    aff_ref[0] = aff
    afft_ref[0] = aff.T


def _router(h, mod, g, w_router, nct):
    bsz, t, d = h.shape
    n_exp = w_router.shape[1]
    wr = jnp.zeros((d, LANES), F32).at[:, :n_exp].set(w_router)
    return pl.pallas_call(
        functools.partial(_router_kernel, n_exp=n_exp),
        grid=(bsz, t // TILE),
        in_specs=[pl.BlockSpec((1, TILE, d), lambda b, i: (b, i, 0)),
                  pl.BlockSpec((1, 1, 6, d), lambda b, i: (b, jnp.where(i < nct, 0, 1), 0, 0)),
                  pl.BlockSpec((1, d), lambda b, i: (0, 0)),
                  pl.BlockSpec((d, LANES), lambda b, i: (0, 0))],
        out_specs=[pl.BlockSpec((1, TILE, d), lambda b, i: (b, i, 0)),
                   pl.BlockSpec((1, TILE, LANES), lambda b, i: (b, i, 0)),
                   pl.BlockSpec((1, LANES, TILE), lambda b, i: (b, 0, i))],
        out_shape=[jax.ShapeDtypeStruct((bsz, t, d), BF16),
                   jax.ShapeDtypeStruct((bsz, t, LANES), F32),
                   jax.ShapeDtypeStruct((bsz, LANES, t), F32)],
        compiler_params=_cparams("arbitrary", "arbitrary"),
        name="router",
    )(h, mod, g.reshape(1, d), wr)


def _select_kernel(afft_ref, pos_ref, post_ref, cb_ref, cnt_ref, sel_ref, bucket_ref, *, n_exp, n_ctx, k_ctx, k_lat, t):
    tok = lax.broadcasted_iota(I32, (n_exp, t), 1)
    is_ctx = tok < n_ctx
    min_normal = 0x00800000

    def choose(in_set, k):
        def vals():
            return jnp.where(in_set, afft_ref[0][:n_exp], -1.0)

        def bit_step(it, thr):
            cand = thr | jnp.left_shift(jnp.int32(1), 30 - it)
            cnt = jnp.sum(jnp.where(vals() >= pltpu.bitcast(cand, F32), 1, 0), axis=1, keepdims=True)
            return jnp.where(cnt >= k, cand, thr)
        thr = lax.fori_loop(0, 31, bit_step, jnp.zeros((n_exp, 1), I32))
        lo = pltpu.bitcast(thr, F32)
        hi = pltpu.bitcast(jnp.maximum(thr + 1, min_normal), F32)
        v = vals()
        above = v >= hi
        sel_ref[...] = jnp.where(above, 1.0, sel_ref[...])
        bucket_ref[...] = jnp.where((v >= lo) & jnp.logical_not(above), 1.0, 0.0)
        need0 = k - jnp.sum(jnp.where(above, 1, 0), axis=1, keepdims=True)

        def peel(need):
            in_bucket = bucket_ref[...] > 0.0
            v = vals()
            top = jnp.max(jnp.where(in_bucket, v, -1.0), axis=1, keepdims=True)
            first = jnp.min(jnp.where(in_bucket & (v == top), tok, t), axis=1, keepdims=True)
            pick = (tok == first) & (need > 0)
            sel_ref[...] = jnp.where(pick, 1.0, sel_ref[...])
            bucket_ref[...] = jnp.where(pick, 0.0, bucket_ref[...])
            return need - 1
        lax.while_loop(lambda need: jnp.max(need) > 0, peel, need0)

    sel_ref[...] = jnp.zeros_like(sel_ref)
    choose(jnp.logical_not(is_ctx), k_lat)
    if k_ctx > 0:
        choose(is_ctx, k_ctx)

    rr = lax.broadcasted_iota(I32, (LANES, LANES), 0)
    cc = lax.broadcasted_iota(I32, (LANES, LANES), 1)
    upper = jnp.where(rr <= cc, 1.0, 0.0).astype(BF16)
    carry = jnp.zeros((n_exp, 1), F32)
    for blk in range(t // LANES):
        c = _dot(sel_ref[:, blk * LANES:(blk + 1) * LANES].astype(BF16), upper) + carry
        cnt_ref[:, blk * LANES:(blk + 1) * LANES] = c
        carry = c[:, LANES - 1:LANES]
    posinc = cnt_ref[...].astype(I32)
    pos = jnp.where(sel_ref[...] > 0.0, posinc - 1, -1)
    pos_ref[0] = pos

    lane = lax.broadcasted_iota(I32, (n_exp, LANES), 1)
    cb = jnp.zeros((n_exp, LANES), I32)
    for i in range(1, t // SUBT + 1):
        cb = jnp.where(lane == i, posinc[:, i * SUBT - 1:i * SUBT], cb)
    cb_ref[0] = cb

    padded = jnp.concatenate([pos.astype(F32), jnp.full((LANES - n_exp, t), -1.0, F32)], axis=0)
    for i in range(t // TILE):
        post_ref[0, i * TILE:(i + 1) * TILE, :] = padded[:, i * TILE:(i + 1) * TILE].T.astype(I32)


def _select(afft, n_exp, n_ctx, k_ctx, k_lat):
    bsz, _, t = afft.shape
    assert n_ctx % LANES == 0 and t % TILE == 0 and t // SUBT < LANES
    return pl.pallas_call(
        functools.partial(_select_kernel, n_exp=n_exp, n_ctx=n_ctx, k_ctx=k_ctx, k_lat=k_lat, t=t),
        grid=(bsz,),
        in_specs=[pl.BlockSpec((1, LANES, t), lambda b: (b, 0, 0))],
        out_specs=[pl.BlockSpec((1, n_exp, t), lambda b: (b, 0, 0)),
                   pl.BlockSpec((1, t, LANES), lambda b: (b, 0, 0)),
                   pl.BlockSpec((1, n_exp, LANES), lambda b: (b, 0, 0))],
        out_shape=[jax.ShapeDtypeStruct((bsz, n_exp, t), I32),
                   jax.ShapeDtypeStruct((bsz, t, LANES), I32),
                   jax.ShapeDtypeStruct((bsz, n_exp, LANES), I32)],
        scratch_shapes=[pltpu.VMEM((n_exp, t), F32)] * 3,
        compiler_params=_cparams("arbitrary"),
        name="select",
    )(afft)


def _ffn_rows(cap_tot):
    for rc in range(min(cap_tot, 512), 7, -8):
        if cap_tot % rc == 0:
            return rc
    raise ValueError(cap_tot)


def _expert_kernel(cb_ref, pos_ref, a_ref, wg_ref, wu_ref, wd_ref, y_ref, x_ref, *, n_exp, nsub, sub_per, cap_tot):
    e, b, ch = pl.program_id(0), pl.program_id(1), pl.program_id(2)

    @pl.when(ch == 0)
    def _():
        x_ref[...] = jnp.zeros_like(x_ref)

    base = (b * n_exp + e) * (nsub + 1) + ch * sub_per
    row = lax.broadcasted_iota(I32, (WIN_X, SUBT), 0)

    def gather(j, carry):
        start = cb_ref[base + j]
        start_al = pl.multiple_of((start // 8) * 8, 8)
        rel = pos_ref[0, 0, j] - start_al
        onehot = jnp.where(row == rel, 1.0, 0.0).astype(BF16)
        tok0 = pl.multiple_of(j * SUBT, SUBT)
        x_ref[pl.ds(start_al, WIN_X), :] += _dot(onehot, a_ref[0, pl.ds(tok0, SUBT), :])
        return carry
    lax.fori_loop(0, sub_per, gather, 0, unroll=4)

    @pl.when(ch == pl.num_programs(2) - 1)
    def _():
        rc = _ffn_rows(cap_tot)
        wg = wg_ref[0, 0].astype(BF16)
        wu = wu_ref[0, 0].astype(BF16)
        wd = wd_ref[0, 0].astype(BF16)
        for r0 in range(0, cap_tot, rc):
            xc = x_ref[r0:r0 + rc, :].astype(BF16)
            hid = _silu(_dot(xc, wg)) * _dot(xc, wu)
            y_ref[0, 0, r0:r0 + rc, :] = _dot(hid.astype(BF16), wd).astype(y_ref.dtype)
        y_ref[0, 0, cap_tot:, :] = jnp.zeros((y_ref.shape[2] - cap_tot, y_ref.shape[-1]), y_ref.dtype)


def _experts(cb_flat, pos, a, wg, wu, wd, layer, cap_tot):
    bsz, n_exp, t = pos.shape
    d = a.shape[-1]
    ff = wg.shape[-1]
    nsub = t // SUBT
    ntiles = t // TILE
    chunk = TILE * max(m for m in range(1, 13) if ntiles % m == 0)
    sub_per = chunk // SUBT
    assert cap_tot % 16 == 0
    rows_x = cap_tot + WIN_X
    rows_y = cap_tot + WIN_Y
    pos5 = pos.reshape(bsz, n_exp, nsub, 1, SUBT)
    grid_spec = pltpu.PrefetchScalarGridSpec(
        num_scalar_prefetch=1,
        grid=(n_exp, bsz, t // chunk),
        in_specs=[pl.BlockSpec((1, 1, sub_per, 1, SUBT), lambda e, b, c, cb: (b, e, c, 0, 0)),
                  pl.BlockSpec((1, chunk, d), lambda e, b, c, cb: (b, c, 0)),
                  pl.BlockSpec((1, 1, d, ff), lambda e, b, c, cb: (layer, e, 0, 0)),
                  pl.BlockSpec((1, 1, d, ff), lambda e, b, c, cb: (layer, e, 0, 0)),
                  pl.BlockSpec((1, 1, ff, d), lambda e, b, c, cb: (layer, e, 0, 0))],
        out_specs=pl.BlockSpec((1, 1, rows_y, d), lambda e, b, c, cb: (b, e, 0, 0)),
        scratch_shapes=[pltpu.VMEM((rows_x, d), F32)],
    )
    return pl.pallas_call(
        functools.partial(_expert_kernel, n_exp=n_exp, nsub=nsub, sub_per=sub_per, cap_tot=cap_tot),
        grid_spec=grid_spec,
        out_shape=jax.ShapeDtypeStruct((bsz, n_exp, rows_y, d), BF16),
        compiler_params=_cparams("arbitrary", "arbitrary", "arbitrary"),
        name="experts",
    )(cb_flat, pos5, a, wg, wu, wd)


def _combine(cb_flat, h, mod, aff, post, y, g_final, nct, final_norm):
    bsz, t, d = h.shape
    n_exp = y.shape[1]
    ntiles = t // TILE
    first = nct if final_norm else 0
    n_win = n_exp * (TILE // SUBT)
    grid_spec = pltpu.PrefetchScalarGridSpec(
        num_scalar_prefetch=1,
        grid=(bsz, ntiles - first),
        in_specs=[pl.BlockSpec((1, TILE, d), lambda b, i, cb: (b, i + first, 0)),
                  pl.BlockSpec((1, 1, 6, d), lambda b, i, cb: (b, jnp.where(i + first < nct, 0, 1), 0, 0)),
                  pl.BlockSpec((1, TILE, LANES), lambda b, i, cb: (b, i + first, 0)),
                  pl.BlockSpec((1, TILE, LANES), lambda b, i, cb: (b, i + first, 0)),
                  pl.BlockSpec(memory_space=pl.ANY),
                  pl.BlockSpec((1, d), lambda b, i, cb: (0, 0))],
        out_specs=pl.BlockSpec((1, TILE, d), lambda b, i, cb: (b, i, 0)),
        scratch_shapes=[pltpu.VMEM((n_win, WIN_Y, d), y.dtype), pltpu.SemaphoreType.DMA((n_win,))],
    )
    return pl.pallas_call(
        functools.partial(_combine_kernel, n_exp=n_exp, nsub=t // SUBT, first=first, final_norm=final_norm),
        grid_spec=grid_spec,
        out_shape=jax.ShapeDtypeStruct((bsz, t - first * TILE, d), F32),
        compiler_params=_cparams("arbitrary", "arbitrary"),
        name="combine",
    )(cb_flat, h, mod, aff, post, y, g_final.reshape(1, d))


def _combine_kernel(cb_ref, h_ref, mod_ref, aff_ref, post_ref, y_hbm, g_ref, o_ref, ybuf, sem,
                    *, n_exp, nsub, first, final_norm):
    b, i = pl.program_id(0), pl.program_id(1) + first
    per_tile = TILE // SUBT

    def window(sub, e):
        start = cb_ref[(b * n_exp + e) * (nsub + 1) + i * per_tile + sub]
        start_al = pl.multiple_of((start // 16) * 16, 16)
        slot = sub * n_exp + e
        copy = pltpu.make_async_copy(y_hbm.at[b, e, pl.ds(start_al, WIN_Y), :], ybuf.at[slot], sem.at[slot])
        return copy, start_al

    for sub in range(per_tile):
        for e in range(n_exp):
            window(sub, e)[0].start()
    lane = lax.broadcasted_iota(I32, (SUBT, WIN_Y), 1)
    mod = mod_ref[0, 0]
    for sub in range(per_tile):
        rows = slice(sub * SUBT, (sub + 1) * SUBT)
        post = post_ref[0, rows, :]
        aff = aff_ref[0, rows, :]
        acc = jnp.zeros((SUBT, o_ref.shape[-1]), F32)
        starts = []
        for e in range(n_exp):
            copy, start_al = window(sub, e)
            copy.wait()
            starts.append(start_al)
        for e, start_al in enumerate(starts):
            rel = post[:, e:e + 1] - start_al
            onehot = jnp.where(lane == rel, 1.0, 0.0).astype(BF16)
            acc = acc + aff[:, e:e + 1] * _dot(onehot, ybuf[sub * n_exp + e])
        out = h_ref[0, rows, :] + mod[5:6] * acc
        if final_norm:
            ms = jnp.mean(out * out, axis=-1, keepdims=True)
            out = out * lax.rsqrt(ms + EPS) * g_ref[...]
        o_ref[0, rows, :] = out


def _moe(h, mod, ln2_g, w_router, wg, wu, wd, layer, g_final, *, n_ctx, route_ctx, final_norm):
    bsz, t, d = h.shape
    n_exp = w_router.shape[1]
    nct = n_ctx // TILE
    n_lat = t - n_ctx
    k_lat = max(1, EC_CAPACITY * n_lat // n_exp)
    k_ctx = max(1, EC_CAPACITY * n_ctx // n_exp) if route_ctx else 0
    a, aff, afft = _router(h, mod, ln2_g, w_router, nct)
    pos, post, cb = _select(afft, n_exp, n_ctx, k_ctx, k_lat)
    cb_flat = cb[:, :, :t // SUBT + 1].reshape(-1)
    y = _experts(cb_flat, pos, a, wg, wu, wd, layer, k_ctx + k_lat)
    return _combine(cb_flat, h, mod, aff, post, y, g_final, nct, final_norm)


HEAD_SHIFT = int(math.log2(HEAD_DIM))


def _same_head(rows, cols):
    r = lax.shift_right_logical(lax.broadcasted_iota(I32, (rows, cols), 0), HEAD_SHIFT)
    c = lax.shift_right_logical(lax.broadcasted_iota(I32, (rows, cols), 1), HEAD_SHIFT)
    return jnp.where(r == c, 1.0, 0.0)


def _head_sum(x2):
    n = x2.shape[-1]
    return _dot_exact_rhs(x2, _same_head(n, n).astype(BF16))


def _outproj_even_kernel(*refs, n_h, nct, a_dim):
    mod_ref, of_ref, ob_ref, z_ref, at_ref, g_ref, w_ref, o_ref = refs[n_h:]
    mod = mod_ref[0, 0]
    o = of_ref[0] + ob_ref[0]
    ms = _head_sum(o * o) * (1.0 / HEAD_DIM)
    ya = o * lax.rsqrt(ms + EPS) * g_ref[...] * _silu(z_ref[0])
    y = _dot(ya.astype(BF16), w_ref[:a_dim, :]) + _dot(at_ref[0], w_ref[a_dim:, :])
    o_ref[0] = _hidden_tile(refs[:n_h], nct) + mod[2:3] * y


def _outproj_even(h, mod, o_delta, p, o_attn, out_norm_g, w_bf16, nct, *, a_dim):
    h_ops, h_specs, t = _hidden_operands(h, nct)
    bsz, _, d = h_ops[0].shape
    o_f, o_b = o_delta
    g = jnp.tile(out_norm_g, a_dim // HEAD_DIM).reshape(1, a_dim)
    tok = lambda b, i: (b, i, 0)
    return pl.pallas_call(
        functools.partial(_outproj_even_kernel, n_h=len(h_ops), nct=nct, a_dim=a_dim),
        grid=(bsz, t // TILE),
        in_specs=h_specs + [pl.BlockSpec((1, 1, 6, d), lambda b, i: (b, jnp.where(i < nct, 0, 1), 0, 0)),
                            pl.BlockSpec((1, TILE, a_dim), tok),
                            pl.BlockSpec((1, TILE, a_dim), tok),
                            pl.BlockSpec((1, TILE, a_dim), lambda b, i: (b, i, 3)),
                            pl.BlockSpec((1, TILE, a_dim), tok),
                            pl.BlockSpec((1, a_dim), lambda b, i: (0, 0)),
                            pl.BlockSpec((d, d), lambda b, i: (0, 0))],
        out_specs=pl.BlockSpec((1, TILE, d), tok),
        out_shape=jax.ShapeDtypeStruct((bsz, t, d), F32),
        compiler_params=_cparams("arbitrary", "arbitrary"),
        name="outproj_even",
    )(*h_ops, mod, o_f, o_b, p, o_attn, g, w_bf16)


def _outproj_odd_kernel(h_ref, mod_ref, a_ref, w_ref, o_ref):
    mod = mod_ref[0, 0]
    o_ref[0] = h_ref[0] + mod[2:3] * _dot(a_ref[0], w_ref[...])


def _outproj_odd(h, mod, o_na, w_bf16, nct):
    bsz, t, d = h.shape
    n_lat = o_na.shape[1]
    return pl.pallas_call(
        _outproj_odd_kernel,
        grid=(bsz, n_lat // TILE),
        in_specs=[pl.BlockSpec((1, TILE, d), lambda b, i: (b, i + nct, 0)),
                  pl.BlockSpec((1, 1, 6, d), lambda b, i: (b, 1, 0, 0)),
                  pl.BlockSpec((1, TILE, d), lambda b, i: (b, i, 0)),
                  pl.BlockSpec((d, d), lambda b, i: (0, 0))],
        out_specs=pl.BlockSpec((1, TILE, d), lambda b, i: (b, i, 0)),
        out_shape=jax.ShapeDtypeStruct((bsz, n_lat, d), F32),
        compiler_params=_cparams("arbitrary", "arbitrary"),
        name="outproj_odd",
    )(h, mod, o_na, w_bf16)


def _rope_tables(n_ctx, n_lat):
    tt = jnp.arange(n_lat)
    n_freq = HEAD_DIM // 4
    inv = ROPE_THETA ** (-jnp.arange(n_freq, dtype=F32) / n_freq)
    ang = jnp.concatenate([(tt // GRID_W).astype(F32)[:, None] * inv, (tt % GRID_W).astype(F32)[:, None] * inv], -1)
    cos = jnp.concatenate([jnp.ones((n_ctx, 2 * n_freq), F32), jnp.cos(ang)], axis=0)
    sin = jnp.concatenate([jnp.zeros((n_ctx, 2 * n_freq), F32), jnp.sin(ang)], axis=0)
    reps = LANES // HEAD_DIM
    return jnp.tile(cos, (1, 2 * reps)), jnp.tile(jnp.concatenate([-sin, sin], axis=1), (1, reps))


def _rope(x, cos, sin):
    n = x.shape[-1]
    half = HEAD_DIM // 2
    lane = lax.broadcasted_iota(I32, x.shape, 1)
    first = (lane & (HEAD_DIM - 1)) < half
    partner = jnp.where(first, pltpu.roll(x, n - half, 1), pltpu.roll(x, half, 1))
    reps = n // LANES
    cos_n = jnp.concatenate([cos] * reps, axis=1) if reps > 1 else cos
    sin_n = jnp.concatenate([sin] * reps, axis=1) if reps > 1 else sin
    return x * cos_n + partner * sin_n


def _gqa_prep_kernel(q_ref, k_ref, v_ref, cos_ref, sin_ref, gq_ref, gk_ref, qo_ref, ko_ref, vo_ref):
    cos, sin = cos_ref[...], sin_ref[...]
    q = q_ref[0]
    q = q * lax.rsqrt(_head_sum(q * q) * (1.0 / HEAD_DIM) + EPS) * gq_ref[...] * (HEAD_DIM ** -0.5 * LOG2_E)
    qo_ref[0] = _rope(q, cos, sin).astype(BF16)
    k = k_ref[0]
    k = k * lax.rsqrt(_head_sum(k * k) * (1.0 / HEAD_DIM) + EPS) * gk_ref[...]
    k = _rope(k, cos, sin).astype(BF16)
    v = v_ref[0].astype(BF16)
    for hh in range(k.shape[-1] // HEAD_DIM):
        ko_ref[0, hh] = k[:, hh * HEAD_DIM:(hh + 1) * HEAD_DIM]
        vo_ref[0, hh] = v[:, hh * HEAD_DIM:(hh + 1) * HEAD_DIM]


def _gqa_prep(p, cos, sin, q_norm_g, k_norm_g, *, a_dim, kv_dim):
    bsz, t, _ = p.shape
    q_dim = a_dim
    kvh = kv_dim // HEAD_DIM
    assert kv_dim == LANES
    c_q = (4 * a_dim) // q_dim
    c_k = (4 * a_dim + q_dim) // kv_dim
    gq = jnp.tile(q_norm_g, q_dim // HEAD_DIM).reshape(1, q_dim)
    gk = jnp.tile(k_norm_g, kvh).reshape(1, kv_dim)
    return pl.pallas_call(
        _gqa_prep_kernel,
        grid=(bsz, t // TILE),
        in_specs=[pl.BlockSpec((1, TILE, q_dim), lambda b, i: (b, i, c_q)),
                  pl.BlockSpec((1, TILE, kv_dim), lambda b, i: (b, i, c_k)),
                  pl.BlockSpec((1, TILE, kv_dim), lambda b, i: (b, i, c_k + 1)),
                  pl.BlockSpec((TILE, LANES), lambda b, i: (i, 0)),
                  pl.BlockSpec((TILE, LANES), lambda b, i: (i, 0)),
                  pl.BlockSpec((1, q_dim), lambda b, i: (0, 0)),
                  pl.BlockSpec((1, kv_dim), lambda b, i: (0, 0))],
        out_specs=[pl.BlockSpec((1, TILE, q_dim), lambda b, i: (b, i, 0)),
                   pl.BlockSpec((1, kvh, TILE, HEAD_DIM), lambda b, i: (b, 0, i, 0)),
                   pl.BlockSpec((1, kvh, TILE, HEAD_DIM), lambda b, i: (b, 0, i, 0))],
        out_shape=[jax.ShapeDtypeStruct((bsz, t, q_dim), BF16),
                   jax.ShapeDtypeStruct((bsz, kvh, t, HEAD_DIM), BF16),
                   jax.ShapeDtypeStruct((bsz, kvh, t, HEAD_DIM), BF16)],
        compiler_params=_cparams("arbitrary", "arbitrary"),
        name="gqa_prep",
    )(p, p, p, cos, sin, gq, gk)


SUB_K = 256
Q_SLAB = 256


def _gqa_kernel(q_ref, k_ref, v_ref, o_ref, qs_ref, m_ref, l_ref, acc_ref, *, n_ctx, nct, tq, tk, group):
    i, kv = pl.program_id(2), pl.program_id(3)

    @pl.when(kv == 0)
    def _():
        m_ref[...] = jnp.full_like(m_ref, -jnp.inf)
        l_ref[...] = jnp.zeros_like(l_ref)
        acc_ref[...] = jnp.zeros_like(acc_ref)
        q = q_ref[0]
        for g in range(group):
            qs_ref[g * tq:(g + 1) * tq, :] = q[:, g * HEAD_DIM:(g + 1) * HEAD_DIM]

    def attend(keys, vals, n_valid):
        starts = list(range(0, group * tq, Q_SLAB))
        scores = {starts[0]: _dot_t(qs_ref[0:Q_SLAB, :], keys)}
        for idx, r0 in enumerate(starts):
            rows = slice(r0, r0 + Q_SLAB)
            if idx + 1 < len(starts):
                nxt = starts[idx + 1]
                scores[nxt] = _dot_t(qs_ref[nxt:nxt + Q_SLAB, :], keys)
            s = scores.pop(r0)
            if n_valid is not None:
                s = jnp.where(lax.broadcasted_iota(I32, s.shape, 1) < n_valid, s, -jnp.inf)
            m_old = m_ref[rows, :]
            m_new = jnp.maximum(m_old, jnp.max(s, axis=-1, keepdims=True))
            alpha = jnp.exp2(m_old - m_new)
            p = jnp.exp2(s - m_new)
            l_ref[rows, :] = alpha * l_ref[rows, :] + jnp.sum(p, axis=-1, keepdims=True)
            acc_ref[rows, :] = alpha * acc_ref[rows, :] + _dot(p.astype(BF16), vals)
            m_ref[rows, :] = m_new

    @pl.when(i >= nct)
    def _():
        attend(k_ref[0, 0], v_ref[0, 0], None)

    @pl.when((i < nct) & (kv == 0))
    def _():
        n_keys = -(-n_ctx // SUB_K) * SUB_K
        attend(k_ref[0, 0, :n_keys, :], v_ref[0, 0, :n_keys, :], n_ctx if n_keys != n_ctx else None)

    @pl.when(kv == pl.num_programs(3) - 1)
    def _():
        out = acc_ref[...] / l_ref[...]
        for g in range(group):
            o_ref[0, :, g * HEAD_DIM:(g + 1) * HEAD_DIM] = out[g * tq:(g + 1) * tq, :].astype(o_ref.dtype)


def _gqa_attention(q, k, v, *, n_ctx):
    bsz, t, q_dim = q.shape
    kvh = k.shape[1]
    group = q_dim // HEAD_DIM // kvh
    tq = TILE
    nt = t // TILE
    tk = TILE * max(m for m in range(1, 13) if nt % m == 0)
    assert n_ctx <= tk and tk % SUB_K == 0
    return pl.pallas_call(
        functools.partial(_gqa_kernel, n_ctx=n_ctx, nct=n_ctx // tq, tq=tq, tk=tk, group=group),
        grid=(bsz, kvh, t // tq, t // tk),
        in_specs=[pl.BlockSpec((1, tq, group * HEAD_DIM), lambda b, h, i, j: (b, i, h)),
                  pl.BlockSpec((1, 1, tk, HEAD_DIM), lambda b, h, i, j: (b, h, j, 0)),
                  pl.BlockSpec((1, 1, tk, HEAD_DIM), lambda b, h, i, j: (b, h, j, 0))],
        out_specs=pl.BlockSpec((1, tq, group * HEAD_DIM), lambda b, h, i, j: (b, i, h)),
        out_shape=jax.ShapeDtypeStruct((bsz, t, q_dim), BF16),
        scratch_shapes=[pltpu.VMEM((group * tq, HEAD_DIM), BF16),
                        pltpu.VMEM((group * tq, 1), F32),
                        pltpu.VMEM((group * tq, 1), F32),
                        pltpu.VMEM((group * tq, HEAD_DIM), F32)],
        compiler_params=_cparams("arbitrary", "arbitrary", "arbitrary", "arbitrary"),
        name="gqa_attention",
    )(q, k, v)


def _natten_bias(rpb):
    n_heads, n_off, n_rel = rpb.shape
    col = np.arange(GRID_W)
    col_start = np.clip(col - NA_COLS // 2, 0, GRID_W - NA_COLS)
    valid = (col[None, :] >= col_start[:, None]) & (col[None, :] < col_start[:, None] + NA_COLS)
    rel = col[None, :] - col[:, None] + NA_COLS - 1
    onehot = ((rel[None] == np.arange(n_rel)[:, None, None]) & valid[None]).astype(np.float32)
    tab = jnp.einsum('hrk,kcd->hrcd', rpb, jnp.asarray(onehot), precision=lax.Precision.HIGHEST)
    tab = jnp.where(valid[None, None], tab, -jnp.inf)
    pairs = jnp.concatenate([tab[:, :-1], tab[:, 1:]], axis=-1)
    return pairs.reshape(n_heads // 2, 2, n_off - 1, GRID_W, 2 * GRID_W).astype(F32)


def _natten_kernel(q_ref, k_ref, v_ref, bias_ref, o_ref, *, n_ctx, n_rows, rblk):
    rb = pl.program_id(2)
    first_head = lax.broadcasted_iota(I32, (1, LANES), 1) < HEAD_DIM
    kc = k_ref[0, 0:n_ctx, :]
    vc = v_ref[0, 0:n_ctx, :]
    band = NA_ROWS * GRID_W
    qs, ks, vs, biases = [], [], [], []
    for rr in range(rblk):
        rg = rb * rblk + rr
        rs = jnp.clip(rg - NA_ROWS // 2, 0, n_rows - NA_ROWS)
        variant = rs - rg + (NA_ROWS - 1)
        start = pl.multiple_of(n_ctx + rs * GRID_W, GRID_W)
        keys = jnp.concatenate([kc, k_ref[0, pl.ds(start, band), :]], axis=0)
        vals = jnp.concatenate([vc, v_ref[0, pl.ds(start, band), :]], axis=0)
        q = q_ref[0, rr * GRID_W:(rr + 1) * GRID_W, :] * (HEAD_DIM ** -0.5)
        for hh in range(2):
            qs.append(jnp.where(first_head if hh == 0 else jnp.logical_not(first_head), q, jnp.zeros_like(q)))
            ks.append(keys)
            vs.append(vals)
            biases.append(jnp.concatenate([bias_ref[0, hh, variant + j] for j in range(0, NA_ROWS, 2)], axis=1))
    batch = (((2,), (2,)), ((0,), (0,)))
    s = lax.dot_general(jnp.stack(qs), jnp.stack(ks), batch, preferred_element_type=F32)
    sc = s[:, :, :n_ctx]
    sb = s[:, :, n_ctx:] + jnp.stack(biases)
    m = jnp.maximum(jnp.max(sc, axis=-1, keepdims=True), jnp.max(sb, axis=-1, keepdims=True))
    pc = jnp.exp(sc - m)
    pb = jnp.exp(sb - m)
    l = jnp.sum(pc, axis=-1, keepdims=True) + jnp.sum(pb, axis=-1, keepdims=True)
    p = jnp.concatenate([pc, pb], axis=-1).astype(BF16)
    o = lax.dot_general(p, jnp.stack(vs), (((2,), (1,)), ((0,), (0,))), preferred_element_type=F32) / l
    for rr in range(rblk):
        o_ref[0, rr * GRID_W:(rr + 1) * GRID_W, :] = jnp.where(first_head, o[2 * rr], o[2 * rr + 1]).astype(o_ref.dtype)


def _natten(p, bias, *, n_ctx, d):
    bsz, t, _ = p.shape
    n_lat = t - n_ctx
    n_rows = n_lat // GRID_W
    rblk = TILE // GRID_W
    npairs = d // LANES
    assert n_rows >= NA_ROWS and n_ctx % TILE == 0
    return pl.pallas_call(
        functools.partial(_natten_kernel, n_ctx=n_ctx, n_rows=n_rows, rblk=rblk),
        grid=(bsz, npairs, n_lat // TILE),
        in_specs=[pl.BlockSpec((1, TILE, LANES), lambda b, hp, r: (b, n_ctx // TILE + r, hp)),
                  pl.BlockSpec((1, t, LANES), lambda b, hp, r: (b, 0, npairs + hp)),
                  pl.BlockSpec((1, t, LANES), lambda b, hp, r: (b, 0, 2 * npairs + hp)),
                  pl.BlockSpec((1, 2, 2 * NA_ROWS - 2, GRID_W, 2 * GRID_W), lambda b, hp, r: (hp, 0, 0, 0, 0))],
        out_specs=pl.BlockSpec((1, TILE, LANES), lambda b, hp, r: (b, r, hp)),
        out_shape=jax.ShapeDtypeStruct((bsz, n_lat, d), BF16),
        compiler_params=_cparams("arbitrary", "arbitrary", "arbitrary"),
        name="natten",
    )(p, p, p, bias)


def _dot_exact_lhs(a_bf16, b):
    b1, b2, b3 = _split3(b)
    return _dot(a_bf16, b1) + _dot(a_bf16, b2) + _dot(a_bf16, b3)


def _delta_prep_kernel(x_ref, prev_ref, next_ref, gate_ref, w_ref, alog_ref, dt_ref, xh_ref, gn_ref, gnt_ref, xs_ref,
                       *, nct, a_dim):
    i = pl.program_id(1)
    last = pl.num_programs(1) - 1
    x = x_ref[0]
    rows = x.shape[0]
    xs_ref[0:8, :] = prev_ref[0]
    xs_ref[8:8 + rows, :] = x
    xs_ref[8 + rows:16 + rows, :] = next_ref[0]
    r = lax.broadcasted_iota(I32, (rows, 1), 0)
    at_start = (i == 0) | (i == nct)
    at_end = (i == nct - 1) | (i == last)
    xm = jnp.where((r == 0) & at_start, 0.0, xs_ref[7:7 + rows, :])
    xp = jnp.where((r == rows - 1) & at_end, 0.0, xs_ref[9:9 + rows, :])
    w = w_ref[...]
    y = _silu(w[0:1] * xm + w[1:2] * x + w[2:3] * xp)
    q, k, v = y[:, :a_dim], y[:, a_dim:2 * a_dim], y[:, 2 * a_dim:]
    nh = a_dim // HEAD_DIM
    q = (q * lax.rsqrt(_head_sum(q * q) + EPS) * (HEAD_DIM ** -0.5)).astype(BF16)
    k = (k * lax.rsqrt(_head_sum(k * k) + EPS)).astype(BF16)
    v = v.astype(BF16)
    for kind, arr in enumerate((q, k, v)):
        for hh in range(nh):
            xh_ref[0, kind, hh] = arr[:, hh * HEAD_DIM:(hh + 1) * HEAD_DIM]

    g = gate_ref[0]
    z = g + dt_ref[...]
    softplus = jnp.maximum(z, 0.0) + jnp.log1p(jnp.exp(-jnp.abs(z)))
    log_a = -jnp.exp(alog_ref[...]) * softplus
    beta = _sigmoid(g)
    ri = lax.broadcasted_iota(I32, (rows, rows), 0)
    ci = lax.broadcasted_iota(I32, (rows, rows), 1)
    chunk_shift = int(math.log2(DELTA_CHUNK))
    same_chunk = lax.shift_right_logical(ri, chunk_shift) == lax.shift_right_logical(ci, chunk_shift)
    prefix = jnp.where(same_chunk & (ci <= ri), 1.0, 0.0).astype(BF16)
    suffix = jnp.where(same_chunk & (ci >= ri), 1.0, 0.0).astype(BF16)
    g_f = _dot_exact_lhs(prefix, log_a)
    g_b = _dot_exact_lhs(suffix, log_a)
    lane = lax.broadcasted_iota(I32, g.shape, 1)
    narrow = jnp.where(lane < nh, g_f, jnp.where(lane < 2 * nh, beta, jnp.where(lane < 3 * nh, g_b, beta)))
    gn_ref[0] = narrow
    gnt_ref[0] = narrow.T


def _delta_prep(p, conv_w, a_log, dt_bias, *, n_ctx, a_dim):
    bsz, t, _ = p.shape
    nh = a_dim // HEAD_DIM
    assert DELTA_CHUNK == HEAD_DIM and TILE % DELTA_CHUNK == 0 and 4 * nh <= LANES
    nct = n_ctx // TILE
    c_gate = (4 * a_dim + a_dim + 2 * LANES) // LANES
    alog = jnp.zeros((1, LANES), F32).at[0, :nh].set(a_log[0]).at[0, 2 * nh:3 * nh].set(a_log[1])
    dtb = jnp.zeros((1, LANES), F32).at[0, :nh].set(dt_bias[0]).at[0, 2 * nh:3 * nh].set(dt_bias[1])
    nblk8 = t // 8
    per = TILE // 8
    return pl.pallas_call(
        functools.partial(_delta_prep_kernel, nct=nct, a_dim=a_dim),
        grid=(bsz, t // TILE),
        in_specs=[pl.BlockSpec((1, TILE, 3 * a_dim), lambda b, i: (b, i, 0)),
                  pl.BlockSpec((1, 8, 3 * a_dim), lambda b, i: (b, jnp.maximum(i * per - 1, 0), 0)),
                  pl.BlockSpec((1, 8, 3 * a_dim), lambda b, i: (b, jnp.minimum((i + 1) * per, nblk8 - 1), 0)),
                  pl.BlockSpec((1, TILE, LANES), lambda b, i: (b, i, c_gate)),
                  pl.BlockSpec((3, 3 * a_dim), lambda b, i: (0, 0)),
                  pl.BlockSpec((1, LANES), lambda b, i: (0, 0)),
                  pl.BlockSpec((1, LANES), lambda b, i: (0, 0))],
        out_specs=[pl.BlockSpec((1, 3, nh, TILE, HEAD_DIM), lambda b, i: (b, 0, 0, i, 0)),
                   pl.BlockSpec((1, TILE, LANES), lambda b, i: (b, i, 0)),
                   pl.BlockSpec((1, LANES, TILE), lambda b, i: (b, 0, i))],
        out_shape=[jax.ShapeDtypeStruct((bsz, 3, nh, t, HEAD_DIM), BF16),
                   jax.ShapeDtypeStruct((bsz, t, LANES), F32),
                   jax.ShapeDtypeStruct((bsz, LANES, t), F32)],
        scratch_shapes=[pltpu.VMEM((TILE + 16, 3 * a_dim), F32)],
        compiler_params=_cparams("arbitrary", "arbitrary"),
        name="delta_prep",
    )(p, p, p, p, conv_w, alog, dtb)


def _bdot(a, b, ca, cb, hi=False):
    dims = (((ca,), (cb,)), ((0,), (0,)))
    dot = lambda x, y: lax.dot_general(x, y, dims, preferred_element_type=F32)
    if not hi:
        return dot(a.astype(BF16), b.astype(BF16))
    a1 = a.astype(BF16)
    a2 = (a - a1.astype(F32)).astype(BF16)
    b1 = b.astype(BF16)
    b2 = (b - b1.astype(F32)).astype(BF16)
    return dot(a1, b1) + dot(a1, b2) + dot(a2, b1)


def _delta_heads(q, k, v, g_col, beta_col, g_row, s, upper):
    nb, lc, _ = q.shape
    ri = lax.broadcasted_iota(I32, (nb, lc, lc), 1)
    ci = lax.broadcasted_iota(I32, (nb, lc, lc), 2)
    ahead = jnp.where(upper, ci - ri, ri - ci)
    tri_incl = ahead >= 0
    tri_strict = ahead > 0
    eye = jnp.where(ri == ci, 1.0, 0.0)
    k32, q32, v32 = k.astype(F32), q.astype(F32), v.astype(F32)
    eg = jnp.exp(g_col)
    g_last = jnp.where(upper, g_col[:, 0:1], g_col[:, lc - 1:lc])
    gam = jnp.exp(jnp.where(tri_incl, g_col - g_row, -jnp.inf))
    kq = _bdot(jnp.concatenate([k, q], axis=1), k, 2, 2)
    m = jnp.where(tri_strict, kq[:, :lc] * beta_col * gam, 0.0)
    a_qk = kq[:, lc:] * gam
    n = -m
    x = eye + n
    p = _bdot(n, n, 2, 1, hi=True)
    for _ in range(int(math.log2(lc)) - 2):
        x, p = x + _bdot(p, x, 2, 1, hi=True), _bdot(p, p, 2, 1, hi=True)
    x = x + _bdot(p, x, 2, 1, hi=True)
    u = _bdot(x, v32 * beta_col, 2, 1, hi=True)
    w = _bdot(x, k32 * (beta_col * eg), 2, 1, hi=True)
    ws_qs = _bdot(jnp.concatenate([w, q32 * eg], axis=1), s, 2, 1)
    v_new = u - ws_qs[:, :lc]
    o = ws_qs[:, lc:] + _bdot(a_qk, v_new, 2, 1)
    kv = _bdot(k32 * jnp.exp(g_last - g_col), v_new, 1, 1)
    return o, s * jnp.exp(g_last) + kv


def _delta_scan_kernel(xf_ref, xb_ref, gf_ref, gb_ref, gtf_ref, gtb_ref, of_ref, ob_ref, s_ref, *, nh):
    c = pl.program_id(1)

    @pl.when(c == 0)
    def _():
        s_ref[...] = jnp.zeros_like(s_ref)

    both = lambda f, b: jnp.concatenate([f, b], axis=0)
    q, k, v = (both(xf_ref[0, i], xb_ref[0, i]) for i in range(3))
    g_cols, b_cols, g_rows = [], [], []
    for d_i, (g_ref, gt_ref) in enumerate(((gf_ref, gtf_ref), (gb_ref, gtb_ref))):
        gn = g_ref[0]
        gt = gt_ref[0, 0]
        for hh in range(nh):
            lane_g = 2 * d_i * nh + hh
            g_cols.append(gn[:, lane_g:lane_g + 1])
            b_cols.append(gn[:, lane_g + nh:lane_g + nh + 1])
            g_rows.append(gt[lane_g:lane_g + 1, :])
    upper = lax.broadcasted_iota(I32, (2 * nh, 1, 1), 0) >= nh
    o, s_new = _delta_heads(q, k, v, jnp.stack(g_cols), jnp.stack(b_cols), jnp.stack(g_rows), s_ref[...], upper)
    s_ref[...] = s_new
    for hh in range(nh):
        of_ref[0, :, hh * HEAD_DIM:(hh + 1) * HEAD_DIM] = o[hh]
        ob_ref[0, :, hh * HEAD_DIM:(hh + 1) * HEAD_DIM] = o[nh + hh]


def _delta_scan(xh, gn, gnt, *, n_ctx):
    bsz, _, nh, t, _ = xh.shape
    lc = DELTA_CHUNK
    nc, ncc = t // lc, n_ctx // lc
    gnt4 = gnt.reshape(bsz, LANES, nc, lc).transpose(0, 2, 1, 3)
    rev = lambda c: jnp.where(c < ncc, ncc - 1 - c, nc - 1 - (c - ncc))
    return pl.pallas_call(
        functools.partial(_delta_scan_kernel, nh=nh),
        grid=(bsz, nc),
        in_specs=[pl.BlockSpec((1, 3, nh, lc, HEAD_DIM), lambda b, c: (b, 0, 0, c, 0)),
                  pl.BlockSpec((1, 3, nh, lc, HEAD_DIM), lambda b, c: (b, 0, 0, rev(c), 0)),
                  pl.BlockSpec((1, lc, LANES), lambda b, c: (b, c, 0)),
                  pl.BlockSpec((1, lc, LANES), lambda b, c: (b, rev(c), 0)),
                  pl.BlockSpec((1, 1, LANES, lc), lambda b, c: (b, c, 0, 0)),
                  pl.BlockSpec((1, 1, LANES, lc), lambda b, c: (b, rev(c), 0, 0))],
        out_specs=[pl.BlockSpec((1, lc, nh * HEAD_DIM), lambda b, c: (b, c, 0)),
                   pl.BlockSpec((1, lc, nh * HEAD_DIM), lambda b, c: (b, rev(c), 0))],
        out_shape=[jax.ShapeDtypeStruct((bsz, t, nh * HEAD_DIM), F32),
                   jax.ShapeDtypeStruct((bsz, t, nh * HEAD_DIM), F32)],
        scratch_shapes=[pltpu.VMEM((2 * nh, HEAD_DIM, HEAD_DIM), F32)],
        compiler_params=_cparams("arbitrary", "arbitrary"),
        name="delta_scan",
    )(xh, xh, gn, gn, gnt4, gnt4)


def kernel(x, c, ctx, c_ctx, ada_w, ada_b, ln1_g, ln2_g, even_w_in, even_conv_w, even_a_log, even_dt_bias, even_out_norm_g, even_q_norm_g, even_k_norm_g, even_w_out, odd_w_in, odd_rpb, odd_w_out, router_w, expert_w_gate, expert_w_up, expert_w_down, final_norm_g):
    bsz, n_lat, d = x.shape
    n_ctx = ctx.shape[1]
    t = n_ctx + n_lat
    nct = n_ctx // TILE
    h = (ctx, x)
    rows = jnp.zeros((8, d), F32).at[0].set(c_ctx).at[1:1 + bsz].set(c)

    def mod_of(layer):
        m = _adaln(rows, ada_w, ada_b, layer).reshape(8, 6, d)
        return jnp.stack([jnp.broadcast_to(m[0], (bsz, 6, d)), m[1:1 + bsz]], axis=1)

    mod = mod_of(0)
    a_dim = d // 2
    a_heads = a_dim // HEAD_DIM
    w_in = even_w_in[0]
    o_qkv, o_z, o_g = 0, 3 * a_dim, 4 * a_dim
    o_bq = o_g + 4 * a_heads
    kv_dim = (w_in.shape[1] - o_bq - a_dim) // 2
    o_bk, o_bv = o_bq + a_dim, o_bq + a_dim + kv_dim
    gates_w = jnp.zeros((d, LANES), F32).at[:, :4 * a_heads].set(w_in[:, o_g:o_bq])
    w_perm = jnp.concatenate([w_in[:, :o_g], w_in[:, o_bq:], gates_w], axis=1).astype(BF16)
    p = _modproj(h, mod, ln1_g[0], w_perm, nct, F32)
    xh, gn, gnt = _delta_prep(p, even_conv_w[0], even_a_log[0], even_dt_bias[0], n_ctx=n_ctx, a_dim=a_dim)
    o_delta = _delta_scan(xh, gn, gnt, n_ctx=n_ctx)
    cos, sin = _rope_tables(n_ctx, n_lat)
    qb, kb, vb = _gqa_prep(p, cos, sin, even_q_norm_g[0], even_k_norm_g[0], a_dim=a_dim, kv_dim=kv_dim)
    o_attn = _gqa_attention(qb, kb, vb, n_ctx=n_ctx)
    h = _outproj_even(h, mod, o_delta, p, o_attn, even_out_norm_g[0], even_w_out[0].astype(BF16), nct, a_dim=a_dim)
    h = _moe(h, mod, ln2_g[0], router_w[0], expert_w_gate, expert_w_up, expert_w_down, 0, final_norm_g,
             n_ctx=n_ctx, route_ctx=True, final_norm=False)

    mod = mod_of(1)
    p = _modproj(h, mod, ln1_g[1], odd_w_in[0].astype(BF16), nct, BF16)
    o_na = _natten(p, _natten_bias(odd_rpb[0]), n_ctx=n_ctx, d=d)
    h_lat = _outproj_odd(h, mod, o_na, odd_w_out[0].astype(BF16), nct)
    return _moe(h_lat, mod, ln2_g[1], router_w[1], expert_w_gate, expert_w_up, expert_w_down, 1, final_norm_g,
                n_ctx=0, route_ctx=False, final_norm=True)
```

```python
import functools
import math

import numpy as np
import jax
import jax.numpy as jnp
from jax import lax
from jax.experimental import pallas as pl
from jax.experimental.pallas import tpu as pltpu

F32, BF16, I32 = jnp.float32, jnp.bfloat16, jnp.int32

HEAD_DIM = 64
GRID_W = 64
DELTA_CHUNK = 64
NA_ROWS = 8
NA_COLS = 16
ROPE_THETA = 10000.0
EC_CAPACITY = 2
EPS = 1e-6
LOG2_E = math.log2(math.e)

LANES = 128
TILE = 256
SUBT = 128
WIN_X = SUBT + 8
WIN_Y = SUBT + 16
VMEM_LIMIT = 56 * 1024 * 1024


def _cparams(*sem):
    return pltpu.CompilerParams(dimension_semantics=sem, vmem_limit_bytes=VMEM_LIMIT)


def _split3(a):
    a1 = a.astype(BF16)
    r1 = a - a1.astype(F32)
    a2 = r1.astype(BF16)
    a3 = (r1 - a2.astype(F32)).astype(BF16)
    return a1, a2, a3


def _dot(a, b):
    return jnp.dot(a, b, preferred_element_type=F32)


def _dot_t(a, b):
    return lax.dot_general(a, b, (((1,), (1,)), ((), ())), preferred_element_type=F32)


def _dot_exact_rhs(a, b_bf16):
    a1, a2, a3 = _split3(a)
    return _dot(a1, b_bf16) + _dot(a2, b_bf16) + _dot(a3, b_bf16)


def _dot_hi(a, b):
    a1 = a.astype(BF16)
    a2 = (a - a1.astype(F32)).astype(BF16)
    b1 = b.astype(BF16)
    b2 = (b - b1.astype(F32)).astype(BF16)
    return _dot(a1, b1) + _dot(a1, b2) + _dot(a2, b1)


def _sigmoid(x):
    return 1.0 / (1.0 + jnp.exp(-x))


def _silu(x):
    return x * _sigmoid(x)


def _modulate(x, g, shift, scale):
    ms = jnp.mean(x * x, axis=-1, keepdims=True)
    return x * lax.rsqrt(ms + EPS) * g * (1.0 + scale) + shift


def _adaln_kernel(s_ref, w_ref, b_ref, o_ref):
    s = _silu(s_ref[...])
    o_ref[...] = _dot_hi(s, w_ref[0]) + b_ref[0]


def _adaln(rows, w, b, layer):
    depth, d, n = w.shape
    tn = n // 4
    return pl.pallas_call(
        _adaln_kernel,
        grid=(n // tn,),
        in_specs=[pl.BlockSpec((8, d), lambda j: (0, 0)),
                  pl.BlockSpec((1, d, tn), lambda j: (layer, 0, j)),
                  pl.BlockSpec((1, 1, tn), lambda j: (layer, 0, j))],
        out_specs=pl.BlockSpec((8, tn), lambda j: (0, j)),
        out_shape=jax.ShapeDtypeStruct((8, n), F32),
        compiler_params=_cparams("arbitrary"),
        name="adaln",
    )(rows, w, b.reshape(depth, 1, n))


def _hidden_operands(h, nct):
    if isinstance(h, tuple):
        hc, hl = h
        d = hc.shape[-1]
        specs = [pl.BlockSpec((1, TILE, d), lambda b, i: (b, jnp.minimum(i, nct - 1), 0)),
                 pl.BlockSpec((1, TILE, d), lambda b, i: (b, jnp.maximum(i - nct, 0), 0))]
        return [hc, hl], specs, hc.shape[1] + hl.shape[1]
    return [h], [pl.BlockSpec((1, TILE, h.shape[-1]), lambda b, i: (b, i, 0))], h.shape[1]


def _hidden_tile(h_refs, nct):
    if len(h_refs) == 2:
        return jnp.where(pl.program_id(1) < nct, h_refs[0][0], h_refs[1][0])
    return h_refs[0][0]


def _modproj_kernel(*refs, n_h, nct):
    mod_ref, g_ref, w_ref, o_ref = refs[n_h:]
    mod = mod_ref[0, 0]
    a = _modulate(_hidden_tile(refs[:n_h], nct), g_ref[...], mod[0:1], mod[1:2])
    o_ref[0] = _dot(a.astype(BF16), w_ref[...]).astype(o_ref.dtype)


def _modproj(h, mod, g, w_bf16, nct, out_dtype):
    h_ops, h_specs, t = _hidden_operands(h, nct)
    bsz, _, d = h_ops[0].shape
    n = w_bf16.shape[1]
    return pl.pallas_call(
        functools.partial(_modproj_kernel, n_h=len(h_ops), nct=nct),
        grid=(bsz, t // TILE),
        in_specs=h_specs + [pl.BlockSpec((1, 1, 6, d), lambda b, i: (b, jnp.where(i < nct, 0, 1), 0, 0)),
                            pl.BlockSpec((1, d), lambda b, i: (0, 0)),
                            pl.BlockSpec((d, n), lambda b, i: (0, 0))],
        out_specs=pl.BlockSpec((1, TILE, n), lambda b, i: (b, i, 0)),
        out_shape=jax.ShapeDtypeStruct((bsz, t, n), out_dtype),
        compiler_params=_cparams("arbitrary", "arbitrary"),
        name="modproj",
    )(*h_ops, mod, g.reshape(1, d), w_bf16)


def _router_kernel(h_ref, mod_ref, g_ref, wr_ref, a_ref, aff_ref, afft_ref, *, n_exp):
    mod = mod_ref[0, 0]
    a = _modulate(h_ref[0], g_ref[...], mod[3:4], mod[4:5])
    a_ref[0] = a.astype(BF16)
    logits = _dot_hi(a, wr_ref[...])
    lane = lax.broadcasted_iota(I32, logits.shape, 1)
    logits = jnp.where(lane < n_exp, logits, -jnp.inf)
    p = jnp.exp(logits - jnp.max(logits, axis=-1, keepdims=True))
    aff = p / jnp.sum(p, axis=-1, keepdims=True)
    aff_ref[0] = aff
    afft_ref[0] = aff.T


def _router(h, mod, g, w_router, nct):
    bsz, t, d = h.shape
    n_exp = w_router.shape[1]
    wr = jnp.zeros((d, LANES), F32).at[:, :n_exp].set(w_router)
    return pl.pallas_call(
        functools.partial(_router_kernel, n_exp=n_exp),
        grid=(bsz, t // TILE),
        in_specs=[pl.BlockSpec((1, TILE, d), lambda b, i: (b, i, 0)),
                  pl.BlockSpec((1, 1, 6, d), lambda b, i: (b, jnp.where(i < nct, 0, 1), 0, 0)),
                  pl.BlockSpec((1, d), lambda b, i: (0, 0)),
                  pl.BlockSpec((d, LANES), lambda b, i: (0, 0))],
        out_specs=[pl.BlockSpec((1, TILE, d), lambda b, i: (b, i, 0)),
                   pl.BlockSpec((1, TILE, LANES), lambda b, i: (b, i, 0)),
                   pl.BlockSpec((1, LANES, TILE), lambda b, i: (b, 0, i))],
        out_shape=[jax.ShapeDtypeStruct((bsz, t, d), BF16),
                   jax.ShapeDtypeStruct((bsz, t, LANES), F32),
                   jax.ShapeDtypeStruct((bsz, LANES, t), F32)],
        compiler_params=_cparams("arbitrary", "arbitrary"),
        name="router",
    )(h, mod, g.reshape(1, d), wr)


def _select_kernel(afft_ref, pos_ref, post_ref, cb_ref, cnt_ref, sel_ref, bucket_ref, *, n_exp, n_ctx, k_ctx, k_lat, t):
    tok = lax.broadcasted_iota(I32, (n_exp, t), 1)
    is_ctx = tok < n_ctx
    min_normal = 0x00800000

    def choose(in_set, k):
        def vals():
            return jnp.where(in_set, afft_ref[0][:n_exp], -1.0)

        def bit_step(it, thr):
            cand = thr | jnp.left_shift(jnp.int32(1), 30 - it)
            cnt = jnp.sum(jnp.where(vals() >= pltpu.bitcast(cand, F32), 1, 0), axis=1, keepdims=True)
            return jnp.where(cnt >= k, cand, thr)
        thr = lax.fori_loop(0, 31, bit_step, jnp.zeros((n_exp, 1), I32))
        lo = pltpu.bitcast(thr, F32)
        hi = pltpu.bitcast(jnp.maximum(thr + 1, min_normal), F32)
        v = vals()
        above = v >= hi
        sel_ref[...] = jnp.where(above, 1.0, sel_ref[...])
        bucket_ref[...] = jnp.where((v >= lo) & jnp.logical_not(above), 1.0, 0.0)
        need0 = k - jnp.sum(jnp.where(above, 1, 0), axis=1, keepdims=True)

        def peel(need):
            in_bucket = bucket_ref[...] > 0.0
            v = vals()
            top = jnp.max(jnp.where(in_bucket, v, -1.0), axis=1, keepdims=True)
            first = jnp.min(jnp.where(in_bucket & (v == top), tok, t), axis=1, keepdims=True)
            pick = (tok == first) & (need > 0)
            sel_ref[...] = jnp.where(pick, 1.0, sel_ref[...])
            bucket_ref[...] = jnp.where(pick, 0.0, bucket_ref[...])
            return need - 1
        lax.while_loop(lambda need: jnp.max(need) > 0, peel, need0)

    sel_ref[...] = jnp.zeros_like(sel_ref)
    choose(jnp.logical_not(is_ctx), k_lat)
    if k_ctx > 0:
        choose(is_ctx, k_ctx)

    rr = lax.broadcasted_iota(I32, (LANES, LANES), 0)
    cc = lax.broadcasted_iota(I32, (LANES, LANES), 1)
    upper = jnp.where(rr <= cc, 1.0, 0.0).astype(BF16)
    carry = jnp.zeros((n_exp, 1), F32)
    for blk in range(t // LANES):
        c = _dot(sel_ref[:, blk * LANES:(blk + 1) * LANES].astype(BF16), upper) + carry
        cnt_ref[:, blk * LANES:(blk + 1) * LANES] = c
        carry = c[:, LANES - 1:LANES]
    posinc = cnt_ref[...].astype(I32)
    pos = jnp.where(sel_ref[...] > 0.0, posinc - 1, -1)
    pos_ref[0] = pos

    lane = lax.broadcasted_iota(I32, (n_exp, LANES), 1)
    cb = jnp.zeros((n_exp, LANES), I32)
    for i in range(1, t // SUBT + 1):
        cb = jnp.where(lane == i, posinc[:, i * SUBT - 1:i * SUBT], cb)
    cb_ref[0] = cb

    padded = jnp.concatenate([pos.astype(F32), jnp.full((LANES - n_exp, t), -1.0, F32)], axis=0)
    for i in range(t // TILE):
        post_ref[0, i * TILE:(i + 1) * TILE, :] = padded[:, i * TILE:(i + 1) * TILE].T.astype(I32)


def _select(afft, n_exp, n_ctx, k_ctx, k_lat):
    bsz, _, t = afft.shape
    assert n_ctx % LANES == 0 and t % TILE == 0 and t // SUBT < LANES
    return pl.pallas_call(
        functools.partial(_select_kernel, n_exp=n_exp, n_ctx=n_ctx, k_ctx=k_ctx, k_lat=k_lat, t=t),
        grid=(bsz,),
        in_specs=[pl.BlockSpec((1, LANES, t), lambda b: (b, 0, 0))],
        out_specs=[pl.BlockSpec((1, n_exp, t), lambda b: (b, 0, 0)),
                   pl.BlockSpec((1, t, LANES), lambda b: (b, 0, 0)),
                   pl.BlockSpec((1, n_exp, LANES), lambda b: (b, 0, 0))],
        out_shape=[jax.ShapeDtypeStruct((bsz, n_exp, t), I32),
                   jax.ShapeDtypeStruct((bsz, t, LANES), I32),
                   jax.ShapeDtypeStruct((bsz, n_exp, LANES), I32)],
        scratch_shapes=[pltpu.VMEM((n_exp, t), F32)] * 3,
        compiler_params=_cparams("arbitrary"),
        name="select",
    )(afft)


def _ffn_rows(cap_tot):
    for rc in range(min(cap_tot, 512), 7, -8):
        if cap_tot % rc == 0:
            return rc
    raise ValueError(cap_tot)


def _expert_kernel(cb_ref, pos_ref, a_ref, wg_ref, wu_ref, wd_ref, y_ref, x_ref, *, n_exp, nsub, sub_per, cap_tot):
    e, b, ch = pl.program_id(0), pl.program_id(1), pl.program_id(2)

    @pl.when(ch == 0)
    def _():
        x_ref[...] = jnp.zeros_like(x_ref)

    base = (b * n_exp + e) * (nsub + 1) + ch * sub_per
    row = lax.broadcasted_iota(I32, (WIN_X, SUBT), 0)

    def gather(j, carry):
        start = cb_ref[base + j]
        start_al = pl.multiple_of((start // 8) * 8, 8)
        rel = pos_ref[0, 0, j] - start_al
        onehot = jnp.where(row == rel, 1.0, 0.0).astype(BF16)
        tok0 = pl.multiple_of(j * SUBT, SUBT)
        x_ref[pl.ds(start_al, WIN_X), :] += _dot(onehot, a_ref[0, pl.ds(tok0, SUBT), :])
        return carry
    lax.fori_loop(0, sub_per, gather, 0, unroll=4)

    @pl.when(ch == pl.num_programs(2) - 1)
    def _():
        rc = _ffn_rows(cap_tot)
        wg = wg_ref[0, 0].astype(BF16)
        wu = wu_ref[0, 0].astype(BF16)
        wd = wd_ref[0, 0].astype(BF16)
        for r0 in range(0, cap_tot, rc):
            xc = x_ref[r0:r0 + rc, :].astype(BF16)
            hid = _silu(_dot(xc, wg)) * _dot(xc, wu)
            y_ref[0, 0, r0:r0 + rc, :] = _dot(hid.astype(BF16), wd).astype(y_ref.dtype)
        y_ref[0, 0, cap_tot:, :] = jnp.zeros((y_ref.shape[2] - cap_tot, y_ref.shape[-1]), y_ref.dtype)


def _experts(cb_flat, pos, a, wg, wu, wd, layer, cap_tot):
    bsz, n_exp, t = pos.shape
    d = a.shape[-1]
    ff = wg.shape[-1]
    nsub = t // SUBT
    ntiles = t // TILE
    chunk = TILE * max(m for m in range(1, 13) if ntiles % m == 0)
    sub_per = chunk // SUBT
    assert cap_tot % 16 == 0
    rows_x = cap_tot + WIN_X
    rows_y = cap_tot + WIN_Y
    pos5 = pos.reshape(bsz, n_exp, nsub, 1, SUBT)
    grid_spec = pltpu.PrefetchScalarGridSpec(
        num_scalar_prefetch=1,
        grid=(n_exp, bsz, t // chunk),
        in_specs=[pl.BlockSpec((1, 1, sub_per, 1, SUBT), lambda e, b, c, cb: (b, e, c, 0, 0)),
                  pl.BlockSpec((1, chunk, d), lambda e, b, c, cb: (b, c, 0)),
                  pl.BlockSpec((1, 1, d, ff), lambda e, b, c, cb: (layer, e, 0, 0)),
                  pl.BlockSpec((1, 1, d, ff), lambda e, b, c, cb: (layer, e, 0, 0)),
                  pl.BlockSpec((1, 1, ff, d), lambda e, b, c, cb: (layer, e, 0, 0))],
        out_specs=pl.BlockSpec((1, 1, rows_y, d), lambda e, b, c, cb: (b, e, 0, 0)),
        scratch_shapes=[pltpu.VMEM((rows_x, d), F32)],
    )
    return pl.pallas_call(
        functools.partial(_expert_kernel, n_exp=n_exp, nsub=nsub, sub_per=sub_per, cap_tot=cap_tot),
        grid_spec=grid_spec,
        out_shape=jax.ShapeDtypeStruct((bsz, n_exp, rows_y, d), BF16),
        compiler_params=_cparams("arbitrary", "arbitrary", "arbitrary"),
        name="experts",
    )(cb_flat, pos5, a, wg, wu, wd)


def _combine(cb_flat, h, mod, aff, post, y, g_final, nct, final_norm):
    bsz, t, d = h.shape
    n_exp = y.shape[1]
    ntiles = t // TILE
    first = nct if final_norm else 0
    n_win = n_exp * (TILE // SUBT)
    grid_spec = pltpu.PrefetchScalarGridSpec(
        num_scalar_prefetch=1,
        grid=(bsz, ntiles - first),
        in_specs=[pl.BlockSpec((1, TILE, d), lambda b, i, cb: (b, i + first, 0)),
                  pl.BlockSpec((1, 1, 6, d), lambda b, i, cb: (b, jnp.where(i + first < nct, 0, 1), 0, 0)),
                  pl.BlockSpec((1, TILE, LANES), lambda b, i, cb: (b, i + first, 0)),
                  pl.BlockSpec((1, TILE, LANES), lambda b, i, cb: (b, i + first, 0)),
                  pl.BlockSpec(memory_space=pl.ANY),
                  pl.BlockSpec((1, d), lambda b, i, cb: (0, 0))],
        out_specs=pl.BlockSpec((1, TILE, d), lambda b, i, cb: (b, i, 0)),
        scratch_shapes=[pltpu.VMEM((2, n_win, WIN_Y, d), y.dtype), pltpu.SemaphoreType.DMA((2, n_win))],
    )
    return pl.pallas_call(
        functools.partial(_combine_kernel, n_exp=n_exp, nsub=t // SUBT, first=first, final_norm=final_norm),
        grid_spec=grid_spec,
        out_shape=jax.ShapeDtypeStruct((bsz, t - first * TILE, d), F32),
        compiler_params=_cparams("arbitrary", "arbitrary"),
        name="combine",
    )(cb_flat, h, mod, aff, post, y, g_final.reshape(1, d))


def _combine_kernel(cb_ref, h_ref, mod_ref, aff_ref, post_ref, y_hbm, g_ref, o_ref, ybuf, sem,
                    *, n_exp, nsub, first, final_norm):
    n_i = pl.num_programs(1)
    step = pl.program_id(0) * n_i + pl.program_id(1)
    n_steps = pl.num_programs(0) * n_i
    per_tile = TILE // SUBT

    def window(at_step, sub, e):
        b = at_step // n_i
        i = at_step - b * n_i + first
        start = cb_ref[(b * n_exp + e) * (nsub + 1) + i * per_tile + sub]
        start_al = pl.multiple_of((start // 16) * 16, 16)
        buf, slot = at_step % 2, sub * n_exp + e
        copy = pltpu.make_async_copy(y_hbm.at[b, e, pl.ds(start_al, WIN_Y), :], ybuf.at[buf, slot],
                                     sem.at[buf, slot])
        return copy, start_al

    def start_windows(at_step):
        for sub in range(per_tile):
            for e in range(n_exp):
                window(at_step, sub, e)[0].start()

    @pl.when(step == 0)
    def _():
        start_windows(step)

    @pl.when(step + 1 < n_steps)
    def _():
        start_windows(step + 1)

    lane = lax.broadcasted_iota(I32, (SUBT, WIN_Y), 1)
    mod = mod_ref[0, 0]
    for sub in range(per_tile):
        rows = slice(sub * SUBT, (sub + 1) * SUBT)
        post = post_ref[0, rows, :]
        aff = aff_ref[0, rows, :]
        acc = jnp.zeros((SUBT, o_ref.shape[-1]), F32)
        starts = []
        for e in range(n_exp):
            copy, start_al = window(step, sub, e)
            copy.wait()
            starts.append(start_al)
        for e, start_al in enumerate(starts):
            rel = post[:, e:e + 1] - start_al
            onehot = jnp.where(lane == rel, 1.0, 0.0).astype(BF16)
            acc = acc + aff[:, e:e + 1] * _dot(onehot, ybuf[step % 2, sub * n_exp + e])
        out = h_ref[0, rows, :] + mod[5:6] * acc
        if final_norm:
            ms = jnp.mean(out * out, axis=-1, keepdims=True)
            out = out * lax.rsqrt(ms + EPS) * g_ref[...]
        o_ref[0, rows, :] = out


def _moe(h, mod, ln2_g, w_router, wg, wu, wd, layer, g_final, *, n_ctx, route_ctx, final_norm):
    bsz, t, d = h.shape
    n_exp = w_router.shape[1]
    nct = n_ctx // TILE
    n_lat = t - n_ctx
    k_lat = max(1, EC_CAPACITY * n_lat // n_exp)
    k_ctx = max(1, EC_CAPACITY * n_ctx // n_exp) if route_ctx else 0
    a, aff, afft = _router(h, mod, ln2_g, w_router, nct)
    pos, post, cb = _select(afft, n_exp, n_ctx, k_ctx, k_lat)
    cb_flat = cb[:, :, :t // SUBT + 1].reshape(-1)
    y = _experts(cb_flat, pos, a, wg, wu, wd, layer, k_ctx + k_lat)
    return _combine(cb_flat, h, mod, aff, post, y, g_final, nct, final_norm)


HEAD_SHIFT = int(math.log2(HEAD_DIM))


def _same_head(rows, cols):
    r = lax.shift_right_logical(lax.broadcasted_iota(I32, (rows, cols), 0), HEAD_SHIFT)
    c = lax.shift_right_logical(lax.broadcasted_iota(I32, (rows, cols), 1), HEAD_SHIFT)
    return jnp.where(r == c, 1.0, 0.0)


def _head_sum(x2):
    n = x2.shape[-1]
    return _dot_exact_rhs(x2, _same_head(n, n).astype(BF16))


def _outproj_even_kernel(*refs, n_h, nct, a_dim):
    mod_ref, of_ref, ob_ref, z_ref, at_ref, g_ref, w_ref, o_ref = refs[n_h:]
    mod = mod_ref[0, 0]
    o = of_ref[0] + ob_ref[0]
    ms = _head_sum(o * o) * (1.0 / HEAD_DIM)
    ya = o * lax.rsqrt(ms + EPS) * g_ref[...] * _silu(z_ref[0])
    y = _dot(ya.astype(BF16), w_ref[:a_dim, :]) + _dot(at_ref[0], w_ref[a_dim:, :])
    o_ref[0] = _hidden_tile(refs[:n_h], nct) + mod[2:3] * y


def _outproj_even(h, mod, o_delta, p, o_attn, out_norm_g, w_bf16, nct, *, a_dim):
    h_ops, h_specs, t = _hidden_operands(h, nct)
    bsz, _, d = h_ops[0].shape
    o_f, o_b = o_delta
    g = jnp.tile(out_norm_g, a_dim // HEAD_DIM).reshape(1, a_dim)
    tok = lambda b, i: (b, i, 0)
    return pl.pallas_call(
        functools.partial(_outproj_even_kernel, n_h=len(h_ops), nct=nct, a_dim=a_dim),
        grid=(bsz, t // TILE),
        in_specs=h_specs + [pl.BlockSpec((1, 1, 6, d), lambda b, i: (b, jnp.where(i < nct, 0, 1), 0, 0)),
                            pl.BlockSpec((1, TILE, a_dim), tok),
                            pl.BlockSpec((1, TILE, a_dim), tok),
                            pl.BlockSpec((1, TILE, a_dim), lambda b, i: (b, i, 3)),
                            pl.BlockSpec((1, TILE, a_dim), tok),
                            pl.BlockSpec((1, a_dim), lambda b, i: (0, 0)),
                            pl.BlockSpec((d, d), lambda b, i: (0, 0))],
        out_specs=pl.BlockSpec((1, TILE, d), tok),
        out_shape=jax.ShapeDtypeStruct((bsz, t, d), F32),
        compiler_params=_cparams("arbitrary", "arbitrary"),
        name="outproj_even",
    )(*h_ops, mod, o_f, o_b, p, o_attn, g, w_bf16)


def _outproj_odd_kernel(h_ref, mod_ref, a_ref, w_ref, o_ref):
    mod = mod_ref[0, 0]
    o_ref[0] = h_ref[0] + mod[2:3] * _dot(a_ref[0], w_ref[...])


def _outproj_odd(h, mod, o_na, w_bf16, nct):
    bsz, t, d = h.shape
    n_lat = o_na.shape[1]
    return pl.pallas_call(
        _outproj_odd_kernel,
        grid=(bsz, n_lat // TILE),
        in_specs=[pl.BlockSpec((1, TILE, d), lambda b, i: (b, i + nct, 0)),
                  pl.BlockSpec((1, 1, 6, d), lambda b, i: (b, 1, 0, 0)),
                  pl.BlockSpec((1, TILE, d), lambda b, i: (b, i, 0)),
                  pl.BlockSpec((d, d), lambda b, i: (0, 0))],
        out_specs=pl.BlockSpec((1, TILE, d), lambda b, i: (b, i, 0)),
        out_shape=jax.ShapeDtypeStruct((bsz, n_lat, d), F32),
        compiler_params=_cparams("arbitrary", "arbitrary"),
        name="outproj_odd",
    )(h, mod, o_na, w_bf16)


def _rope_tables(n_ctx, n_lat):
    tt = jnp.arange(n_lat)
    n_freq = HEAD_DIM // 4
    inv = ROPE_THETA ** (-jnp.arange(n_freq, dtype=F32) / n_freq)
    ang = jnp.concatenate([(tt // GRID_W).astype(F32)[:, None] * inv, (tt % GRID_W).astype(F32)[:, None] * inv], -1)
    cos = jnp.concatenate([jnp.ones((n_ctx, 2 * n_freq), F32), jnp.cos(ang)], axis=0)
    sin = jnp.concatenate([jnp.zeros((n_ctx, 2 * n_freq), F32), jnp.sin(ang)], axis=0)
    reps = LANES // HEAD_DIM
    return jnp.tile(cos, (1, 2 * reps)), jnp.tile(jnp.concatenate([-sin, sin], axis=1), (1, reps))


def _rope(x, cos, sin):
    n = x.shape[-1]
    half = HEAD_DIM // 2
    lane = lax.broadcasted_iota(I32, x.shape, 1)
    first = (lane & (HEAD_DIM - 1)) < half
    partner = jnp.where(first, pltpu.roll(x, n - half, 1), pltpu.roll(x, half, 1))
    reps = n // LANES
    cos_n = jnp.concatenate([cos] * reps, axis=1) if reps > 1 else cos
    sin_n = jnp.concatenate([sin] * reps, axis=1) if reps > 1 else sin
    return x * cos_n + partner * sin_n


def _gqa_prep_kernel(q_ref, k_ref, v_ref, cos_ref, sin_ref, gq_ref, gk_ref, qo_ref, ko_ref, vo_ref):
    cos, sin = cos_ref[...], sin_ref[...]
    q = q_ref[0]
    q = q * lax.rsqrt(_head_sum(q * q) * (1.0 / HEAD_DIM) + EPS) * gq_ref[...] * (HEAD_DIM ** -0.5 * LOG2_E)
    qo_ref[0] = _rope(q, cos, sin).astype(BF16)
    k = k_ref[0]
    k = k * lax.rsqrt(_head_sum(k * k) * (1.0 / HEAD_DIM) + EPS) * gk_ref[...]
    k = _rope(k, cos, sin).astype(BF16)
    v = v_ref[0].astype(BF16)
    for hh in range(k.shape[-1] // HEAD_DIM):
        ko_ref[0, hh] = k[:, hh * HEAD_DIM:(hh + 1) * HEAD_DIM]
        vo_ref[0, hh] = v[:, hh * HEAD_DIM:(hh + 1) * HEAD_DIM]


def _gqa_prep(p, cos, sin, q_norm_g, k_norm_g, *, a_dim, kv_dim):
    bsz, t, _ = p.shape
    q_dim = a_dim
    kvh = kv_dim // HEAD_DIM
    assert kv_dim == LANES
    c_q = (4 * a_dim) // q_dim
    c_k = (4 * a_dim + q_dim) // kv_dim
    gq = jnp.tile(q_norm_g, q_dim // HEAD_DIM).reshape(1, q_dim)
    gk = jnp.tile(k_norm_g, kvh).reshape(1, kv_dim)
    return pl.pallas_call(
        _gqa_prep_kernel,
        grid=(bsz, t // TILE),
        in_specs=[pl.BlockSpec((1, TILE, q_dim), lambda b, i: (b, i, c_q)),
                  pl.BlockSpec((1, TILE, kv_dim), lambda b, i: (b, i, c_k)),
                  pl.BlockSpec((1, TILE, kv_dim), lambda b, i: (b, i, c_k + 1)),
                  pl.BlockSpec((TILE, LANES), lambda b, i: (i, 0)),
                  pl.BlockSpec((TILE, LANES), lambda b, i: (i, 0)),
                  pl.BlockSpec((1, q_dim), lambda b, i: (0, 0)),
                  pl.BlockSpec((1, kv_dim), lambda b, i: (0, 0))],
        out_specs=[pl.BlockSpec((1, TILE, q_dim), lambda b, i: (b, i, 0)),
                   pl.BlockSpec((1, kvh, TILE, HEAD_DIM), lambda b, i: (b, 0, i, 0)),
                   pl.BlockSpec((1, kvh, TILE, HEAD_DIM), lambda b, i: (b, 0, i, 0))],
        out_shape=[jax.ShapeDtypeStruct((bsz, t, q_dim), BF16),
                   jax.ShapeDtypeStruct((bsz, kvh, t, HEAD_DIM), BF16),
                   jax.ShapeDtypeStruct((bsz, kvh, t, HEAD_DIM), BF16)],
        compiler_params=_cparams("arbitrary", "arbitrary"),
        name="gqa_prep",
    )(p, p, p, cos, sin, gq, gk)


SUB_K = 256
Q_SLAB = 256


def _gqa_kernel(q_ref, k_ref, v_ref, o_ref, qs_ref, m_ref, l_ref, acc_ref, *, n_ctx, nct, tq, tk, group):
    i, kv = pl.program_id(2), pl.program_id(3)

    @pl.when(kv == 0)
    def _():
        m_ref[...] = jnp.full_like(m_ref, -jnp.inf)
        l_ref[...] = jnp.zeros_like(l_ref)
        acc_ref[...] = jnp.zeros_like(acc_ref)
        q = q_ref[0]
        for g in range(group):
            qs_ref[g * tq:(g + 1) * tq, :] = q[:, g * HEAD_DIM:(g + 1) * HEAD_DIM]

    def attend(keys, vals, n_valid):
        starts = list(range(0, group * tq, Q_SLAB))
        scores = {starts[0]: _dot_t(qs_ref[0:Q_SLAB, :], keys)}
        for idx, r0 in enumerate(starts):
            rows = slice(r0, r0 + Q_SLAB)
            if idx + 1 < len(starts):
                nxt = starts[idx + 1]
                scores[nxt] = _dot_t(qs_ref[nxt:nxt + Q_SLAB, :], keys)
            s = scores.pop(r0)
            if n_valid is not None:
                s = jnp.where(lax.broadcasted_iota(I32, s.shape, 1) < n_valid, s, -jnp.inf)
            m_old = m_ref[rows, :]
            m_new = jnp.maximum(m_old, jnp.max(s, axis=-1, keepdims=True))
            alpha = jnp.exp2(m_old - m_new)
            p = jnp.exp2(s - m_new)
            l_ref[rows, :] = alpha * l_ref[rows, :] + jnp.sum(p, axis=-1, keepdims=True)
            acc_ref[rows, :] = alpha * acc_ref[rows, :] + _dot(p.astype(BF16), vals)
            m_ref[rows, :] = m_new

    @pl.when(i >= nct)
    def _():
        attend(k_ref[0, 0], v_ref[0, 0], None)

    @pl.when((i < nct) & (kv == 0))
    def _():
        n_keys = -(-n_ctx // SUB_K) * SUB_K
        attend(k_ref[0, 0, :n_keys, :], v_ref[0, 0, :n_keys, :], n_ctx if n_keys != n_ctx else None)

    @pl.when(kv == pl.num_programs(3) - 1)
    def _():
        out = acc_ref[...] / l_ref[...]
        for g in range(group):
            o_ref[0, :, g * HEAD_DIM:(g + 1) * HEAD_DIM] = out[g * tq:(g + 1) * tq, :].astype(o_ref.dtype)


def _gqa_attention(q, k, v, *, n_ctx):
    bsz, t, q_dim = q.shape
    kvh = k.shape[1]
    group = q_dim // HEAD_DIM // kvh
    tq = TILE
    nt = t // TILE
    tk = TILE * max(m for m in range(1, 13) if nt % m == 0)
    assert n_ctx <= tk and tk % SUB_K == 0
    return pl.pallas_call(
        functools.partial(_gqa_kernel, n_ctx=n_ctx, nct=n_ctx // tq, tq=tq, tk=tk, group=group),
        grid=(bsz, kvh, t // tq, t // tk),
        in_specs=[pl.BlockSpec((1, tq, group * HEAD_DIM), lambda b, h, i, j: (b, i, h)),
                  pl.BlockSpec((1, 1, tk, HEAD_DIM), lambda b, h, i, j: (b, h, j, 0)),
                  pl.BlockSpec((1, 1, tk, HEAD_DIM), lambda b, h, i, j: (b, h, j, 0))],
        out_specs=pl.BlockSpec((1, tq, group * HEAD_DIM), lambda b, h, i, j: (b, i, h)),
        out_shape=jax.ShapeDtypeStruct((bsz, t, q_dim), BF16),
        scratch_shapes=[pltpu.VMEM((group * tq, HEAD_DIM), BF16),
                        pltpu.VMEM((group * tq, 1), F32),
                        pltpu.VMEM((group * tq, 1), F32),
                        pltpu.VMEM((group * tq, HEAD_DIM), F32)],
        compiler_params=_cparams("arbitrary", "arbitrary", "arbitrary", "arbitrary"),
        name="gqa_attention",
    )(q, k, v)


def _natten_bias(rpb):
    n_heads, n_off, n_rel = rpb.shape
    col = np.arange(GRID_W)
    col_start = np.clip(col - NA_COLS // 2, 0, GRID_W - NA_COLS)
    valid = (col[None, :] >= col_start[:, None]) & (col[None, :] < col_start[:, None] + NA_COLS)
    rel = col[None, :] - col[:, None] + NA_COLS - 1
    onehot = ((rel[None] == np.arange(n_rel)[:, None, None]) & valid[None]).astype(np.float32)
    tab = jnp.einsum('hrk,kcd->hrcd', rpb, jnp.asarray(onehot), precision=lax.Precision.HIGHEST)
    tab = jnp.where(valid[None, None], tab, -jnp.inf)
    pairs = jnp.concatenate([tab[:, :-1], tab[:, 1:]], axis=-1)
    return pairs.reshape(n_heads // 2, 2, n_off - 1, GRID_W, 2 * GRID_W).astype(F32)


def _natten_kernel(*refs, n_ctx, n_rows, rblk, n_q):
    q_refs = refs[:n_q]
    k_ref, v_ref, bias_ref, o_ref = refs[n_q:]
    per_q = TILE // GRID_W
    rb = pl.program_id(2)
    first_head = lax.broadcasted_iota(I32, (1, LANES), 1) < HEAD_DIM
    kc = k_ref[0, 0:n_ctx, :]
    vc = v_ref[0, 0:n_ctx, :]
    band = NA_ROWS * GRID_W
    qs, ks, vs, biases = [], [], [], []
    for rr in range(rblk):
        rg = rb * rblk + rr
        rs = jnp.clip(rg - NA_ROWS // 2, 0, n_rows - NA_ROWS)
        variant = rs - rg + (NA_ROWS - 1)
        start = pl.multiple_of(n_ctx + rs * GRID_W, GRID_W)
        keys = jnp.concatenate([kc, k_ref[0, pl.ds(start, band), :]], axis=0)
        vals = jnp.concatenate([vc, v_ref[0, pl.ds(start, band), :]], axis=0)
        q_lo = (rr % per_q) * GRID_W
        q = q_refs[rr // per_q][0, q_lo:q_lo + GRID_W, :] * (HEAD_DIM ** -0.5)
        for hh in range(2):
            qs.append(jnp.where(first_head if hh == 0 else jnp.logical_not(first_head), q, jnp.zeros_like(q)))
            ks.append(keys)
            vs.append(vals)
            biases.append(jnp.concatenate([bias_ref[0, hh, variant + j] for j in range(0, NA_ROWS, 2)], axis=1))
    batch = (((2,), (2,)), ((0,), (0,)))
    s = lax.dot_general(jnp.stack(qs), jnp.stack(ks), batch, preferred_element_type=F32)
    sc = s[:, :, :n_ctx]
    sb = s[:, :, n_ctx:] + jnp.stack(biases)
    m = jnp.maximum(jnp.max(sc, axis=-1, keepdims=True), jnp.max(sb, axis=-1, keepdims=True))
    pc = jnp.exp(sc - m)
    pb = jnp.exp(sb - m)
    l = jnp.sum(pc, axis=-1, keepdims=True) + jnp.sum(pb, axis=-1, keepdims=True)
    p = jnp.concatenate([pc, pb], axis=-1).astype(BF16)
    o = lax.dot_general(p, jnp.stack(vs), (((2,), (1,)), ((0,), (0,))), preferred_element_type=F32) / l
    for rr in range(rblk):
        o_ref[0, rr * GRID_W:(rr + 1) * GRID_W, :] = jnp.where(first_head, o[2 * rr], o[2 * rr + 1]).astype(o_ref.dtype)


def _natten(p, bias, *, n_ctx, d):
    bsz, t, _ = p.shape
    n_lat = t - n_ctx
    n_rows = n_lat // GRID_W
    n_q = 2 if n_lat % (2 * TILE) == 0 else 1
    rblk = n_q * TILE // GRID_W
    npairs = d // LANES
    nct = n_ctx // TILE
    assert n_rows >= NA_ROWS and n_ctx % TILE == 0
    return pl.pallas_call(
        functools.partial(_natten_kernel, n_ctx=n_ctx, n_rows=n_rows, rblk=rblk, n_q=n_q),
        grid=(bsz, npairs, n_lat // (n_q * TILE)),
        in_specs=[pl.BlockSpec((1, TILE, LANES), lambda b, hp, r, j=j: (b, nct + n_q * r + j, hp)) for j in range(n_q)] + [
                  pl.BlockSpec((1, t, LANES), lambda b, hp, r: (b, 0, npairs + hp)),
                  pl.BlockSpec((1, t, LANES), lambda b, hp, r: (b, 0, 2 * npairs + hp)),
                  pl.BlockSpec((1, 2, 2 * NA_ROWS - 2, GRID_W, 2 * GRID_W), lambda b, hp, r: (hp, 0, 0, 0, 0))],
        out_specs=pl.BlockSpec((1, n_q * TILE, LANES), lambda b, hp, r: (b, r, hp)),
        out_shape=jax.ShapeDtypeStruct((bsz, n_lat, d), BF16),
        compiler_params=_cparams("arbitrary", "arbitrary", "arbitrary"),
        name="natten",
    )(*([p] * n_q), p, p, bias)


def _dot_exact_lhs(a_bf16, b):
    b1, b2, b3 = _split3(b)
    return _dot(a_bf16, b1) + _dot(a_bf16, b2) + _dot(a_bf16, b3)


def _delta_prep_kernel(x_ref, prev_ref, next_ref, gate_ref, w_ref, alog_ref, dt_ref, xh_ref, gn_ref, gnt_ref, xs_ref,
                       *, nct, a_dim):
    i = pl.program_id(1)
    last = pl.num_programs(1) - 1
    x = x_ref[0]
    rows = x.shape[0]
    xs_ref[0:8, :] = prev_ref[0]
    xs_ref[8:8 + rows, :] = x
    xs_ref[8 + rows:16 + rows, :] = next_ref[0]
    r = lax.broadcasted_iota(I32, (rows, 1), 0)
    at_start = (i == 0) | (i == nct)
    at_end = (i == nct - 1) | (i == last)
    xm = jnp.where((r == 0) & at_start, 0.0, xs_ref[7:7 + rows, :])
    xp = jnp.where((r == rows - 1) & at_end, 0.0, xs_ref[9:9 + rows, :])
    w = w_ref[...]
    y = _silu(w[0:1] * xm + w[1:2] * x + w[2:3] * xp)
    q, k, v = y[:, :a_dim], y[:, a_dim:2 * a_dim], y[:, 2 * a_dim:]
    nh = a_dim // HEAD_DIM
    q = (q * lax.rsqrt(_head_sum(q * q) + EPS) * (HEAD_DIM ** -0.5)).astype(BF16)
    k = (k * lax.rsqrt(_head_sum(k * k) + EPS)).astype(BF16)
    v = v.astype(BF16)
    for kind, arr in enumerate((q, k, v)):
        for hh in range(nh):
            xh_ref[0, kind, hh] = arr[:, hh * HEAD_DIM:(hh + 1) * HEAD_DIM]

    g = gate_ref[0]
    z = g + dt_ref[...]
    softplus = jnp.maximum(z, 0.0) + jnp.log1p(jnp.exp(-jnp.abs(z)))
    log_a = -jnp.exp(alog_ref[...]) * softplus
    beta = _sigmoid(g)
    ri = lax.broadcasted_iota(I32, (rows, rows), 0)
    ci = lax.broadcasted_iota(I32, (rows, rows), 1)
    chunk_shift = int(math.log2(DELTA_CHUNK))
    same_chunk = lax.shift_right_logical(ri, chunk_shift) == lax.shift_right_logical(ci, chunk_shift)
    prefix = jnp.where(same_chunk & (ci <= ri), 1.0, 0.0).astype(BF16)
    suffix = jnp.where(same_chunk & (ci >= ri), 1.0, 0.0).astype(BF16)
    g_f = _dot_exact_lhs(prefix, log_a)
    g_b = _dot_exact_lhs(suffix, log_a)
    lane = lax.broadcasted_iota(I32, g.shape, 1)
    narrow = jnp.where(lane < nh, g_f, jnp.where(lane < 2 * nh, beta, jnp.where(lane < 3 * nh, g_b, beta)))
    gn_ref[0] = narrow
    gnt_ref[0] = narrow.T


def _delta_prep(p, conv_w, a_log, dt_bias, *, n_ctx, a_dim):
    bsz, t, _ = p.shape
    nh = a_dim // HEAD_DIM
    assert DELTA_CHUNK == HEAD_DIM and TILE % DELTA_CHUNK == 0 and 4 * nh <= LANES
    nct = n_ctx // TILE
    c_gate = (4 * a_dim + a_dim + 2 * LANES) // LANES
    alog = jnp.zeros((1, LANES), F32).at[0, :nh].set(a_log[0]).at[0, 2 * nh:3 * nh].set(a_log[1])
    dtb = jnp.zeros((1, LANES), F32).at[0, :nh].set(dt_bias[0]).at[0, 2 * nh:3 * nh].set(dt_bias[1])
    nblk8 = t // 8
    per = TILE // 8
    return pl.pallas_call(
        functools.partial(_delta_prep_kernel, nct=nct, a_dim=a_dim),
        grid=(bsz, t // TILE),
        in_specs=[pl.BlockSpec((1, TILE, 3 * a_dim), lambda b, i: (b, i, 0)),
                  pl.BlockSpec((1, 8, 3 * a_dim), lambda b, i: (b, jnp.maximum(i * per - 1, 0), 0)),
                  pl.BlockSpec((1, 8, 3 * a_dim), lambda b, i: (b, jnp.minimum((i + 1) * per, nblk8 - 1), 0)),
                  pl.BlockSpec((1, TILE, LANES), lambda b, i: (b, i, c_gate)),
                  pl.BlockSpec((3, 3 * a_dim), lambda b, i: (0, 0)),
                  pl.BlockSpec((1, LANES), lambda b, i: (0, 0)),
                  pl.BlockSpec((1, LANES), lambda b, i: (0, 0))],
        out_specs=[pl.BlockSpec((1, 3, nh, TILE, HEAD_DIM), lambda b, i: (b, 0, 0, i, 0)),
                   pl.BlockSpec((1, TILE, LANES), lambda b, i: (b, i, 0)),
                   pl.BlockSpec((1, LANES, TILE), lambda b, i: (b, 0, i))],
        out_shape=[jax.ShapeDtypeStruct((bsz, 3, nh, t, HEAD_DIM), BF16),
                   jax.ShapeDtypeStruct((bsz, t, LANES), F32),
                   jax.ShapeDtypeStruct((bsz, LANES, t), F32)],
        scratch_shapes=[pltpu.VMEM((TILE + 16, 3 * a_dim), F32)],
        compiler_params=_cparams("arbitrary", "arbitrary"),
        name="delta_prep",
    )(p, p, p, p, conv_w, alog, dtb)


def _bdot(a, b, ca, cb, hi=False):
    dims = (((ca,), (cb,)), ((0,), (0,)))
    dot = lambda x, y: lax.dot_general(x, y, dims, preferred_element_type=F32)
    if not hi:
        return dot(a.astype(BF16), b.astype(BF16))
    a1 = a.astype(BF16)
    a2 = (a - a1.astype(F32)).astype(BF16)
    b1 = b.astype(BF16)
    b2 = (b - b1.astype(F32)).astype(BF16)
    return dot(a1, b1) + dot(a1, b2) + dot(a2, b1)


def _delta_heads(q, k, v, g_col, beta_col, g_row, s, upper):
    nb, lc, _ = q.shape
    ri = lax.broadcasted_iota(I32, (nb, lc, lc), 1)
    ci = lax.broadcasted_iota(I32, (nb, lc, lc), 2)
    ahead = jnp.where(upper, ci - ri, ri - ci)
    tri_incl = ahead >= 0
    tri_strict = ahead > 0
    eye = jnp.where(ri == ci, 1.0, 0.0)
    k32, q32, v32 = k.astype(F32), q.astype(F32), v.astype(F32)
    eg = jnp.exp(g_col)
    g_last = jnp.where(upper, g_col[:, 0:1], g_col[:, lc - 1:lc])
    gam = jnp.exp(jnp.where(tri_incl, g_col - g_row, -jnp.inf))
    kq = _bdot(jnp.concatenate([k, q], axis=1), k, 2, 2)
    m = jnp.where(tri_strict, kq[:, :lc] * beta_col * gam, 0.0)
    a_qk = kq[:, lc:] * gam
    n = -m
    x = eye + n
    p = _bdot(n, n, 2, 1, hi=True)
    for _ in range(int(math.log2(lc)) - 2):
        px_pp = _bdot(p, jnp.concatenate([x, p], axis=2), 2, 1, hi=True)
        x, p = x + px_pp[:, :, :lc], px_pp[:, :, lc:]
    x = x + _bdot(p, x, 2, 1, hi=True)
    uw = _bdot(x, jnp.concatenate([v32 * beta_col, k32 * (beta_col * eg)], axis=2), 2, 1, hi=True)
    u, w = uw[:, :, :v.shape[2]], uw[:, :, v.shape[2]:]
    ws_qs = _bdot(jnp.concatenate([w, q32 * eg], axis=1), s, 2, 1)
    v_new = u - ws_qs[:, :lc]
    o = ws_qs[:, lc:] + _bdot(a_qk, v_new, 2, 1)
    kv = _bdot(k32 * jnp.exp(g_last - g_col), v_new, 1, 1)
    return o, s * jnp.exp(g_last) + kv


def _delta_scan_kernel(xf_ref, xb_ref, gf_ref, gb_ref, gtf_ref, gtb_ref, of_ref, ob_ref, s_ref, *, nh):
    c = pl.program_id(1)

    @pl.when(c == 0)
    def _():
        s_ref[...] = jnp.zeros_like(s_ref)

    both = lambda f, b: jnp.concatenate([f, b], axis=0)
    q, k, v = (both(xf_ref[0, i], xb_ref[0, i]) for i in range(3))
    g_cols, b_cols, g_rows = [], [], []
    for d_i, (g_ref, gt_ref) in enumerate(((gf_ref, gtf_ref), (gb_ref, gtb_ref))):
        gn = g_ref[0]
        gt = gt_ref[0, 0]
        for hh in range(nh):
            lane_g = 2 * d_i * nh + hh
            g_cols.append(gn[:, lane_g:lane_g + 1])
            b_cols.append(gn[:, lane_g + nh:lane_g + nh + 1])
            g_rows.append(gt[lane_g:lane_g + 1, :])
    upper = lax.broadcasted_iota(I32, (2 * nh, 1, 1), 0) >= nh
    o, s_new = _delta_heads(q, k, v, jnp.stack(g_cols), jnp.stack(b_cols), jnp.stack(g_rows), s_ref[...], upper)
    s_ref[...] = s_new
    for hh in range(nh):
        of_ref[0, :, hh * HEAD_DIM:(hh + 1) * HEAD_DIM] = o[hh]
        ob_ref[0, :, hh * HEAD_DIM:(hh + 1) * HEAD_DIM] = o[nh + hh]


def _delta_scan(xh, gn, gnt, *, n_ctx):
    bsz, _, nh, t, _ = xh.shape
    lc = DELTA_CHUNK
    nc, ncc = t // lc, n_ctx // lc
    gnt4 = gnt.reshape(bsz, LANES, nc, lc).transpose(0, 2, 1, 3)
    rev = lambda c: jnp.where(c < ncc, ncc - 1 - c, nc - 1 - (c - ncc))
    return pl.pallas_call(
        functools.partial(_delta_scan_kernel, nh=nh),
        grid=(bsz, nc),
        in_specs=[pl.BlockSpec((1, 3, nh, lc, HEAD_DIM), lambda b, c: (b, 0, 0, c, 0)),
                  pl.BlockSpec((1, 3, nh, lc, HEAD_DIM), lambda b, c: (b, 0, 0, rev(c), 0)),
                  pl.BlockSpec((1, lc, LANES), lambda b, c: (b, c, 0)),
                  pl.BlockSpec((1, lc, LANES), lambda b, c: (b, rev(c), 0)),
                  pl.BlockSpec((1, 1, LANES, lc), lambda b, c: (b, c, 0, 0)),
                  pl.BlockSpec((1, 1, LANES, lc), lambda b, c: (b, rev(c), 0, 0))],
        out_specs=[pl.BlockSpec((1, lc, nh * HEAD_DIM), lambda b, c: (b, c, 0)),
                   pl.BlockSpec((1, lc, nh * HEAD_DIM), lambda b, c: (b, rev(c), 0))],
        out_shape=[jax.ShapeDtypeStruct((bsz, t, nh * HEAD_DIM), F32),
                   jax.ShapeDtypeStruct((bsz, t, nh * HEAD_DIM), F32)],
        scratch_shapes=[pltpu.VMEM((2 * nh, HEAD_DIM, HEAD_DIM), F32)],
        compiler_params=_cparams("arbitrary", "arbitrary"),
        name="delta_scan",
    )(xh, xh, gn, gn, gnt4, gnt4)


def kernel(x, c, ctx, c_ctx, ada_w, ada_b, ln1_g, ln2_g, even_w_in, even_conv_w, even_a_log, even_dt_bias, even_out_norm_g, even_q_norm_g, even_k_norm_g, even_w_out, odd_w_in, odd_rpb, odd_w_out, router_w, expert_w_gate, expert_w_up, expert_w_down, final_norm_g):
    bsz, n_lat, d = x.shape
    n_ctx = ctx.shape[1]
    t = n_ctx + n_lat
    nct = n_ctx // TILE
    h = (ctx, x)
    rows = jnp.zeros((8, d), F32).at[0].set(c_ctx).at[1:1 + bsz].set(c)

    def mod_of(layer):
        m = _adaln(rows, ada_w, ada_b, layer).reshape(8, 6, d)
        return jnp.stack([jnp.broadcast_to(m[0], (bsz, 6, d)), m[1:1 + bsz]], axis=1)

    mod = mod_of(0)
    a_dim = d // 2
    a_heads = a_dim // HEAD_DIM
    w_in = even_w_in[0]
    o_qkv, o_z, o_g = 0, 3 * a_dim, 4 * a_dim
    o_bq = o_g + 4 * a_heads
    kv_dim = (w_in.shape[1] - o_bq - a_dim) // 2
    o_bk, o_bv = o_bq + a_dim, o_bq + a_dim + kv_dim
    gates_w = jnp.zeros((d, LANES), F32).at[:, :4 * a_heads].set(w_in[:, o_g:o_bq])
    w_perm = jnp.concatenate([w_in[:, :o_g], w_in[:, o_bq:], gates_w], axis=1).astype(BF16)
    p = _modproj(h, mod, ln1_g[0], w_perm, nct, F32)
    xh, gn, gnt = _delta_prep(p, even_conv_w[0], even_a_log[0], even_dt_bias[0], n_ctx=n_ctx, a_dim=a_dim)
    o_delta = _delta_scan(xh, gn, gnt, n_ctx=n_ctx)
    cos, sin = _rope_tables(n_ctx, n_lat)
    qb, kb, vb = _gqa_prep(p, cos, sin, even_q_norm_g[0], even_k_norm_g[0], a_dim=a_dim, kv_dim=kv_dim)
    o_attn = _gqa_attention(qb, kb, vb, n_ctx=n_ctx)
    h = _outproj_even(h, mod, o_delta, p, o_attn, even_out_norm_g[0], even_w_out[0].astype(BF16), nct, a_dim=a_dim)
    h = _moe(h, mod, ln2_g[0], router_w[0], expert_w_gate, expert_w_up, expert_w_down, 0, final_norm_g,
             n_ctx=n_ctx, route_ctx=True, final_norm=False)

    mod = mod_of(1)
    p = _modproj(h, mod, ln1_g[1], odd_w_in[0].astype(BF16), nct, BF16)
    o_na = _natten(p, _natten_bias(odd_rpb[0]), n_ctx=n_ctx, d=d)
    h_lat = _outproj_odd(h, mod, o_na, odd_w_out[0].astype(BF16), nct)
    return _moe(h_lat, mod, ln2_g[1], router_w[1], expert_w_gate, expert_w_up, expert_w_down, 1, final_norm_g,
                n_ctx=0, route_ctx=False, final_norm=True)
```

```python
import functools
import math

import numpy as np
import jax
import jax.numpy as jnp
from jax import lax
from jax.experimental import pallas as pl
from jax.experimental.pallas import tpu as pltpu

F32, BF16, I32 = jnp.float32, jnp.bfloat16, jnp.int32

HEAD_DIM = 64
GRID_W = 64
DELTA_CHUNK = 64
NA_ROWS = 8
NA_COLS = 16
ROPE_THETA = 10000.0
EC_CAPACITY = 2
EPS = 1e-6
LOG2_E = math.log2(math.e)

LANES = 128
TILE = 256
SUBT = 128
WIN_X = SUBT + 8
WIN_Y = SUBT + 16
VMEM_LIMIT = 56 * 1024 * 1024


def _cparams(*sem):
    return pltpu.CompilerParams(dimension_semantics=sem, vmem_limit_bytes=VMEM_LIMIT)


def _split3(a):
    a1 = a.astype(BF16)
    r1 = a - a1.astype(F32)
    a2 = r1.astype(BF16)
    a3 = (r1 - a2.astype(F32)).astype(BF16)
    return a1, a2, a3


def _dot(a, b):
    return jnp.dot(a, b, preferred_element_type=F32)


def _dot_t(a, b):
    return lax.dot_general(a, b, (((1,), (1,)), ((), ())), preferred_element_type=F32)


def _dot_exact_rhs(a, b_bf16):
    a1, a2, a3 = _split3(a)
    return _dot(a1, b_bf16) + _dot(a2, b_bf16) + _dot(a3, b_bf16)


def _dot_hi(a, b):
    a1 = a.astype(BF16)
    a2 = (a - a1.astype(F32)).astype(BF16)
    b1 = b.astype(BF16)
    b2 = (b - b1.astype(F32)).astype(BF16)
    return _dot(a1, b1) + _dot(a1, b2) + _dot(a2, b1)


def _sigmoid(x):
    return 1.0 / (1.0 + jnp.exp(-x))


def _silu(x):
    return x * _sigmoid(x)


def _modulate(x, g, shift, scale):
    ms = jnp.mean(x * x, axis=-1, keepdims=True)
    return x * lax.rsqrt(ms + EPS) * g * (1.0 + scale) + shift


def _adaln_kernel(s_ref, w_ref, b_ref, o_ref):
    s = _silu(s_ref[...])
    o_ref[...] = _dot_hi(s, w_ref[0]) + b_ref[0]


def _adaln(rows, w, b, layer):
    depth, d, n = w.shape
    tn = n // 4
    return pl.pallas_call(
        _adaln_kernel,
        grid=(n // tn,),
        in_specs=[pl.BlockSpec((8, d), lambda j: (0, 0)),
                  pl.BlockSpec((1, d, tn), lambda j: (layer, 0, j)),
                  pl.BlockSpec((1, 1, tn), lambda j: (layer, 0, j))],
        out_specs=pl.BlockSpec((8, tn), lambda j: (0, j)),
        out_shape=jax.ShapeDtypeStruct((8, n), F32),
        compiler_params=_cparams("arbitrary"),
        name="adaln",
    )(rows, w, b.reshape(depth, 1, n))


def _hidden_operands(h, nct):
    if isinstance(h, tuple):
        hc, hl = h
        d = hc.shape[-1]
        specs = [pl.BlockSpec((1, TILE, d), lambda b, i: (b, jnp.minimum(i, nct - 1), 0)),
                 pl.BlockSpec((1, TILE, d), lambda b, i: (b, jnp.maximum(i - nct, 0), 0))]
        return [hc, hl], specs, hc.shape[1] + hl.shape[1]
    return [h], [pl.BlockSpec((1, TILE, h.shape[-1]), lambda b, i: (b, i, 0))], h.shape[1]


def _hidden_tile(h_refs, nct):
    if len(h_refs) == 2:
        return jnp.where(pl.program_id(1) < nct, h_refs[0][0], h_refs[1][0])
    return h_refs[0][0]


def _modproj_kernel(*refs, n_h, nct):
    mod_ref, g_ref, w_ref, o_ref = refs[n_h:]
    mod = mod_ref[0, 0]
    a = _modulate(_hidden_tile(refs[:n_h], nct), g_ref[...], mod[0:1], mod[1:2])
    o_ref[0] = _dot(a.astype(BF16), w_ref[...]).astype(o_ref.dtype)


def _modproj(h, mod, g, w_bf16, nct, out_dtype):
    h_ops, h_specs, t = _hidden_operands(h, nct)
    bsz, _, d = h_ops[0].shape
    n = w_bf16.shape[1]
    return pl.pallas_call(
        functools.partial(_modproj_kernel, n_h=len(h_ops), nct=nct),
        grid=(bsz, t // TILE),
        in_specs=h_specs + [pl.BlockSpec((1, 1, 6, d), lambda b, i: (b, jnp.where(i < nct, 0, 1), 0, 0)),
                            pl.BlockSpec((1, d), lambda b, i: (0, 0)),
                            pl.BlockSpec((d, n), lambda b, i: (0, 0))],
        out_specs=pl.BlockSpec((1, TILE, n), lambda b, i: (b, i, 0)),
        out_shape=jax.ShapeDtypeStruct((bsz, t, n), out_dtype),
        compiler_params=_cparams("arbitrary", "arbitrary"),
        name="modproj",
    )(*h_ops, mod, g.reshape(1, d), w_bf16)


def _router_kernel(h_ref, mod_ref, g_ref, wr_ref, a_ref, aff_ref, afft_ref, *, n_exp):
    mod = mod_ref[0, 0]
    a = _modulate(h_ref[0], g_ref[...], mod[3:4], mod[4:5])
    a_ref[0] = a.astype(BF16)
    logits = _dot_hi(a, wr_ref[...])
    lane = lax.broadcasted_iota(I32, logits.shape, 1)
    logits = jnp.where(lane < n_exp, logits, -jnp.inf)
    p = jnp.exp(logits - jnp.max(logits, axis=-1, keepdims=True))
    aff = p / jnp.sum(p, axis=-1, keepdims=True)
    aff_ref[0] = aff
    afft_ref[0] = aff.T


def _router(h, mod, g, w_router, nct):
    bsz, t, d = h.shape
    n_exp = w_router.shape[1]
    wr = jnp.zeros((d, LANES), F32).at[:, :n_exp].set(w_router)
    return pl.pallas_call(
        functools.partial(_router_kernel, n_exp=n_exp),
        grid=(bsz, t // TILE),
        in_specs=[pl.BlockSpec((1, TILE, d), lambda b, i: (b, i, 0)),
                  pl.BlockSpec((1, 1, 6, d), lambda b, i: (b, jnp.where(i < nct, 0, 1), 0, 0)),
                  pl.BlockSpec((1, d), lambda b, i: (0, 0)),
                  pl.BlockSpec((d, LANES), lambda b, i: (0, 0))],
        out_specs=[pl.BlockSpec((1, TILE, d), lambda b, i: (b, i, 0)),
                   pl.BlockSpec((1, TILE, LANES), lambda b, i: (b, i, 0)),
                   pl.BlockSpec((1, LANES, TILE), lambda b, i: (b, 0, i))],
        out_shape=[jax.ShapeDtypeStruct((bsz, t, d), BF16),
                   jax.ShapeDtypeStruct((bsz, t, LANES), F32),
                   jax.ShapeDtypeStruct((bsz, LANES, t), F32)],
        compiler_params=_cparams("arbitrary", "arbitrary"),
        name="router",
    )(h, mod, g.reshape(1, d), wr)


def _select_kernel(afft_ref, pos_ref, post_ref, cb_ref, cnt_ref, sel_ref, bucket_ref, *, n_exp, n_ctx, k_ctx, k_lat, t):
    tok = lax.broadcasted_iota(I32, (n_exp, t), 1)
    is_ctx = tok < n_ctx
    min_normal = 0x00800000

    def choose(in_set, k):
        def vals():
            return jnp.where(in_set, afft_ref[0][:n_exp], -1.0)

        def bit_step(it, thr):
            cand = thr | jnp.left_shift(jnp.int32(1), 30 - it)
            cnt = jnp.sum(jnp.where(vals() >= pltpu.bitcast(cand, F32), 1, 0), axis=1, keepdims=True)
            return jnp.where(cnt >= k, cand, thr)
        thr = lax.fori_loop(0, 31, bit_step, jnp.zeros((n_exp, 1), I32))
        lo = pltpu.bitcast(thr, F32)
        hi = pltpu.bitcast(jnp.maximum(thr + 1, min_normal), F32)
        v = vals()
        above = v >= hi
        sel_ref[...] = jnp.where(above, 1.0, sel_ref[...])
        bucket_ref[...] = jnp.where((v >= lo) & jnp.logical_not(above), 1.0, 0.0)
        need0 = k - jnp.sum(jnp.where(above, 1, 0), axis=1, keepdims=True)

        def peel(need):
            in_bucket = bucket_ref[...] > 0.0
            v = vals()
            top = jnp.max(jnp.where(in_bucket, v, -1.0), axis=1, keepdims=True)
            first = jnp.min(jnp.where(in_bucket & (v == top), tok, t), axis=1, keepdims=True)
            pick = (tok == first) & (need > 0)
            sel_ref[...] = jnp.where(pick, 1.0, sel_ref[...])
            bucket_ref[...] = jnp.where(pick, 0.0, bucket_ref[...])
            return need - 1
        lax.while_loop(lambda need: jnp.max(need) > 0, peel, need0)

    sel_ref[...] = jnp.zeros_like(sel_ref)
    choose(jnp.logical_not(is_ctx), k_lat)
    if k_ctx > 0:
        choose(is_ctx, k_ctx)

    rr = lax.broadcasted_iota(I32, (LANES, LANES), 0)
    cc = lax.broadcasted_iota(I32, (LANES, LANES), 1)
    upper = jnp.where(rr <= cc, 1.0, 0.0).astype(BF16)
    carry = jnp.zeros((n_exp, 1), F32)
    for blk in range(t // LANES):
        c = _dot(sel_ref[:, blk * LANES:(blk + 1) * LANES].astype(BF16), upper) + carry
        cnt_ref[:, blk * LANES:(blk + 1) * LANES] = c
        carry = c[:, LANES - 1:LANES]
    posinc = cnt_ref[...].astype(I32)
    pos = jnp.where(sel_ref[...] > 0.0, posinc - 1, -1)
    pos_ref[0] = pos

    lane = lax.broadcasted_iota(I32, (n_exp, LANES), 1)
    cb = jnp.zeros((n_exp, LANES), I32)
    for i in range(1, t // SUBT + 1):
        cb = jnp.where(lane == i, posinc[:, i * SUBT - 1:i * SUBT], cb)
    cb_ref[0] = cb

    padded = jnp.concatenate([pos.astype(F32), jnp.full((LANES - n_exp, t), -1.0, F32)], axis=0)
    for i in range(t // TILE):
        post_ref[0, i * TILE:(i + 1) * TILE, :] = padded[:, i * TILE:(i + 1) * TILE].T.astype(I32)


def _select(afft, n_exp, n_ctx, k_ctx, k_lat):
    bsz, _, t = afft.shape
    assert n_ctx % LANES == 0 and t % TILE == 0 and t // SUBT < LANES
    return pl.pallas_call(
        functools.partial(_select_kernel, n_exp=n_exp, n_ctx=n_ctx, k_ctx=k_ctx, k_lat=k_lat, t=t),
        grid=(bsz,),
        in_specs=[pl.BlockSpec((1, LANES, t), lambda b: (b, 0, 0))],
        out_specs=[pl.BlockSpec((1, n_exp, t), lambda b: (b, 0, 0)),
                   pl.BlockSpec((1, t, LANES), lambda b: (b, 0, 0)),
                   pl.BlockSpec((1, n_exp, LANES), lambda b: (b, 0, 0))],
        out_shape=[jax.ShapeDtypeStruct((bsz, n_exp, t), I32),
                   jax.ShapeDtypeStruct((bsz, t, LANES), I32),
                   jax.ShapeDtypeStruct((bsz, n_exp, LANES), I32)],
        scratch_shapes=[pltpu.VMEM((n_exp, t), F32)] * 3,
        compiler_params=_cparams("arbitrary"),
        name="select",
    )(afft)


def _ffn_rows(cap_tot):
    for rc in range(min(cap_tot, 512), 7, -8):
        if cap_tot % rc == 0:
            return rc
    raise ValueError(cap_tot)


def _expert_kernel(cb_ref, pos_ref, a_ref, wg_ref, wu_ref, wd_ref, y_ref, x_ref, *, n_exp, nsub, sub_per, cap_tot):
    e, b, ch = pl.program_id(0), pl.program_id(1), pl.program_id(2)

    @pl.when(ch == 0)
    def _():
        x_ref[...] = jnp.zeros_like(x_ref)

    base = (b * n_exp + e) * (nsub + 1) + ch * sub_per
    row = lax.broadcasted_iota(I32, (WIN_X, SUBT), 0)

    def gather(j, carry):
        start = cb_ref[base + j]
        start_al = pl.multiple_of((start // 8) * 8, 8)
        rel = pos_ref[0, 0, j] - start_al
        onehot = jnp.where(row == rel, 1.0, 0.0).astype(BF16)
        tok0 = pl.multiple_of(j * SUBT, SUBT)
        x_ref[pl.ds(start_al, WIN_X), :] += _dot(onehot, a_ref[0, pl.ds(tok0, SUBT), :])
        return carry
    lax.fori_loop(0, sub_per, gather, 0, unroll=4)

    @pl.when(ch == pl.num_programs(2) - 1)
    def _():
        rc = _ffn_rows(cap_tot)
        wg = wg_ref[0, 0].astype(BF16)
        wu = wu_ref[0, 0].astype(BF16)
        wd = wd_ref[0, 0].astype(BF16)
        for r0 in range(0, cap_tot, rc):
            xc = x_ref[r0:r0 + rc, :].astype(BF16)
            hid = _silu(_dot(xc, wg)) * _dot(xc, wu)
            y_ref[0, 0, r0:r0 + rc, :] = _dot(hid.astype(BF16), wd).astype(y_ref.dtype)
        y_ref[0, 0, cap_tot:, :] = jnp.zeros((y_ref.shape[2] - cap_tot, y_ref.shape[-1]), y_ref.dtype)


def _experts(cb_flat, pos, a, wg, wu, wd, layer, cap_tot):
    bsz, n_exp, t = pos.shape
    d = a.shape[-1]
    ff = wg.shape[-1]
    nsub = t // SUBT
    ntiles = t // TILE
    chunk = TILE * max(m for m in range(1, 13) if ntiles % m == 0)
    sub_per = chunk // SUBT
    assert cap_tot % 16 == 0
    rows_x = cap_tot + WIN_X
    rows_y = cap_tot + WIN_Y
    pos5 = pos.reshape(bsz, n_exp, nsub, 1, SUBT)
    grid_spec = pltpu.PrefetchScalarGridSpec(
        num_scalar_prefetch=1,
        grid=(n_exp, bsz, t // chunk),
        in_specs=[pl.BlockSpec((1, 1, sub_per, 1, SUBT), lambda e, b, c, cb: (b, e, c, 0, 0)),
                  pl.BlockSpec((1, chunk, d), lambda e, b, c, cb: (b, c, 0)),
                  pl.BlockSpec((1, 1, d, ff), lambda e, b, c, cb: (layer, e, 0, 0)),
                  pl.BlockSpec((1, 1, d, ff), lambda e, b, c, cb: (layer, e, 0, 0)),
                  pl.BlockSpec((1, 1, ff, d), lambda e, b, c, cb: (layer, e, 0, 0))],
        out_specs=pl.BlockSpec((1, 1, rows_y, d), lambda e, b, c, cb: (b, e, 0, 0)),
        scratch_shapes=[pltpu.VMEM((rows_x, d), F32)],
    )
    return pl.pallas_call(
        functools.partial(_expert_kernel, n_exp=n_exp, nsub=nsub, sub_per=sub_per, cap_tot=cap_tot),
        grid_spec=grid_spec,
        out_shape=jax.ShapeDtypeStruct((bsz, n_exp, rows_y, d), BF16),
        compiler_params=_cparams("arbitrary", "arbitrary", "arbitrary"),
        name="experts",
    )(cb_flat, pos5, a, wg, wu, wd)


def _combine(cb_flat, h, mod, aff, post, y, g_final, nct, final_norm):
    bsz, t, d = h.shape
    n_exp = y.shape[1]
    ntiles = t // TILE
    first = nct if final_norm else 0
    n_win = n_exp * (TILE // SUBT)
    grid_spec = pltpu.PrefetchScalarGridSpec(
        num_scalar_prefetch=1,
        grid=(bsz, ntiles - first),
        in_specs=[pl.BlockSpec((1, TILE, d), lambda b, i, cb: (b, i + first, 0)),
                  pl.BlockSpec((1, 1, 6, d), lambda b, i, cb: (b, jnp.where(i + first < nct, 0, 1), 0, 0)),
                  pl.BlockSpec((1, TILE, LANES), lambda b, i, cb: (b, i + first, 0)),
                  pl.BlockSpec((1, TILE, LANES), lambda b, i, cb: (b, i + first, 0)),
                  pl.BlockSpec(memory_space=pl.ANY),
                  pl.BlockSpec((1, d), lambda b, i, cb: (0, 0))],
        out_specs=pl.BlockSpec((1, TILE, d), lambda b, i, cb: (b, i, 0)),
        scratch_shapes=[pltpu.VMEM((2, n_win, WIN_Y, d), y.dtype), pltpu.SemaphoreType.DMA((2, n_win))],
    )
    return pl.pallas_call(
        functools.partial(_combine_kernel, n_exp=n_exp, nsub=t // SUBT, first=first, final_norm=final_norm),
        grid_spec=grid_spec,
        out_shape=jax.ShapeDtypeStruct((bsz, t - first * TILE, d), F32),
        compiler_params=_cparams("arbitrary", "arbitrary"),
        name="combine",
    )(cb_flat, h, mod, aff, post, y, g_final.reshape(1, d))


def _combine_kernel(cb_ref, h_ref, mod_ref, aff_ref, post_ref, y_hbm, g_ref, o_ref, ybuf, sem,
                    *, n_exp, nsub, first, final_norm):
    n_i = pl.num_programs(1)
    step = pl.program_id(0) * n_i + pl.program_id(1)
    n_steps = pl.num_programs(0) * n_i
    per_tile = TILE // SUBT

    def window(at_step, sub, e):
        b = at_step // n_i
        i = at_step - b * n_i + first
        start = cb_ref[(b * n_exp + e) * (nsub + 1) + i * per_tile + sub]
        start_al = pl.multiple_of((start // 16) * 16, 16)
        buf, slot = at_step % 2, sub * n_exp + e
        copy = pltpu.make_async_copy(y_hbm.at[b, e, pl.ds(start_al, WIN_Y), :], ybuf.at[buf, slot],
                                     sem.at[buf, slot])
        return copy, start_al

    def start_windows(at_step):
        for sub in range(per_tile):
            for e in range(n_exp):
                window(at_step, sub, e)[0].start()

    @pl.when(step == 0)
    def _():
        start_windows(step)

    @pl.when(step + 1 < n_steps)
    def _():
        start_windows(step + 1)

    lane = lax.broadcasted_iota(I32, (SUBT, WIN_Y), 1)
    mod = mod_ref[0, 0]
    for sub in range(per_tile):
        rows = slice(sub * SUBT, (sub + 1) * SUBT)
        post = post_ref[0, rows, :]
        aff = aff_ref[0, rows, :]
        acc = jnp.zeros((SUBT, o_ref.shape[-1]), F32)
        starts = []
        for e in range(n_exp):
            copy, start_al = window(step, sub, e)
            copy.wait()
            starts.append(start_al)
        for e, start_al in enumerate(starts):
            rel = post[:, e:e + 1] - start_al
            onehot = jnp.where(lane == rel, 1.0, 0.0).astype(BF16)
            acc = acc + aff[:, e:e + 1] * _dot(onehot, ybuf[step % 2, sub * n_exp + e])
        out = h_ref[0, rows, :] + mod[5:6] * acc
        if final_norm:
            ms = jnp.mean(out * out, axis=-1, keepdims=True)
            out = out * lax.rsqrt(ms + EPS) * g_ref[...]
        o_ref[0, rows, :] = out


def _moe(h, mod, ln2_g, w_router, wg, wu, wd, layer, g_final, *, n_ctx, route_ctx, final_norm):
    bsz, t, d = h.shape
    n_exp = w_router.shape[1]
    nct = n_ctx // TILE
    n_lat = t - n_ctx
    k_lat = max(1, EC_CAPACITY * n_lat // n_exp)
    k_ctx = max(1, EC_CAPACITY * n_ctx // n_exp) if route_ctx else 0
    a, aff, afft = _router(h, mod, ln2_g, w_router, nct)
    pos, post, cb = _select(afft, n_exp, n_ctx, k_ctx, k_lat)
    cb_flat = cb[:, :, :t // SUBT + 1].reshape(-1)
    y = _experts(cb_flat, pos, a, wg, wu, wd, layer, k_ctx + k_lat)
    return _combine(cb_flat, h, mod, aff, post, y, g_final, nct, final_norm)


HEAD_SHIFT = int(math.log2(HEAD_DIM))


def _same_head(rows, cols):
    r = lax.shift_right_logical(lax.broadcasted_iota(I32, (rows, cols), 0), HEAD_SHIFT)
    c = lax.shift_right_logical(lax.broadcasted_iota(I32, (rows, cols), 1), HEAD_SHIFT)
    return jnp.where(r == c, 1.0, 0.0)


def _head_sum(x2):
    n = x2.shape[-1]
    return _dot_exact_rhs(x2, _same_head(n, n).astype(BF16))


def _outproj_even_kernel(*refs, n_h, nct, a_dim):
    mod_ref, of_ref, ob_ref, z_ref, at_ref, g_ref, w_ref, o_ref = refs[n_h:]
    mod = mod_ref[0, 0]
    o = of_ref[0] + ob_ref[0]
    ms = _head_sum(o * o) * (1.0 / HEAD_DIM)
    ya = o * lax.rsqrt(ms + EPS) * g_ref[...] * _silu(z_ref[0])
    y = _dot(ya.astype(BF16), w_ref[:a_dim, :]) + _dot(at_ref[0], w_ref[a_dim:, :])
    o_ref[0] = _hidden_tile(refs[:n_h], nct) + mod[2:3] * y


def _outproj_even(h, mod, o_delta, p, o_attn, out_norm_g, w_bf16, nct, *, a_dim):
    h_ops, h_specs, t = _hidden_operands(h, nct)
    bsz, _, d = h_ops[0].shape
    o_f, o_b = o_delta
    g = jnp.tile(out_norm_g, a_dim // HEAD_DIM).reshape(1, a_dim)
    tok = lambda b, i: (b, i, 0)
    return pl.pallas_call(
        functools.partial(_outproj_even_kernel, n_h=len(h_ops), nct=nct, a_dim=a_dim),
        grid=(bsz, t // TILE),
        in_specs=h_specs + [pl.BlockSpec((1, 1, 6, d), lambda b, i: (b, jnp.where(i < nct, 0, 1), 0, 0)),
                            pl.BlockSpec((1, TILE, a_dim), tok),
                            pl.BlockSpec((1, TILE, a_dim), tok),
                            pl.BlockSpec((1, TILE, a_dim), lambda b, i: (b, i, 3)),
                            pl.BlockSpec((1, TILE, a_dim), tok),
                            pl.BlockSpec((1, a_dim), lambda b, i: (0, 0)),
                            pl.BlockSpec((d, d), lambda b, i: (0, 0))],
        out_specs=pl.BlockSpec((1, TILE, d), tok),
        out_shape=jax.ShapeDtypeStruct((bsz, t, d), F32),
        compiler_params=_cparams("arbitrary", "arbitrary"),
        name="outproj_even",
    )(*h_ops, mod, o_f, o_b, p, o_attn, g, w_bf16)


def _outproj_odd_kernel(h_ref, mod_ref, a_ref, w_ref, o_ref):
    mod = mod_ref[0, 0]
    o_ref[0] = h_ref[0] + mod[2:3] * _dot(a_ref[0], w_ref[...])


def _outproj_odd(h, mod, o_na, w_bf16, nct):
    bsz, t, d = h.shape
    n_lat = o_na.shape[1]
    return pl.pallas_call(
        _outproj_odd_kernel,
        grid=(bsz, n_lat // TILE),
        in_specs=[pl.BlockSpec((1, TILE, d), lambda b, i: (b, i + nct, 0)),
                  pl.BlockSpec((1, 1, 6, d), lambda b, i: (b, 1, 0, 0)),
                  pl.BlockSpec((1, TILE, d), lambda b, i: (b, i, 0)),
                  pl.BlockSpec((d, d), lambda b, i: (0, 0))],
        out_specs=pl.BlockSpec((1, TILE, d), lambda b, i: (b, i, 0)),
        out_shape=jax.ShapeDtypeStruct((bsz, n_lat, d), F32),
        compiler_params=_cparams("arbitrary", "arbitrary"),
        name="outproj_odd",
    )(h, mod, o_na, w_bf16)


def _rope_tables(n_ctx, n_lat):
    tt = jnp.arange(n_lat)
    n_freq = HEAD_DIM // 4
    inv = ROPE_THETA ** (-jnp.arange(n_freq, dtype=F32) / n_freq)
    ang = jnp.concatenate([(tt // GRID_W).astype(F32)[:, None] * inv, (tt % GRID_W).astype(F32)[:, None] * inv], -1)
    cos = jnp.concatenate([jnp.ones((n_ctx, 2 * n_freq), F32), jnp.cos(ang)], axis=0)
    sin = jnp.concatenate([jnp.zeros((n_ctx, 2 * n_freq), F32), jnp.sin(ang)], axis=0)
    reps = LANES // HEAD_DIM
    return jnp.tile(cos, (1, 2 * reps)), jnp.tile(jnp.concatenate([-sin, sin], axis=1), (1, reps))


def _rope(x, cos, sin):
    n = x.shape[-1]
    half = HEAD_DIM // 2
    lane = lax.broadcasted_iota(I32, x.shape, 1)
    first = (lane & (HEAD_DIM - 1)) < half
    partner = jnp.where(first, pltpu.roll(x, n - half, 1), pltpu.roll(x, half, 1))
    reps = n // LANES
    cos_n = jnp.concatenate([cos] * reps, axis=1) if reps > 1 else cos
    sin_n = jnp.concatenate([sin] * reps, axis=1) if reps > 1 else sin
    return x * cos_n + partner * sin_n


def _gqa_prep_kernel(q_ref, k_ref, v_ref, cos_ref, sin_ref, gq_ref, gk_ref, qo_ref, ko_ref, vo_ref):
    cos, sin = cos_ref[...], sin_ref[...]
    q = q_ref[0]
    q = q * lax.rsqrt(_head_sum(q * q) * (1.0 / HEAD_DIM) + EPS) * gq_ref[...] * (HEAD_DIM ** -0.5 * LOG2_E)
    qo_ref[0] = _rope(q, cos, sin).astype(BF16)
    k = k_ref[0]
    k = k * lax.rsqrt(_head_sum(k * k) * (1.0 / HEAD_DIM) + EPS) * gk_ref[...]
    k = _rope(k, cos, sin).astype(BF16)
    v = v_ref[0].astype(BF16)
    for hh in range(k.shape[-1] // HEAD_DIM):
        ko_ref[0, hh] = k[:, hh * HEAD_DIM:(hh + 1) * HEAD_DIM]
        vo_ref[0, hh] = v[:, hh * HEAD_DIM:(hh + 1) * HEAD_DIM]


def _gqa_prep(p, cos, sin, q_norm_g, k_norm_g, *, a_dim, kv_dim):
    bsz, t, _ = p.shape
    q_dim = a_dim
    kvh = kv_dim // HEAD_DIM
    assert kv_dim == LANES
    c_q = (4 * a_dim) // q_dim
    c_k = (4 * a_dim + q_dim) // kv_dim
    gq = jnp.tile(q_norm_g, q_dim // HEAD_DIM).reshape(1, q_dim)
    gk = jnp.tile(k_norm_g, kvh).reshape(1, kv_dim)
    return pl.pallas_call(
        _gqa_prep_kernel,
        grid=(bsz, t // TILE),
        in_specs=[pl.BlockSpec((1, TILE, q_dim), lambda b, i: (b, i, c_q)),
                  pl.BlockSpec((1, TILE, kv_dim), lambda b, i: (b, i, c_k)),
                  pl.BlockSpec((1, TILE, kv_dim), lambda b, i: (b, i, c_k + 1)),
                  pl.BlockSpec((TILE, LANES), lambda b, i: (i, 0)),
                  pl.BlockSpec((TILE, LANES), lambda b, i: (i, 0)),
                  pl.BlockSpec((1, q_dim), lambda b, i: (0, 0)),
                  pl.BlockSpec((1, kv_dim), lambda b, i: (0, 0))],
        out_specs=[pl.BlockSpec((1, TILE, q_dim), lambda b, i: (b, i, 0)),
                   pl.BlockSpec((1, kvh, TILE, HEAD_DIM), lambda b, i: (b, 0, i, 0)),
                   pl.BlockSpec((1, kvh, TILE, HEAD_DIM), lambda b, i: (b, 0, i, 0))],
        out_shape=[jax.ShapeDtypeStruct((bsz, t, q_dim), BF16),
                   jax.ShapeDtypeStruct((bsz, kvh, t, HEAD_DIM), BF16),
                   jax.ShapeDtypeStruct((bsz, kvh, t, HEAD_DIM), BF16)],
        compiler_params=_cparams("arbitrary", "arbitrary"),
        name="gqa_prep",
    )(p, p, p, cos, sin, gq, gk)


SUB_K = 256
Q_SLAB = 256


def _gqa_kernel(q_ref, k_ref, v_ref, o_ref, qs_ref, m_ref, l_ref, acc_ref, *, n_ctx, nct, tq, tk, n_heads, kvh):
    i, kv = pl.program_id(1), pl.program_id(2)
    total = n_heads * tq
    per_kv = total // kvh

    @pl.when(kv == 0)
    def _():
        m_ref[...] = jnp.full_like(m_ref, -jnp.inf)
        l_ref[...] = jnp.zeros_like(l_ref)
        acc_ref[...] = jnp.zeros_like(acc_ref)
        q = q_ref[0]
        for hh in range(n_heads):
            qs_ref[hh * tq:(hh + 1) * tq, :] = q[:, hh * HEAD_DIM:(hh + 1) * HEAD_DIM]

    def attend(n_keys, n_valid):
        edges = ([0, Q_SLAB // 2] + list(range(Q_SLAB + Q_SLAB // 2, total - Q_SLAB // 2, Q_SLAB))
                 + [total - Q_SLAB // 2, total])
        edges = sorted(set(edges) | set(range(0, total + 1, per_kv)))
        slabs = [slice(a, b) for a, b in zip(edges[:-1], edges[1:])]
        keys = lambda rows: k_ref[0, rows.start // per_kv, :n_keys, :]
        vals = lambda rows: v_ref[0, rows.start // per_kv, :n_keys, :]
        scores = {0: _dot_t(qs_ref[slabs[0], :], keys(slabs[0]))}
        for idx, rows in enumerate(slabs):
            if idx + 1 < len(slabs):
                scores[idx + 1] = _dot_t(qs_ref[slabs[idx + 1], :], keys(slabs[idx + 1]))
            s = scores.pop(idx)
            if n_valid is not None:
                s = jnp.where(lax.broadcasted_iota(I32, s.shape, 1) < n_valid, s, -jnp.inf)
            m_old = m_ref[rows, :]
            m_new = jnp.maximum(m_old, jnp.max(s, axis=-1, keepdims=True))
            alpha = jnp.exp2(m_old - m_new)
            p = jnp.exp2(s - m_new)
            l_ref[rows, :] = alpha * l_ref[rows, :] + jnp.sum(p, axis=-1, keepdims=True)
            acc_ref[rows, :] = alpha * acc_ref[rows, :] + _dot(p.astype(BF16), vals(rows))
            m_ref[rows, :] = m_new

    @pl.when(i >= nct)
    def _():
        attend(tk, None)

    @pl.when((i < nct) & (kv == 0))
    def _():
        n_keys = -(-n_ctx // SUB_K) * SUB_K
        attend(n_keys, n_ctx if n_keys != n_ctx else None)

    @pl.when(kv == pl.num_programs(2) - 1)
    def _():
        out = acc_ref[...] / l_ref[...]
        for hh in range(n_heads):
            o_ref[0, :, hh * HEAD_DIM:(hh + 1) * HEAD_DIM] = out[hh * tq:(hh + 1) * tq, :].astype(o_ref.dtype)


def _gqa_attention(q, k, v, *, n_ctx):
    bsz, t, q_dim = q.shape
    kvh = k.shape[1]
    n_heads = q_dim // HEAD_DIM
    tq = TILE
    nt = t // TILE
    tk = TILE * max(m for m in range(1, 13) if nt % m == 0)
    assert n_ctx <= tk and tk % SUB_K == 0 and (n_heads // kvh * tq) % Q_SLAB == 0
    return pl.pallas_call(
        functools.partial(_gqa_kernel, n_ctx=n_ctx, nct=n_ctx // tq, tq=tq, tk=tk, n_heads=n_heads, kvh=kvh),
        grid=(bsz, t // tq, t // tk),
        in_specs=[pl.BlockSpec((1, tq, q_dim), lambda b, i, j: (b, i, 0)),
                  pl.BlockSpec((1, kvh, tk, HEAD_DIM), lambda b, i, j: (b, 0, j, 0)),
                  pl.BlockSpec((1, kvh, tk, HEAD_DIM), lambda b, i, j: (b, 0, j, 0))],
        out_specs=pl.BlockSpec((1, tq, q_dim), lambda b, i, j: (b, i, 0)),
        out_shape=jax.ShapeDtypeStruct((bsz, t, q_dim), BF16),
        scratch_shapes=[pltpu.VMEM((n_heads * tq, HEAD_DIM), BF16),
                        pltpu.VMEM((n_heads * tq, 1), F32),
                        pltpu.VMEM((n_heads * tq, 1), F32),
                        pltpu.VMEM((n_heads * tq, HEAD_DIM), F32)],
        compiler_params=_cparams("arbitrary", "arbitrary", "arbitrary"),
        name="gqa_attention",
    )(q, k, v)


def _natten_bias(rpb):
    n_heads, n_off, n_rel = rpb.shape
    col = np.arange(GRID_W)
    col_start = np.clip(col - NA_COLS // 2, 0, GRID_W - NA_COLS)
    valid = (col[None, :] >= col_start[:, None]) & (col[None, :] < col_start[:, None] + NA_COLS)
    rel = col[None, :] - col[:, None] + NA_COLS - 1
    onehot = ((rel[None] == np.arange(n_rel)[:, None, None]) & valid[None]).astype(np.float32)
    tab = jnp.einsum('hrk,kcd->hrcd', rpb, jnp.asarray(onehot), precision=lax.Precision.HIGHEST)
    tab = jnp.where(valid[None, None], tab, -jnp.inf)
    pairs = jnp.concatenate([tab[:, :-1], tab[:, 1:]], axis=-1)
    return pairs.reshape(n_heads // 2, 2, n_off - 1, GRID_W, 2 * GRID_W).astype(F32)


def _natten_kernel(*refs, n_ctx, n_rows, rblk, n_q):
    q_refs = refs[:n_q]
    k_ref, v_ref, bias_ref, o_ref = refs[n_q:]
    per_q = TILE // GRID_W
    rb = pl.program_id(2)
    first_head = lax.broadcasted_iota(I32, (1, LANES), 1) < HEAD_DIM
    kc = k_ref[0, 0:n_ctx, :]
    vc = v_ref[0, 0:n_ctx, :]
    band = NA_ROWS * GRID_W
    qs, ks, vs, biases = [], [], [], []
    for rr in range(rblk):
        rg = rb * rblk + rr
        rs = jnp.clip(rg - NA_ROWS // 2, 0, n_rows - NA_ROWS)
        variant = rs - rg + (NA_ROWS - 1)
        start = pl.multiple_of(n_ctx + rs * GRID_W, GRID_W)
        keys = jnp.concatenate([kc, k_ref[0, pl.ds(start, band), :]], axis=0)
        vals = jnp.concatenate([vc, v_ref[0, pl.ds(start, band), :]], axis=0)
        q_lo = (rr % per_q) * GRID_W
        q = q_refs[rr // per_q][0, q_lo:q_lo + GRID_W, :] * (HEAD_DIM ** -0.5)
        for hh in range(2):
            qs.append(jnp.where(first_head if hh == 0 else jnp.logical_not(first_head), q, jnp.zeros_like(q)))
            ks.append(keys)
            vs.append(vals)
            biases.append(jnp.concatenate([bias_ref[0, hh, variant + j] for j in range(0, NA_ROWS, 2)], axis=1))
    batch = (((2,), (2,)), ((0,), (0,)))
    s = lax.dot_general(jnp.stack(qs), jnp.stack(ks), batch, preferred_element_type=F32)
    sc = s[:, :, :n_ctx]
    sb = s[:, :, n_ctx:] + jnp.stack(biases)
    m = jnp.maximum(jnp.max(sc, axis=-1, keepdims=True), jnp.max(sb, axis=-1, keepdims=True))
    pc = jnp.exp(sc - m)
    pb = jnp.exp(sb - m)
    l = jnp.sum(pc, axis=-1, keepdims=True) + jnp.sum(pb, axis=-1, keepdims=True)
    p = jnp.concatenate([pc, pb], axis=-1).astype(BF16)
    o = lax.dot_general(p, jnp.stack(vs), (((2,), (1,)), ((0,), (0,))), preferred_element_type=F32) / l
    for rr in range(rblk):
        o_ref[0, rr * GRID_W:(rr + 1) * GRID_W, :] = jnp.where(first_head, o[2 * rr], o[2 * rr + 1]).astype(o_ref.dtype)


def _natten(p, bias, *, n_ctx, d):
    bsz, t, _ = p.shape
    n_lat = t - n_ctx
    n_rows = n_lat // GRID_W
    n_q = 2 if n_lat % (2 * TILE) == 0 else 1
    rblk = n_q * TILE // GRID_W
    npairs = d // LANES
    nct = n_ctx // TILE
    assert n_rows >= NA_ROWS and n_ctx % TILE == 0
    return pl.pallas_call(
        functools.partial(_natten_kernel, n_ctx=n_ctx, n_rows=n_rows, rblk=rblk, n_q=n_q),
        grid=(bsz, npairs, n_lat // (n_q * TILE)),
        in_specs=[pl.BlockSpec((1, TILE, LANES), lambda b, hp, r, j=j: (b, nct + n_q * r + j, hp)) for j in range(n_q)] + [
                  pl.BlockSpec((1, t, LANES), lambda b, hp, r: (b, 0, npairs + hp)),
                  pl.BlockSpec((1, t, LANES), lambda b, hp, r: (b, 0, 2 * npairs + hp)),
                  pl.BlockSpec((1, 2, 2 * NA_ROWS - 2, GRID_W, 2 * GRID_W), lambda b, hp, r: (hp, 0, 0, 0, 0))],
        out_specs=pl.BlockSpec((1, n_q * TILE, LANES), lambda b, hp, r: (b, r, hp)),
        out_shape=jax.ShapeDtypeStruct((bsz, n_lat, d), BF16),
        compiler_params=_cparams("arbitrary", "arbitrary", "arbitrary"),
        name="natten",
    )(*([p] * n_q), p, p, bias)


def _dot_exact_lhs(a_bf16, b):
    b1, b2, b3 = _split3(b)
    return _dot(a_bf16, b1) + _dot(a_bf16, b2) + _dot(a_bf16, b3)


def _delta_prep_kernel(x_ref, prev_ref, next_ref, gate_ref, w_ref, alog_ref, dt_ref, xh_ref, gn_ref, gnt_ref, xs_ref,
                       *, nct, a_dim):
    i = pl.program_id(1)
    last = pl.num_programs(1) - 1
    x = x_ref[0]
    rows = x.shape[0]
    xs_ref[0:8, :] = prev_ref[0]
    xs_ref[8:8 + rows, :] = x
    xs_ref[8 + rows:16 + rows, :] = next_ref[0]
    r = lax.broadcasted_iota(I32, (rows, 1), 0)
    at_start = (i == 0) | (i == nct)
    at_end = (i == nct - 1) | (i == last)
    xm = jnp.where((r == 0) & at_start, 0.0, xs_ref[7:7 + rows, :])
    xp = jnp.where((r == rows - 1) & at_end, 0.0, xs_ref[9:9 + rows, :])
    w = w_ref[...]
    y = _silu(w[0:1] * xm + w[1:2] * x + w[2:3] * xp)
    q, k, v = y[:, :a_dim], y[:, a_dim:2 * a_dim], y[:, 2 * a_dim:]
    nh = a_dim // HEAD_DIM
    q = (q * lax.rsqrt(_head_sum(q * q) + EPS) * (HEAD_DIM ** -0.5)).astype(BF16)
    k = (k * lax.rsqrt(_head_sum(k * k) + EPS)).astype(BF16)
    v = v.astype(BF16)
    for kind, arr in enumerate((q, k, v)):
        for hh in range(nh):
            xh_ref[0, kind, hh] = arr[:, hh * HEAD_DIM:(hh + 1) * HEAD_DIM]

    g = gate_ref[0]
    z = g + dt_ref[...]
    softplus = jnp.maximum(z, 0.0) + jnp.log1p(jnp.exp(-jnp.abs(z)))
    log_a = -jnp.exp(alog_ref[...]) * softplus
    beta = _sigmoid(g)
    ri = lax.broadcasted_iota(I32, (rows, rows), 0)
    ci = lax.broadcasted_iota(I32, (rows, rows), 1)
    chunk_shift = int(math.log2(DELTA_CHUNK))
    same_chunk = lax.shift_right_logical(ri, chunk_shift) == lax.shift_right_logical(ci, chunk_shift)
    prefix = jnp.where(same_chunk & (ci <= ri), 1.0, 0.0).astype(BF16)
    suffix = jnp.where(same_chunk & (ci >= ri), 1.0, 0.0).astype(BF16)
    g_f = _dot_exact_lhs(prefix, log_a)
    g_b = _dot_exact_lhs(suffix, log_a)
    lane = lax.broadcasted_iota(I32, g.shape, 1)
    narrow = jnp.where(lane < nh, g_f, jnp.where(lane < 2 * nh, beta, jnp.where(lane < 3 * nh, g_b, beta)))
    gn_ref[0] = narrow
    gnt_ref[0] = narrow.T


def _delta_prep(p, conv_w, a_log, dt_bias, *, n_ctx, a_dim):
    bsz, t, _ = p.shape
    nh = a_dim // HEAD_DIM
    assert DELTA_CHUNK == HEAD_DIM and TILE % DELTA_CHUNK == 0 and 4 * nh <= LANES
    nct = n_ctx // TILE
    c_gate = (4 * a_dim + a_dim + 2 * LANES) // LANES
    alog = jnp.zeros((1, LANES), F32).at[0, :nh].set(a_log[0]).at[0, 2 * nh:3 * nh].set(a_log[1])
    dtb = jnp.zeros((1, LANES), F32).at[0, :nh].set(dt_bias[0]).at[0, 2 * nh:3 * nh].set(dt_bias[1])
    nblk8 = t // 8
    per = TILE // 8
    return pl.pallas_call(
        functools.partial(_delta_prep_kernel, nct=nct, a_dim=a_dim),
        grid=(bsz, t // TILE),
        in_specs=[pl.BlockSpec((1, TILE, 3 * a_dim), lambda b, i: (b, i, 0)),
                  pl.BlockSpec((1, 8, 3 * a_dim), lambda b, i: (b, jnp.maximum(i * per - 1, 0), 0)),
                  pl.BlockSpec((1, 8, 3 * a_dim), lambda b, i: (b, jnp.minimum((i + 1) * per, nblk8 - 1), 0)),
                  pl.BlockSpec((1, TILE, LANES), lambda b, i: (b, i, c_gate)),
                  pl.BlockSpec((3, 3 * a_dim), lambda b, i: (0, 0)),
                  pl.BlockSpec((1, LANES), lambda b, i: (0, 0)),
                  pl.BlockSpec((1, LANES), lambda b, i: (0, 0))],
        out_specs=[pl.BlockSpec((1, 3, nh, TILE, HEAD_DIM), lambda b, i: (b, 0, 0, i, 0)),
                   pl.BlockSpec((1, TILE, LANES), lambda b, i: (b, i, 0)),
                   pl.BlockSpec((1, LANES, TILE), lambda b, i: (b, 0, i))],
        out_shape=[jax.ShapeDtypeStruct((bsz, 3, nh, t, HEAD_DIM), BF16),
                   jax.ShapeDtypeStruct((bsz, t, LANES), F32),
                   jax.ShapeDtypeStruct((bsz, LANES, t), F32)],
        scratch_shapes=[pltpu.VMEM((TILE + 16, 3 * a_dim), F32)],
        compiler_params=_cparams("arbitrary", "arbitrary"),
        name="delta_prep",
    )(p, p, p, p, conv_w, alog, dtb)


def _bdot(a, b, ca, cb, hi=False):
    dims = (((ca,), (cb,)), ((0,), (0,)))
    dot = lambda x, y: lax.dot_general(x, y, dims, preferred_element_type=F32)
    if not hi:
        return dot(a.astype(BF16), b.astype(BF16))
    a1 = a.astype(BF16)
    a2 = (a - a1.astype(F32)).astype(BF16)
    b1 = b.astype(BF16)
    b2 = (b - b1.astype(F32)).astype(BF16)
    return dot(a1, b1) + dot(a1, b2) + dot(a2, b1)


def _delta_heads(q, k, v, g_col, beta_col, g_row, s, upper):
    nb, lc, _ = q.shape
    ri = lax.broadcasted_iota(I32, (nb, lc, lc), 1)
    ci = lax.broadcasted_iota(I32, (nb, lc, lc), 2)
    ahead = jnp.where(upper, ci - ri, ri - ci)
    tri_incl = ahead >= 0
    tri_strict = ahead > 0
    eye = jnp.where(ri == ci, 1.0, 0.0)
    k32, q32, v32 = k.astype(F32), q.astype(F32), v.astype(F32)
    eg = jnp.exp(g_col)
    g_last = jnp.where(upper, g_col[:, 0:1], g_col[:, lc - 1:lc])
    gam = jnp.exp(jnp.where(tri_incl, g_col - g_row, -jnp.inf))
    kq = _bdot(jnp.concatenate([k, q], axis=1), k, 2, 2)
    m = jnp.where(tri_strict, kq[:, :lc] * beta_col * gam, 0.0)
    a_qk = kq[:, lc:] * gam
    n = -m
    x = eye + n
    p = _bdot(n, n, 2, 1, hi=True)
    for _ in range(int(math.log2(lc)) - 2):
        px_pp = _bdot(p, jnp.concatenate([x, p], axis=2), 2, 1, hi=True)
        x, p = x + px_pp[:, :, :lc], px_pp[:, :, lc:]
    x = x + _bdot(p, x, 2, 1, hi=True)
    uw = _bdot(x, jnp.concatenate([v32 * beta_col, k32 * (beta_col * eg)], axis=2), 2, 1, hi=True)
    u, w = uw[:, :, :v.shape[2]], uw[:, :, v.shape[2]:]
    ws_qs = _bdot(jnp.concatenate([w, q32 * eg], axis=1), s, 2, 1)
    v_new = u - ws_qs[:, :lc]
    o = ws_qs[:, lc:] + _bdot(a_qk, v_new, 2, 1)
    kv = _bdot(k32 * jnp.exp(g_last - g_col), v_new, 1, 1)
    return o, s * jnp.exp(g_last) + kv


def _delta_scan_kernel(xf_ref, xb_ref, gf_ref, gb_ref, gtf_ref, gtb_ref, of_ref, ob_ref, s_ref, *, nh):
    c = pl.program_id(1)

    @pl.when(c == 0)
    def _():
        s_ref[...] = jnp.zeros_like(s_ref)

    both = lambda f, b: jnp.concatenate([f, b], axis=0)
    q, k, v = (both(xf_ref[0, i], xb_ref[0, i]) for i in range(3))
    g_cols, b_cols, g_rows = [], [], []
    for d_i, (g_ref, gt_ref) in enumerate(((gf_ref, gtf_ref), (gb_ref, gtb_ref))):
        gn = g_ref[0]
        gt = gt_ref[0, 0]
        for hh in range(nh):
            lane_g = 2 * d_i * nh + hh
            g_cols.append(gn[:, lane_g:lane_g + 1])
            b_cols.append(gn[:, lane_g + nh:lane_g + nh + 1])
            g_rows.append(gt[lane_g:lane_g + 1, :])
    upper = lax.broadcasted_iota(I32, (2 * nh, 1, 1), 0) >= nh
    o, s_new = _delta_heads(q, k, v, jnp.stack(g_cols), jnp.stack(b_cols), jnp.stack(g_rows), s_ref[...], upper)
    s_ref[...] = s_new
    for hh in range(nh):
        of_ref[0, :, hh * HEAD_DIM:(hh + 1) * HEAD_DIM] = o[hh]
        ob_ref[0, :, hh * HEAD_DIM:(hh + 1) * HEAD_DIM] = o[nh + hh]


def _delta_scan(xh, gn, gnt, *, n_ctx):
    bsz, _, nh, t, _ = xh.shape
    lc = DELTA_CHUNK
    nc, ncc = t // lc, n_ctx // lc
    gnt4 = gnt.reshape(bsz, LANES, nc, lc).transpose(0, 2, 1, 3)
    rev = lambda c: jnp.where(c < ncc, ncc - 1 - c, nc - 1 - (c - ncc))
    return pl.pallas_call(
        functools.partial(_delta_scan_kernel, nh=nh),
        grid=(bsz, nc),
        in_specs=[pl.BlockSpec((1, 3, nh, lc, HEAD_DIM), lambda b, c: (b, 0, 0, c, 0)),
                  pl.BlockSpec((1, 3, nh, lc, HEAD_DIM), lambda b, c: (b, 0, 0, rev(c), 0)),
                  pl.BlockSpec((1, lc, LANES), lambda b, c: (b, c, 0)),
                  pl.BlockSpec((1, lc, LANES), lambda b, c: (b, rev(c), 0)),
                  pl.BlockSpec((1, 1, LANES, lc), lambda b, c: (b, c, 0, 0)),
                  pl.BlockSpec((1, 1, LANES, lc), lambda b, c: (b, rev(c), 0, 0))],
        out_specs=[pl.BlockSpec((1, lc, nh * HEAD_DIM), lambda b, c: (b, c, 0)),
                   pl.BlockSpec((1, lc, nh * HEAD_DIM), lambda b, c: (b, rev(c), 0))],
        out_shape=[jax.ShapeDtypeStruct((bsz, t, nh * HEAD_DIM), F32),
                   jax.ShapeDtypeStruct((bsz, t, nh * HEAD_DIM), F32)],
        scratch_shapes=[pltpu.VMEM((2 * nh, HEAD_DIM, HEAD_DIM), F32)],
        compiler_params=_cparams("arbitrary", "arbitrary"),
        name="delta_scan",
    )(xh, xh, gn, gn, gnt4, gnt4)


def kernel(x, c, ctx, c_ctx, ada_w, ada_b, ln1_g, ln2_g, even_w_in, even_conv_w, even_a_log, even_dt_bias, even_out_norm_g, even_q_norm_g, even_k_norm_g, even_w_out, odd_w_in, odd_rpb, odd_w_out, router_w, expert_w_gate, expert_w_up, expert_w_down, final_norm_g):
    bsz, n_lat, d = x.shape
    n_ctx = ctx.shape[1]
    t = n_ctx + n_lat
    nct = n_ctx // TILE
    h = (ctx, x)
    rows = jnp.zeros((8, d), F32).at[0].set(c_ctx).at[1:1 + bsz].set(c)

    def mod_of(layer):
        m = _adaln(rows, ada_w, ada_b, layer).reshape(8, 6, d)
        return jnp.stack([jnp.broadcast_to(m[0], (bsz, 6, d)), m[1:1 + bsz]], axis=1)

    mod = mod_of(0)
    a_dim = d // 2
    a_heads = a_dim // HEAD_DIM
    w_in = even_w_in[0]
    o_qkv, o_z, o_g = 0, 3 * a_dim, 4 * a_dim
    o_bq = o_g + 4 * a_heads
    kv_dim = (w_in.shape[1] - o_bq - a_dim) // 2
    o_bk, o_bv = o_bq + a_dim, o_bq + a_dim + kv_dim
    gates_w = jnp.zeros((d, LANES), F32).at[:, :4 * a_heads].set(w_in[:, o_g:o_bq])
    w_perm = jnp.concatenate([w_in[:, :o_g], w_in[:, o_bq:], gates_w], axis=1).astype(BF16)
    p = _modproj(h, mod, ln1_g[0], w_perm, nct, F32)
    xh, gn, gnt = _delta_prep(p, even_conv_w[0], even_a_log[0], even_dt_bias[0], n_ctx=n_ctx, a_dim=a_dim)
    o_delta = _delta_scan(xh, gn, gnt, n_ctx=n_ctx)
    cos, sin = _rope_tables(n_ctx, n_lat)
    qb, kb, vb = _gqa_prep(p, cos, sin, even_q_norm_g[0], even_k_norm_g[0], a_dim=a_dim, kv_dim=kv_dim)
    o_attn = _gqa_attention(qb, kb, vb, n_ctx=n_ctx)
    h = _outproj_even(h, mod, o_delta, p, o_attn, even_out_norm_g[0], even_w_out[0].astype(BF16), nct, a_dim=a_dim)
    h = _moe(h, mod, ln2_g[0], router_w[0], expert_w_gate, expert_w_up, expert_w_down, 0, final_norm_g,
             n_ctx=n_ctx, route_ctx=True, final_norm=False)

    mod = mod_of(1)
    p = _modproj(h, mod, ln1_g[1], odd_w_in[0].astype(BF16), nct, BF16)
    o_na = _natten(p, _natten_bias(odd_rpb[0]), n_ctx=n_ctx, d=d)
    h_lat = _outproj_odd(h, mod, o_na, odd_w_out[0].astype(BF16), nct)
    return _moe(h_lat, mod, ln2_g[1], router_w[1], expert_w_gate, expert_w_up, expert_w_down, 1, final_norm_g,
                n_ctx=0, route_ctx=False, final_norm=True)
```

```python
import functools
import math

import numpy as np
import jax
import jax.numpy as jnp
from jax import lax
from jax.experimental import pallas as pl
from jax.experimental.pallas import tpu as pltpu

F32, BF16, I32 = jnp.float32, jnp.bfloat16, jnp.int32

HEAD_DIM = 64
GRID_W = 64
DELTA_CHUNK = 64
NA_ROWS = 8
NA_COLS = 16
ROPE_THETA = 10000.0
EC_CAPACITY = 2
EPS = 1e-6
LOG2_E = math.log2(math.e)

LANES = 128
TILE = 256
SUBT = 128
WIN_X = SUBT + 8
WIN_Y = SUBT + 16
VMEM_LIMIT = 56 * 1024 * 1024


def _cparams(*sem):
    return pltpu.CompilerParams(dimension_semantics=sem, vmem_limit_bytes=VMEM_LIMIT)


def _split3(a):
    a1 = a.astype(BF16)
    r1 = a - a1.astype(F32)
    a2 = r1.astype(BF16)
    a3 = (r1 - a2.astype(F32)).astype(BF16)
    return a1, a2, a3


def _dot(a, b):
    return jnp.dot(a, b, preferred_element_type=F32)


def _dot_t(a, b):
    return lax.dot_general(a, b, (((1,), (1,)), ((), ())), preferred_element_type=F32)


def _dot_exact_rhs(a, b_bf16):
    a1, a2, a3 = _split3(a)
    return _dot(a1, b_bf16) + _dot(a2, b_bf16) + _dot(a3, b_bf16)


def _dot_hi(a, b):
    a1 = a.astype(BF16)
    a2 = (a - a1.astype(F32)).astype(BF16)
    b1 = b.astype(BF16)
    b2 = (b - b1.astype(F32)).astype(BF16)
    return _dot(a1, b1) + _dot(a1, b2) + _dot(a2, b1)


def _sigmoid(x):
    return 1.0 / (1.0 + jnp.exp(-x))


def _silu(x):
    return x * _sigmoid(x)


def _modulate(x, g, shift, scale):
    ms = jnp.mean(x * x, axis=-1, keepdims=True)
    return x * lax.rsqrt(ms + EPS) * g * (1.0 + scale) + shift


def _adaln_kernel(s_ref, w_ref, b_ref, o_ref):
    s = _silu(s_ref[...])
    o_ref[...] = _dot_hi(s, w_ref[0]) + b_ref[0]


def _adaln(rows, w, b, layer):
    depth, d, n = w.shape
    tn = n // 4
    return pl.pallas_call(
        _adaln_kernel,
        grid=(n // tn,),
        in_specs=[pl.BlockSpec((8, d), lambda j: (0, 0)),
                  pl.BlockSpec((1, d, tn), lambda j: (layer, 0, j)),
                  pl.BlockSpec((1, 1, tn), lambda j: (layer, 0, j))],
        out_specs=pl.BlockSpec((8, tn), lambda j: (0, j)),
        out_shape=jax.ShapeDtypeStruct((8, n), F32),
        compiler_params=_cparams("arbitrary"),
        name="adaln",
    )(rows, w, b.reshape(depth, 1, n))


def _hidden_operands(h, nct):
    if isinstance(h, tuple):
        hc, hl = h
        d = hc.shape[-1]
        specs = [pl.BlockSpec((1, TILE, d), lambda b, i: (b, jnp.minimum(i, nct - 1), 0)),
                 pl.BlockSpec((1, TILE, d), lambda b, i: (b, jnp.maximum(i - nct, 0), 0))]
        return [hc, hl], specs, hc.shape[1] + hl.shape[1]
    return [h], [pl.BlockSpec((1, TILE, h.shape[-1]), lambda b, i: (b, i, 0))], h.shape[1]


def _hidden_tile(h_refs, nct):
    if len(h_refs) == 2:
        return jnp.where(pl.program_id(1) < nct, h_refs[0][0], h_refs[1][0])
    return h_refs[0][0]


def _modproj_kernel(*refs, n_h, nct):
    mod_ref, g_ref, w_ref, o_ref = refs[n_h:]
    mod = mod_ref[0, 0]
    a = _modulate(_hidden_tile(refs[:n_h], nct), g_ref[...], mod[0:1], mod[1:2])
    o_ref[0] = _dot(a.astype(BF16), w_ref[...]).astype(o_ref.dtype)


def _modproj(h, mod, g, w_bf16, nct, out_dtype):
    h_ops, h_specs, t = _hidden_operands(h, nct)
    bsz, _, d = h_ops[0].shape
    n = w_bf16.shape[1]
    return pl.pallas_call(
        functools.partial(_modproj_kernel, n_h=len(h_ops), nct=nct),
        grid=(bsz, t // TILE),
        in_specs=h_specs + [pl.BlockSpec((1, 1, 6, d), lambda b, i: (b, jnp.where(i < nct, 0, 1), 0, 0)),
                            pl.BlockSpec((1, d), lambda b, i: (0, 0)),
                            pl.BlockSpec((d, n), lambda b, i: (0, 0))],
        out_specs=pl.BlockSpec((1, TILE, n), lambda b, i: (b, i, 0)),
        out_shape=jax.ShapeDtypeStruct((bsz, t, n), out_dtype),
        compiler_params=_cparams("arbitrary", "arbitrary"),
        name="modproj",
    )(*h_ops, mod, g.reshape(1, d), w_bf16)


def _router_kernel(h_ref, mod_ref, g_ref, wr_ref, a_ref, aff_ref, afft_ref, *, n_exp):
    mod = mod_ref[0, 0]
    a = _modulate(h_ref[0], g_ref[...], mod[3:4], mod[4:5])
    a_ref[0] = a.astype(BF16)
    logits = _dot_hi(a, wr_ref[...])
    lane = lax.broadcasted_iota(I32, logits.shape, 1)
    logits = jnp.where(lane < n_exp, logits, -jnp.inf)
    p = jnp.exp(logits - jnp.max(logits, axis=-1, keepdims=True))
    aff = p / jnp.sum(p, axis=-1, keepdims=True)
    aff_ref[0] = aff
    afft_ref[0] = aff.T


def _router(h, mod, g, w_router, nct):
    bsz, t, d = h.shape
    n_exp = w_router.shape[1]
    wr = jnp.zeros((d, LANES), F32).at[:, :n_exp].set(w_router)
    return pl.pallas_call(
        functools.partial(_router_kernel, n_exp=n_exp),
        grid=(bsz, t // TILE),
        in_specs=[pl.BlockSpec((1, TILE, d), lambda b, i: (b, i, 0)),
                  pl.BlockSpec((1, 1, 6, d), lambda b, i: (b, jnp.where(i < nct, 0, 1), 0, 0)),
                  pl.BlockSpec((1, d), lambda b, i: (0, 0)),
                  pl.BlockSpec((d, LANES), lambda b, i: (0, 0))],
        out_specs=[pl.BlockSpec((1, TILE, d), lambda b, i: (b, i, 0)),
                   pl.BlockSpec((1, TILE, LANES), lambda b, i: (b, i, 0)),
                   pl.BlockSpec((1, LANES, TILE), lambda b, i: (b, 0, i))],
        out_shape=[jax.ShapeDtypeStruct((bsz, t, d), BF16),
                   jax.ShapeDtypeStruct((bsz, t, LANES), F32),
                   jax.ShapeDtypeStruct((bsz, LANES, t), F32)],
        compiler_params=_cparams("arbitrary", "arbitrary"),
        name="router",
    )(h, mod, g.reshape(1, d), wr)


def _select_kernel(afft_ref, pos_ref, post_ref, cb_ref, cnt_ref, sel_ref, bucket_ref, *, n_exp, n_ctx, k_ctx, k_lat, t):
    tok = lax.broadcasted_iota(I32, (n_exp, t), 1)
    is_ctx = tok < n_ctx
    min_normal = 0x00800000

    def choose(in_set, k):
        def vals():
            return jnp.where(in_set, afft_ref[0][:n_exp], -1.0)

        def bit_step(it, thr):
            cand = thr | jnp.left_shift(jnp.int32(1), 30 - it)
            cnt = jnp.sum(jnp.where(vals() >= pltpu.bitcast(cand, F32), 1, 0), axis=1, keepdims=True)
            return jnp.where(cnt >= k, cand, thr)
        thr = lax.fori_loop(0, 31, bit_step, jnp.zeros((n_exp, 1), I32))
        lo = pltpu.bitcast(thr, F32)
        hi = pltpu.bitcast(jnp.maximum(thr + 1, min_normal), F32)
        v = vals()
        above = v >= hi
        sel_ref[...] = jnp.where(above, 1.0, sel_ref[...])
        bucket_ref[...] = jnp.where((v >= lo) & jnp.logical_not(above), 1.0, 0.0)
        need0 = k - jnp.sum(jnp.where(above, 1, 0), axis=1, keepdims=True)

        def peel(need):
            in_bucket = bucket_ref[...] > 0.0
            v = vals()
            top = jnp.max(jnp.where(in_bucket, v, -1.0), axis=1, keepdims=True)
            first = jnp.min(jnp.where(in_bucket & (v == top), tok, t), axis=1, keepdims=True)
            pick = (tok == first) & (need > 0)
            sel_ref[...] = jnp.where(pick, 1.0, sel_ref[...])
            bucket_ref[...] = jnp.where(pick, 0.0, bucket_ref[...])
            return need - 1
        lax.while_loop(lambda need: jnp.max(need) > 0, peel, need0)

    sel_ref[...] = jnp.zeros_like(sel_ref)
    choose(jnp.logical_not(is_ctx), k_lat)
    if k_ctx > 0:
        choose(is_ctx, k_ctx)

    rr = lax.broadcasted_iota(I32, (LANES, LANES), 0)
    cc = lax.broadcasted_iota(I32, (LANES, LANES), 1)
    upper = jnp.where(rr <= cc, 1.0, 0.0).astype(BF16)
    carry = jnp.zeros((n_exp, 1), F32)
    for blk in range(t // LANES):
        c = _dot(sel_ref[:, blk * LANES:(blk + 1) * LANES].astype(BF16), upper) + carry
        cnt_ref[:, blk * LANES:(blk + 1) * LANES] = c
        carry = c[:, LANES - 1:LANES]
    posinc = cnt_ref[...].astype(I32)
    pos = jnp.where(sel_ref[...] > 0.0, posinc - 1, -1)
    pos_ref[0] = pos

    lane = lax.broadcasted_iota(I32, (n_exp, LANES), 1)
    cb = jnp.zeros((n_exp, LANES), I32)
    for i in range(1, t // SUBT + 1):
        cb = jnp.where(lane == i, posinc[:, i * SUBT - 1:i * SUBT], cb)
    cb_ref[0] = cb

    padded = jnp.concatenate([pos.astype(F32), jnp.full((LANES - n_exp, t), -1.0, F32)], axis=0)
    for i in range(t // TILE):
        post_ref[0, i * TILE:(i + 1) * TILE, :] = padded[:, i * TILE:(i + 1) * TILE].T.astype(I32)


def _select(afft, n_exp, n_ctx, k_ctx, k_lat):
    bsz, _, t = afft.shape
    assert n_ctx % LANES == 0 and t % TILE == 0 and t // SUBT < LANES
    return pl.pallas_call(
        functools.partial(_select_kernel, n_exp=n_exp, n_ctx=n_ctx, k_ctx=k_ctx, k_lat=k_lat, t=t),
        grid=(bsz,),
        in_specs=[pl.BlockSpec((1, LANES, t), lambda b: (b, 0, 0))],
        out_specs=[pl.BlockSpec((1, n_exp, t), lambda b: (b, 0, 0)),
                   pl.BlockSpec((1, t, LANES), lambda b: (b, 0, 0)),
                   pl.BlockSpec((1, n_exp, LANES), lambda b: (b, 0, 0))],
        out_shape=[jax.ShapeDtypeStruct((bsz, n_exp, t), I32),
                   jax.ShapeDtypeStruct((bsz, t, LANES), I32),
                   jax.ShapeDtypeStruct((bsz, n_exp, LANES), I32)],
        scratch_shapes=[pltpu.VMEM((n_exp, t), F32)] * 3,
        compiler_params=_cparams("arbitrary"),
        name="select",
    )(afft)


def _ffn_rows(cap_tot):
    for rc in range(min(cap_tot, 512), 7, -8):
        if cap_tot % rc == 0:
            return rc
    raise ValueError(cap_tot)


def _expert_kernel(cb_ref, pos_ref, a_ref, wg_ref, wu_ref, wd_ref, y_ref, x_ref, *, n_exp, nsub, sub_per, cap_tot):
    e, b, ch = pl.program_id(0), pl.program_id(1), pl.program_id(2)

    @pl.when(ch == 0)
    def _():
        x_ref[...] = jnp.zeros_like(x_ref)

    base = (b * n_exp + e) * (nsub + 1) + ch * sub_per
    row = lax.broadcasted_iota(I32, (WIN_X, SUBT), 0)

    def gather(j, carry):
        start = cb_ref[base + j]
        start_al = pl.multiple_of((start // 8) * 8, 8)
        rel = pos_ref[0, 0, j] - start_al
        onehot = jnp.where(row == rel, 1.0, 0.0).astype(BF16)
        tok0 = pl.multiple_of(j * SUBT, SUBT)
        x_ref[pl.ds(start_al, WIN_X), :] += _dot(onehot, a_ref[0, pl.ds(tok0, SUBT), :])
        return carry
    lax.fori_loop(0, sub_per, gather, 0, unroll=4)

    @pl.when(ch == pl.num_programs(2) - 1)
    def _():
        rc = _ffn_rows(cap_tot)
        wg = wg_ref[0, 0].astype(BF16)
        wu = wu_ref[0, 0].astype(BF16)
        wd = wd_ref[0, 0].astype(BF16)
        for r0 in range(0, cap_tot, rc):
            xc = x_ref[r0:r0 + rc, :].astype(BF16)
            hid = _silu(_dot(xc, wg)) * _dot(xc, wu)
            y_ref[0, 0, r0:r0 + rc, :] = _dot(hid.astype(BF16), wd).astype(y_ref.dtype)
        y_ref[0, 0, cap_tot:, :] = jnp.zeros((y_ref.shape[2] - cap_tot, y_ref.shape[-1]), y_ref.dtype)


def _experts(cb_flat, pos, a, wg, wu, wd, layer, cap_tot):
    bsz, n_exp, t = pos.shape
    d = a.shape[-1]
    ff = wg.shape[-1]
    nsub = t // SUBT
    ntiles = t // TILE
    chunk = TILE * max(m for m in range(1, 13) if ntiles % m == 0)
    sub_per = chunk // SUBT
    assert cap_tot % 16 == 0
    rows_x = cap_tot + WIN_X
    rows_y = cap_tot + WIN_Y
    pos5 = pos.reshape(bsz, n_exp, nsub, 1, SUBT)
    grid_spec = pltpu.PrefetchScalarGridSpec(
        num_scalar_prefetch=1,
        grid=(n_exp, bsz, t // chunk),
        in_specs=[pl.BlockSpec((1, 1, sub_per, 1, SUBT), lambda e, b, c, cb: (b, e, c, 0, 0)),
                  pl.BlockSpec((1, chunk, d), lambda e, b, c, cb: (b, c, 0)),
                  pl.BlockSpec((1, 1, d, ff), lambda e, b, c, cb: (layer, e, 0, 0)),
                  pl.BlockSpec((1, 1, d, ff), lambda e, b, c, cb: (layer, e, 0, 0)),
                  pl.BlockSpec((1, 1, ff, d), lambda e, b, c, cb: (layer, e, 0, 0))],
        out_specs=pl.BlockSpec((1, 1, rows_y, d), lambda e, b, c, cb: (b, e, 0, 0)),
        scratch_shapes=[pltpu.VMEM((rows_x, d), F32)],
    )
    return pl.pallas_call(
        functools.partial(_expert_kernel, n_exp=n_exp, nsub=nsub, sub_per=sub_per, cap_tot=cap_tot),
        grid_spec=grid_spec,
        out_shape=jax.ShapeDtypeStruct((bsz, n_exp, rows_y, d), BF16),
        compiler_params=_cparams("arbitrary", "arbitrary", "arbitrary"),
        name="experts",
    )(cb_flat, pos5, a, wg, wu, wd)


def _combine(cb_flat, h, mod, aff, post, y, g_final, nct, final_norm):
    bsz, t, d = h.shape
    n_exp = y.shape[1]
    ntiles = t // TILE
    first = nct if final_norm else 0
    n_win = n_exp * (TILE // SUBT)
    grid_spec = pltpu.PrefetchScalarGridSpec(
        num_scalar_prefetch=1,
        grid=(bsz, ntiles - first),
        in_specs=[pl.BlockSpec((1, TILE, d), lambda b, i, cb: (b, i + first, 0)),
                  pl.BlockSpec((1, 1, 6, d), lambda b, i, cb: (b, jnp.where(i + first < nct, 0, 1), 0, 0)),
                  pl.BlockSpec((1, TILE, LANES), lambda b, i, cb: (b, i + first, 0)),
                  pl.BlockSpec((1, TILE, LANES), lambda b, i, cb: (b, i + first, 0)),
                  pl.BlockSpec(memory_space=pl.ANY),
                  pl.BlockSpec((1, d), lambda b, i, cb: (0, 0))],
        out_specs=pl.BlockSpec((1, TILE, d), lambda b, i, cb: (b, i, 0)),
        scratch_shapes=[pltpu.VMEM((2, n_win, WIN_Y, d), y.dtype), pltpu.SemaphoreType.DMA((2, n_win))],
    )
    return pl.pallas_call(
        functools.partial(_combine_kernel, n_exp=n_exp, nsub=t // SUBT, first=first, final_norm=final_norm),
        grid_spec=grid_spec,
        out_shape=jax.ShapeDtypeStruct((bsz, t - first * TILE, d), F32),
        compiler_params=_cparams("arbitrary", "arbitrary"),
        name="combine",
    )(cb_flat, h, mod, aff, post, y, g_final.reshape(1, d))


def _combine_kernel(cb_ref, h_ref, mod_ref, aff_ref, post_ref, y_hbm, g_ref, o_ref, ybuf, sem,
                    *, n_exp, nsub, first, final_norm):
    n_i = pl.num_programs(1)
    step = pl.program_id(0) * n_i + pl.program_id(1)
    n_steps = pl.num_programs(0) * n_i
    per_tile = TILE // SUBT

    def window(at_step, sub, e):
        b = at_step // n_i
        i = at_step - b * n_i + first
        start = cb_ref[(b * n_exp + e) * (nsub + 1) + i * per_tile + sub]
        start_al = pl.multiple_of((start // 16) * 16, 16)
        buf, slot = at_step % 2, sub * n_exp + e
        copy = pltpu.make_async_copy(y_hbm.at[b, e, pl.ds(start_al, WIN_Y), :], ybuf.at[buf, slot],
                                     sem.at[buf, slot])
        return copy, start_al

    def start_windows(at_step):
        for sub in range(per_tile):
            for e in range(n_exp):
                window(at_step, sub, e)[0].start()

    @pl.when(step == 0)
    def _():
        start_windows(step)

    @pl.when(step + 1 < n_steps)
    def _():
        start_windows(step + 1)

    lane = lax.broadcasted_iota(I32, (SUBT, WIN_Y), 1)
    mod = mod_ref[0, 0]
    for sub in range(per_tile):
        rows = slice(sub * SUBT, (sub + 1) * SUBT)
        post = post_ref[0, rows, :]
        aff = aff_ref[0, rows, :]
        acc = jnp.zeros((SUBT, o_ref.shape[-1]), F32)
        starts = []
        for e in range(n_exp):
            copy, start_al = window(step, sub, e)
            copy.wait()
            starts.append(start_al)
        for e, start_al in enumerate(starts):
            rel = post[:, e:e + 1] - start_al
            onehot = jnp.where(lane == rel, 1.0, 0.0).astype(BF16)
            acc = acc + aff[:, e:e + 1] * _dot(onehot, ybuf[step % 2, sub * n_exp + e])
        out = h_ref[0, rows, :] + mod[5:6] * acc
        if final_norm:
            ms = jnp.mean(out * out, axis=-1, keepdims=True)
            out = out * lax.rsqrt(ms + EPS) * g_ref[...]
        o_ref[0, rows, :] = out


def _moe(h, mod, ln2_g, w_router, wg, wu, wd, layer, g_final, *, n_ctx, route_ctx, final_norm):
    bsz, t, d = h.shape
    n_exp = w_router.shape[1]
    nct = n_ctx // TILE
    n_lat = t - n_ctx
    k_lat = max(1, EC_CAPACITY * n_lat // n_exp)
    k_ctx = max(1, EC_CAPACITY * n_ctx // n_exp) if route_ctx else 0
    a, aff, afft = _router(h, mod, ln2_g, w_router, nct)
    pos, post, cb = _select(afft, n_exp, n_ctx, k_ctx, k_lat)
    cb_flat = cb[:, :, :t // SUBT + 1].reshape(-1)
    y = _experts(cb_flat, pos, a, wg, wu, wd, layer, k_ctx + k_lat)
    return _combine(cb_flat, h, mod, aff, post, y, g_final, nct, final_norm)


HEAD_SHIFT = int(math.log2(HEAD_DIM))


def _same_head(rows, cols):
    r = lax.shift_right_logical(lax.broadcasted_iota(I32, (rows, cols), 0), HEAD_SHIFT)
    c = lax.shift_right_logical(lax.broadcasted_iota(I32, (rows, cols), 1), HEAD_SHIFT)
    return jnp.where(r == c, 1.0, 0.0)


def _head_sum(x2):
    n = x2.shape[-1]
    return _dot_exact_rhs(x2, _same_head(n, n).astype(BF16))


def _outproj_even_kernel(*refs, n_h, nct, a_dim):
    mod_ref, of_ref, ob_ref, z_ref, at_ref, g_ref, w_ref, o_ref = refs[n_h:]
    mod = mod_ref[0, 0]
    o = of_ref[0] + ob_ref[0]
    ms = _head_sum(o * o) * (1.0 / HEAD_DIM)
    ya = o * lax.rsqrt(ms + EPS) * g_ref[...] * _silu(z_ref[0])
    y = _dot(ya.astype(BF16), w_ref[:a_dim, :]) + _dot(at_ref[0], w_ref[a_dim:, :])
    o_ref[0] = _hidden_tile(refs[:n_h], nct) + mod[2:3] * y


def _outproj_even(h, mod, o_delta, p, o_attn, out_norm_g, w_bf16, nct, *, a_dim):
    h_ops, h_specs, t = _hidden_operands(h, nct)
    bsz, _, d = h_ops[0].shape
    o_f, o_b = o_delta
    g = jnp.tile(out_norm_g, a_dim // HEAD_DIM).reshape(1, a_dim)
    tok = lambda b, i: (b, i, 0)
    return pl.pallas_call(
        functools.partial(_outproj_even_kernel, n_h=len(h_ops), nct=nct, a_dim=a_dim),
        grid=(bsz, t // TILE),
        in_specs=h_specs + [pl.BlockSpec((1, 1, 6, d), lambda b, i: (b, jnp.where(i < nct, 0, 1), 0, 0)),
                            pl.BlockSpec((1, TILE, a_dim), tok),
                            pl.BlockSpec((1, TILE, a_dim), tok),
                            pl.BlockSpec((1, TILE, a_dim), lambda b, i: (b, i, 3)),
                            pl.BlockSpec((1, TILE, a_dim), tok),
                            pl.BlockSpec((1, a_dim), lambda b, i: (0, 0)),
                            pl.BlockSpec((d, d), lambda b, i: (0, 0))],
        out_specs=pl.BlockSpec((1, TILE, d), tok),
        out_shape=jax.ShapeDtypeStruct((bsz, t, d), F32),
        compiler_params=_cparams("arbitrary", "arbitrary"),
        name="outproj_even",
    )(*h_ops, mod, o_f, o_b, p, o_attn, g, w_bf16)


def _outproj_odd_kernel(h_ref, mod_ref, a_ref, w_ref, o_ref):
    mod = mod_ref[0, 0]
    o_ref[0] = h_ref[0] + mod[2:3] * _dot(a_ref[0], w_ref[...])


def _outproj_odd(h, mod, o_na, w_bf16, nct):
    bsz, t, d = h.shape
    n_lat = o_na.shape[1]
    return pl.pallas_call(
        _outproj_odd_kernel,
        grid=(bsz, n_lat // TILE),
        in_specs=[pl.BlockSpec((1, TILE, d), lambda b, i: (b, i + nct, 0)),
                  pl.BlockSpec((1, 1, 6, d), lambda b, i: (b, 1, 0, 0)),
                  pl.BlockSpec((1, TILE, d), lambda b, i: (b, i, 0)),
                  pl.BlockSpec((d, d), lambda b, i: (0, 0))],
        out_specs=pl.BlockSpec((1, TILE, d), lambda b, i: (b, i, 0)),
        out_shape=jax.ShapeDtypeStruct((bsz, n_lat, d), F32),
        compiler_params=_cparams("arbitrary", "arbitrary"),
        name="outproj_odd",
    )(h, mod, o_na, w_bf16)


def _rope_tables(n_ctx, n_lat):
    tt = jnp.arange(n_lat)
    n_freq = HEAD_DIM // 4
    inv = ROPE_THETA ** (-jnp.arange(n_freq, dtype=F32) / n_freq)
    ang = jnp.concatenate([(tt // GRID_W).astype(F32)[:, None] * inv, (tt % GRID_W).astype(F32)[:, None] * inv], -1)
    cos = jnp.concatenate([jnp.ones((n_ctx, 2 * n_freq), F32), jnp.cos(ang)], axis=0)
    sin = jnp.concatenate([jnp.zeros((n_ctx, 2 * n_freq), F32), jnp.sin(ang)], axis=0)
    reps = LANES // HEAD_DIM
    return jnp.tile(cos, (1, 2 * reps)), jnp.tile(jnp.concatenate([-sin, sin], axis=1), (1, reps))


def _rope(x, cos, sin):
    n = x.shape[-1]
    half = HEAD_DIM // 2
    lane = lax.broadcasted_iota(I32, x.shape, 1)
    first = (lane & (HEAD_DIM - 1)) < half
    partner = jnp.where(first, pltpu.roll(x, n - half, 1), pltpu.roll(x, half, 1))
    reps = n // LANES
    cos_n = jnp.concatenate([cos] * reps, axis=1) if reps > 1 else cos
    sin_n = jnp.concatenate([sin] * reps, axis=1) if reps > 1 else sin
    return x * cos_n + partner * sin_n


def _gqa_prep_kernel(q_ref, k_ref, v_ref, cos_ref, sin_ref, gq_ref, gk_ref, qo_ref, ko_ref, vo_ref):
    cos, sin = cos_ref[...], sin_ref[...]
    q = q_ref[0]
    q = q * lax.rsqrt(_head_sum(q * q) * (1.0 / HEAD_DIM) + EPS) * gq_ref[...] * (HEAD_DIM ** -0.5 * LOG2_E)
    qo_ref[0] = _rope(q, cos, sin).astype(BF16)
    k = k_ref[0]
    k = k * lax.rsqrt(_head_sum(k * k) * (1.0 / HEAD_DIM) + EPS) * gk_ref[...]
    k = _rope(k, cos, sin).astype(BF16)
    v = v_ref[0].astype(BF16)
    for hh in range(k.shape[-1] // HEAD_DIM):
        ko_ref[0, hh] = k[:, hh * HEAD_DIM:(hh + 1) * HEAD_DIM]
        vo_ref[0, hh] = v[:, hh * HEAD_DIM:(hh + 1) * HEAD_DIM]


def _gqa_prep(p, cos, sin, q_norm_g, k_norm_g, *, a_dim, kv_dim):
    bsz, t, _ = p.shape
    q_dim = a_dim
    kvh = kv_dim // HEAD_DIM
    assert kv_dim == LANES
    c_q = (4 * a_dim) // q_dim
    c_k = (4 * a_dim + q_dim) // kv_dim
    gq = jnp.tile(q_norm_g, q_dim // HEAD_DIM).reshape(1, q_dim)
    gk = jnp.tile(k_norm_g, kvh).reshape(1, kv_dim)
    return pl.pallas_call(
        _gqa_prep_kernel,
        grid=(bsz, t // TILE),
        in_specs=[pl.BlockSpec((1, TILE, q_dim), lambda b, i: (b, i, c_q)),
                  pl.BlockSpec((1, TILE, kv_dim), lambda b, i: (b, i, c_k)),
                  pl.BlockSpec((1, TILE, kv_dim), lambda b, i: (b, i, c_k + 1)),
                  pl.BlockSpec((TILE, LANES), lambda b, i: (i, 0)),
                  pl.BlockSpec((TILE, LANES), lambda b, i: (i, 0)),
                  pl.BlockSpec((1, q_dim), lambda b, i: (0, 0)),
                  pl.BlockSpec((1, kv_dim), lambda b, i: (0, 0))],
        out_specs=[pl.BlockSpec((1, TILE, q_dim), lambda b, i: (b, i, 0)),
                   pl.BlockSpec((1, kvh, TILE, HEAD_DIM), lambda b, i: (b, 0, i, 0)),
                   pl.BlockSpec((1, kvh, TILE, HEAD_DIM), lambda b, i: (b, 0, i, 0))],
        out_shape=[jax.ShapeDtypeStruct((bsz, t, q_dim), BF16),
                   jax.ShapeDtypeStruct((bsz, kvh, t, HEAD_DIM), BF16),
                   jax.ShapeDtypeStruct((bsz, kvh, t, HEAD_DIM), BF16)],
        compiler_params=_cparams("arbitrary", "arbitrary"),
        name="gqa_prep",
    )(p, p, p, cos, sin, gq, gk)


SUB_K = 256
Q_SLAB = 256


def _gqa_kernel(q_ref, k_ref, v_ref, o_ref, qs_ref, m_ref, l_ref, acc_ref, *, n_ctx, nct, tq, tk, n_heads, kvh):
    i, kv = pl.program_id(1), pl.program_id(2)
    total = n_heads * tq
    per_kv = total // kvh

    @pl.when(kv == 0)
    def _():
        m_ref[...] = jnp.full_like(m_ref, -jnp.inf)
        l_ref[...] = jnp.zeros_like(l_ref)
        acc_ref[...] = jnp.zeros_like(acc_ref)
        q = q_ref[0]
        for hh in range(n_heads):
            qs_ref[hh * tq:(hh + 1) * tq, :] = q[:, hh * HEAD_DIM:(hh + 1) * HEAD_DIM]

    def attend(n_keys, n_valid):
        edges = ([0, Q_SLAB // 2] + list(range(Q_SLAB + Q_SLAB // 2, total - Q_SLAB // 2, Q_SLAB))
                 + [total - Q_SLAB // 2, total])
        edges = sorted(set(edges) | set(range(0, total + 1, per_kv)))
        slabs = [slice(a, b) for a, b in zip(edges[:-1], edges[1:])]
        keys = lambda rows: k_ref[0, rows.start // per_kv, :n_keys, :]
        vals = lambda rows: v_ref[0, rows.start // per_kv, :n_keys, :]
        scores = {0: _dot_t(qs_ref[slabs[0], :], keys(slabs[0]))}
        for idx, rows in enumerate(slabs):
            if idx + 1 < len(slabs):
                scores[idx + 1] = _dot_t(qs_ref[slabs[idx + 1], :], keys(slabs[idx + 1]))
            s = scores.pop(idx)
            if n_valid is not None:
                s = jnp.where(lax.broadcasted_iota(I32, s.shape, 1) < n_valid, s, -jnp.inf)
            m_old = m_ref[rows, :]
            m_new = jnp.maximum(m_old, jnp.max(s, axis=-1, keepdims=True))
            alpha = jnp.exp2(m_old - m_new)
            p = jnp.exp2(s - m_new)
            l_ref[rows, :] = alpha * l_ref[rows, :] + jnp.sum(p, axis=-1, keepdims=True)
            acc_ref[rows, :] = alpha * acc_ref[rows, :] + _dot(p.astype(BF16), vals(rows))
            m_ref[rows, :] = m_new

    @pl.when(i >= nct)
    def _():
        attend(tk, None)

    @pl.when((i < nct) & (kv == 0))
    def _():
        n_keys = -(-n_ctx // SUB_K) * SUB_K
        attend(n_keys, n_ctx if n_keys != n_ctx else None)

    @pl.when(kv == pl.num_programs(2) - 1)
    def _():
        out = acc_ref[...] / l_ref[...]
        for hh in range(n_heads):
            o_ref[0, :, hh * HEAD_DIM:(hh + 1) * HEAD_DIM] = out[hh * tq:(hh + 1) * tq, :].astype(o_ref.dtype)


def _gqa_attention(q, k, v, *, n_ctx):
    bsz, t, q_dim = q.shape
    kvh = k.shape[1]
    n_heads = q_dim // HEAD_DIM
    tq = TILE
    nt = t // TILE
    tk = TILE * max(m for m in range(1, 13) if nt % m == 0)
    assert n_ctx <= tk and tk % SUB_K == 0 and (n_heads // kvh * tq) % Q_SLAB == 0
    return pl.pallas_call(
        functools.partial(_gqa_kernel, n_ctx=n_ctx, nct=n_ctx // tq, tq=tq, tk=tk, n_heads=n_heads, kvh=kvh),
        grid=(bsz, t // tq, t // tk),
        in_specs=[pl.BlockSpec((1, tq, q_dim), lambda b, i, j: (b, i, 0)),
                  pl.BlockSpec((1, kvh, tk, HEAD_DIM), lambda b, i, j: (b, 0, j, 0)),
                  pl.BlockSpec((1, kvh, tk, HEAD_DIM), lambda b, i, j: (b, 0, j, 0))],
        out_specs=pl.BlockSpec((1, tq, q_dim), lambda b, i, j: (b, i, 0)),
        out_shape=jax.ShapeDtypeStruct((bsz, t, q_dim), BF16),
        scratch_shapes=[pltpu.VMEM((n_heads * tq, HEAD_DIM), BF16),
                        pltpu.VMEM((n_heads * tq, 1), F32),
                        pltpu.VMEM((n_heads * tq, 1), F32),
                        pltpu.VMEM((n_heads * tq, HEAD_DIM), F32)],
        compiler_params=_cparams("arbitrary", "arbitrary", "arbitrary"),
        name="gqa_attention",
    )(q, k, v)


def _natten_bias(rpb):
    n_heads, n_off, n_rel = rpb.shape
    col = np.arange(GRID_W)
    col_start = np.clip(col - NA_COLS // 2, 0, GRID_W - NA_COLS)
    valid = (col[None, :] >= col_start[:, None]) & (col[None, :] < col_start[:, None] + NA_COLS)
    rel = col[None, :] - col[:, None] + NA_COLS - 1
    onehot = ((rel[None] == np.arange(n_rel)[:, None, None]) & valid[None]).astype(np.float32)
    tab = jnp.einsum('hrk,kcd->hrcd', rpb, jnp.asarray(onehot), precision=lax.Precision.HIGHEST)
    tab = jnp.where(valid[None, None], tab, -jnp.inf)
    pairs = jnp.concatenate([tab[:, :-1], tab[:, 1:]], axis=-1)
    return pairs.reshape(n_heads // 2, 2, n_off - 1, GRID_W, 2 * GRID_W).astype(F32)


def _natten_kernel(*refs, n_ctx, n_rows, rblk, n_q):
    q_refs = refs[:n_q]
    k_ref, v_ref, bias_ref, o_ref = refs[n_q:]
    per_q = TILE // GRID_W
    rb = pl.program_id(2)
    first_head = lax.broadcasted_iota(I32, (1, LANES), 1) < HEAD_DIM
    kc = k_ref[0, 0:n_ctx, :]
    vc = v_ref[0, 0:n_ctx, :]
    band = NA_ROWS * GRID_W
    qs, ks, vs, biases = [], [], [], []
    for rr in range(rblk):
        rg = rb * rblk + rr
        rs = jnp.clip(rg - NA_ROWS // 2, 0, n_rows - NA_ROWS)
        variant = rs - rg + (NA_ROWS - 1)
        start = pl.multiple_of(n_ctx + rs * GRID_W, GRID_W)
        keys = jnp.concatenate([kc, k_ref[0, pl.ds(start, band), :]], axis=0)
        vals = jnp.concatenate([vc, v_ref[0, pl.ds(start, band), :]], axis=0)
        q_lo = (rr % per_q) * GRID_W
        q = q_refs[rr // per_q][0, q_lo:q_lo + GRID_W, :] * (HEAD_DIM ** -0.5)
        for hh in range(2):
            qs.append(jnp.where(first_head if hh == 0 else jnp.logical_not(first_head), q, jnp.zeros_like(q)))
            ks.append(keys)
            vs.append(vals)
            biases.append(jnp.concatenate([bias_ref[0, hh, variant + j] for j in range(0, NA_ROWS, 2)], axis=1))
    batch = (((2,), (2,)), ((0,), (0,)))
    s = lax.dot_general(jnp.stack(qs), jnp.stack(ks), batch, preferred_element_type=F32)
    sc = s[:, :, :n_ctx]
    sb = s[:, :, n_ctx:] + jnp.stack(biases)
    m = jnp.maximum(jnp.max(sc, axis=-1, keepdims=True), jnp.max(sb, axis=-1, keepdims=True))
    pc = jnp.exp(sc - m)
    pb = jnp.exp(sb - m)
    l = jnp.sum(pc, axis=-1, keepdims=True) + jnp.sum(pb, axis=-1, keepdims=True)
    p = jnp.concatenate([pc, pb], axis=-1).astype(BF16)
    o = lax.dot_general(p, jnp.stack(vs), (((2,), (1,)), ((0,), (0,))), preferred_element_type=F32) / l
    for rr in range(rblk):
        o_ref[0, rr * GRID_W:(rr + 1) * GRID_W, :] = jnp.where(first_head, o[2 * rr], o[2 * rr + 1]).astype(o_ref.dtype)


def _natten(p, bias, *, n_ctx, d):
    bsz, t, _ = p.shape
    n_lat = t - n_ctx
    n_rows = n_lat // GRID_W
    n_q = 2 if n_lat % (2 * TILE) == 0 else 1
    rblk = n_q * TILE // GRID_W
    npairs = d // LANES
    nct = n_ctx // TILE
    assert n_rows >= NA_ROWS and n_ctx % TILE == 0
    return pl.pallas_call(
        functools.partial(_natten_kernel, n_ctx=n_ctx, n_rows=n_rows, rblk=rblk, n_q=n_q),
        grid=(bsz, npairs, n_lat // (n_q * TILE)),
        in_specs=[pl.BlockSpec((1, TILE, LANES), lambda b, hp, r, j=j: (b, nct + n_q * r + j, hp)) for j in range(n_q)] + [
                  pl.BlockSpec((1, t, LANES), lambda b, hp, r: (b, 0, npairs + hp)),
                  pl.BlockSpec((1, t, LANES), lambda b, hp, r: (b, 0, 2 * npairs + hp)),
                  pl.BlockSpec((1, 2, 2 * NA_ROWS - 2, GRID_W, 2 * GRID_W), lambda b, hp, r: (hp, 0, 0, 0, 0))],
        out_specs=pl.BlockSpec((1, n_q * TILE, LANES), lambda b, hp, r: (b, r, hp)),
        out_shape=jax.ShapeDtypeStruct((bsz, n_lat, d), BF16),
        compiler_params=_cparams("arbitrary", "arbitrary", "arbitrary"),
        name="natten",
    )(*([p] * n_q), p, p, bias)


def _dot_exact_lhs(a_bf16, b):
    b1, b2, b3 = _split3(b)
    return _dot(a_bf16, b1) + _dot(a_bf16, b2) + _dot(a_bf16, b3)


def _delta_prep_kernel(x_ref, prev_ref, next_ref, gate_ref, w_ref, alog_ref, dt_ref, xh_ref, kt_ref, gn_ref, gnt_ref,
                       xs_ref,
                       *, nct, a_dim):
    i = pl.program_id(1)
    last = pl.num_programs(1) - 1
    x = x_ref[0]
    rows = x.shape[0]
    xs_ref[0:8, :] = prev_ref[0]
    xs_ref[8:8 + rows, :] = x
    xs_ref[8 + rows:16 + rows, :] = next_ref[0]
    r = lax.broadcasted_iota(I32, (rows, 1), 0)
    at_start = (i == 0) | (i == nct)
    at_end = (i == nct - 1) | (i == last)
    xm = jnp.where((r == 0) & at_start, 0.0, xs_ref[7:7 + rows, :])
    xp = jnp.where((r == rows - 1) & at_end, 0.0, xs_ref[9:9 + rows, :])
    w = w_ref[...]
    y = _silu(w[0:1] * xm + w[1:2] * x + w[2:3] * xp)
    q, k, v = y[:, :a_dim], y[:, a_dim:2 * a_dim], y[:, 2 * a_dim:]
    nh = a_dim // HEAD_DIM
    q = (q * lax.rsqrt(_head_sum(q * q) + EPS) * (HEAD_DIM ** -0.5)).astype(BF16)
    k = k * lax.rsqrt(_head_sum(k * k) + EPS)
    k_t = k.T.astype(BF16)
    for hh in range(nh):
        for cc in range(rows // DELTA_CHUNK):
            kt_ref[0, hh, cc] = k_t[hh * HEAD_DIM:(hh + 1) * HEAD_DIM, cc * DELTA_CHUNK:(cc + 1) * DELTA_CHUNK]
    k = k.astype(BF16)
    v = v.astype(BF16)
    for kind, arr in enumerate((q, k, v)):
        for hh in range(nh):
            xh_ref[0, kind, hh] = arr[:, hh * HEAD_DIM:(hh + 1) * HEAD_DIM]

    g = gate_ref[0]
    z = g + dt_ref[...]
    softplus = jnp.maximum(z, 0.0) + jnp.log1p(jnp.exp(-jnp.abs(z)))
    log_a = -jnp.exp(alog_ref[...]) * softplus
    beta = _sigmoid(g)
    ri = lax.broadcasted_iota(I32, (rows, rows), 0)
    ci = lax.broadcasted_iota(I32, (rows, rows), 1)
    chunk_shift = int(math.log2(DELTA_CHUNK))
    same_chunk = lax.shift_right_logical(ri, chunk_shift) == lax.shift_right_logical(ci, chunk_shift)
    prefix = jnp.where(same_chunk & (ci <= ri), 1.0, 0.0).astype(BF16)
    suffix = jnp.where(same_chunk & (ci >= ri), 1.0, 0.0).astype(BF16)
    g_f = _dot_exact_lhs(prefix, log_a)
    g_b = _dot_exact_lhs(suffix, log_a)
    lane = lax.broadcasted_iota(I32, g.shape, 1)
    narrow = jnp.where(lane < nh, g_f, jnp.where(lane < 2 * nh, beta, jnp.where(lane < 3 * nh, g_b, beta)))
    gn_ref[0] = narrow
    gnt_ref[0] = narrow.T


def _delta_prep(p, conv_w, a_log, dt_bias, *, n_ctx, a_dim):
    bsz, t, _ = p.shape
    nh = a_dim // HEAD_DIM
    assert DELTA_CHUNK == HEAD_DIM and TILE % DELTA_CHUNK == 0 and 4 * nh <= LANES
    nct = n_ctx // TILE
    c_gate = (4 * a_dim + a_dim + 2 * LANES) // LANES
    alog = jnp.zeros((1, LANES), F32).at[0, :nh].set(a_log[0]).at[0, 2 * nh:3 * nh].set(a_log[1])
    dtb = jnp.zeros((1, LANES), F32).at[0, :nh].set(dt_bias[0]).at[0, 2 * nh:3 * nh].set(dt_bias[1])
    nblk8 = t // 8
    per = TILE // 8
    return pl.pallas_call(
        functools.partial(_delta_prep_kernel, nct=nct, a_dim=a_dim),
        grid=(bsz, t // TILE),
        in_specs=[pl.BlockSpec((1, TILE, 3 * a_dim), lambda b, i: (b, i, 0)),
                  pl.BlockSpec((1, 8, 3 * a_dim), lambda b, i: (b, jnp.maximum(i * per - 1, 0), 0)),
                  pl.BlockSpec((1, 8, 3 * a_dim), lambda b, i: (b, jnp.minimum((i + 1) * per, nblk8 - 1), 0)),
                  pl.BlockSpec((1, TILE, LANES), lambda b, i: (b, i, c_gate)),
                  pl.BlockSpec((3, 3 * a_dim), lambda b, i: (0, 0)),
                  pl.BlockSpec((1, LANES), lambda b, i: (0, 0)),
                  pl.BlockSpec((1, LANES), lambda b, i: (0, 0))],
        out_specs=[pl.BlockSpec((1, 3, nh, TILE, HEAD_DIM), lambda b, i: (b, 0, 0, i, 0)),
                   pl.BlockSpec((1, nh, TILE // DELTA_CHUNK, HEAD_DIM, DELTA_CHUNK), lambda b, i: (b, 0, i, 0, 0)),
                   pl.BlockSpec((1, TILE, LANES), lambda b, i: (b, i, 0)),
                   pl.BlockSpec((1, LANES, TILE), lambda b, i: (b, 0, i))],
        out_shape=[jax.ShapeDtypeStruct((bsz, 3, nh, t, HEAD_DIM), BF16),
                   jax.ShapeDtypeStruct((bsz, nh, t // DELTA_CHUNK, HEAD_DIM, DELTA_CHUNK), BF16),
                   jax.ShapeDtypeStruct((bsz, t, LANES), F32),
                   jax.ShapeDtypeStruct((bsz, LANES, t), F32)],
        scratch_shapes=[pltpu.VMEM((TILE + 16, 3 * a_dim), F32)],
        compiler_params=_cparams("arbitrary", "arbitrary"),
        name="delta_prep",
    )(p, p, p, p, conv_w, alog, dtb)


def _bdot(a, b, ca, cb, hi=False):
    dims = (((ca,), (cb,)), ((0,), (0,)))
    dot = lambda x, y: lax.dot_general(x, y, dims, preferred_element_type=F32)
    if not hi:
        return dot(a.astype(BF16), b.astype(BF16))
    a1 = a.astype(BF16)
    a2 = (a - a1.astype(F32)).astype(BF16)
    b1 = b.astype(BF16)
    b2 = (b - b1.astype(F32)).astype(BF16)
    return dot(jnp.concatenate([a1, a1, a2], axis=ca), jnp.concatenate([b1, b2, b1], axis=cb))


def _delta_heads(q, k, kt, v, g_col, beta_col, g_row, s, upper, out):
    nb, lc, _ = q.shape
    ri = lax.broadcasted_iota(I32, (nb, lc, lc), 1)
    ci = lax.broadcasted_iota(I32, (nb, lc, lc), 2)
    tri_incl = (ci >= ri) if upper else (ci <= ri)
    tri_strict = (ci > ri) if upper else (ci < ri)
    eye = jnp.where(ri == ci, 1.0, 0.0)
    k32, q32, v32 = k.astype(F32), q.astype(F32), v.astype(F32)
    eg = jnp.exp(g_col)
    g_last = g_col[:, 0:1] if upper else g_col[:, lc - 1:lc]
    gam = jnp.exp(jnp.where(tri_incl, g_col - g_row, -jnp.inf))
    kq = _bdot(jnp.concatenate([k, q], axis=1), kt, 2, 1)
    yield
    m = jnp.where(tri_strict, kq[:, :lc] * beta_col * gam, 0.0)
    a_qk = kq[:, lc:] * gam
    n = -m
    x = eye + n
    p = _bdot(n, n, 2, 1, hi=True)
    yield
    for _ in range(int(math.log2(lc)) - 2):
        px_pp = _bdot(p, jnp.concatenate([x, p], axis=2), 2, 1, hi=True)
        yield
        x, p = x + px_pp[:, :, :lc], px_pp[:, :, lc:]
    x = x + _bdot(p, x, 2, 1, hi=True)
    yield
    uw = _bdot(x, jnp.concatenate([v32 * beta_col, k32 * (beta_col * eg)], axis=2), 2, 1, hi=True)
    yield
    u, w = uw[:, :, :v.shape[2]], uw[:, :, v.shape[2]:]
    ws_qs = _bdot(jnp.concatenate([w, q32 * eg], axis=1), s, 2, 1)
    yield
    v_new = u - ws_qs[:, :lc]
    o = ws_qs[:, lc:] + _bdot(a_qk, v_new, 2, 1)
    kv = _bdot(kt.astype(F32) * jnp.exp(g_last - g_row), v_new, 2, 1)
    out.append((o, s * jnp.exp(g_last) + kv))


def _delta_scan_kernel(xf_ref, xb_ref, ktf_ref, ktb_ref, gf_ref, gb_ref, gtf_ref, gtb_ref, of_ref, ob_ref, s_ref, *, nh):
    c = pl.program_id(1)

    @pl.when(c == 0)
    def _():
        s_ref[...] = jnp.zeros_like(s_ref)

    results, stages = [], []
    dirs = ((xf_ref, ktf_ref, gf_ref, gtf_ref), (xb_ref, ktb_ref, gb_ref, gtb_ref))
    for d_i, (x_ref, kt_ref, g_ref, gt_ref) in enumerate(dirs):
        gn = g_ref[0]
        gt = gt_ref[0, 0]
        lanes = [2 * d_i * nh + hh for hh in range(nh)]
        g_col = jnp.stack([gn[:, l:l + 1] for l in lanes])
        b_col = jnp.stack([gn[:, l + nh:l + nh + 1] for l in lanes])
        g_row = jnp.stack([gt[l:l + 1, :] for l in lanes])
        out = []
        results.append(out)
        stages.append(_delta_heads(x_ref[0, 0], x_ref[0, 1], kt_ref[0, :, 0], x_ref[0, 2], g_col, b_col, g_row,
                                   s_ref[d_i * nh:(d_i + 1) * nh], d_i == 1, out))
    fwd, bwd = stages
    next(fwd)
    for _ in fwd:
        next(bwd, None)
    for _ in bwd:
        pass
    for d_i, o_ref in enumerate((of_ref, ob_ref)):
        o, s_new = results[d_i][0]
        s_ref[d_i * nh:(d_i + 1) * nh] = s_new
        for hh in range(nh):
            o_ref[0, :, hh * HEAD_DIM:(hh + 1) * HEAD_DIM] = o[hh]


def _delta_scan(xh, kt, gn, gnt, *, n_ctx):
    bsz, _, nh, t, _ = xh.shape
    lc = DELTA_CHUNK
    nc, ncc = t // lc, n_ctx // lc
    gnt4 = gnt.reshape(bsz, LANES, nc, lc).transpose(0, 2, 1, 3)
    rev = lambda c: jnp.where(c < ncc, ncc - 1 - c, nc - 1 - (c - ncc))
    return pl.pallas_call(
        functools.partial(_delta_scan_kernel, nh=nh),
        grid=(bsz, nc),
        in_specs=[pl.BlockSpec((1, 3, nh, lc, HEAD_DIM), lambda b, c: (b, 0, 0, c, 0)),
                  pl.BlockSpec((1, 3, nh, lc, HEAD_DIM), lambda b, c: (b, 0, 0, rev(c), 0)),
                  pl.BlockSpec((1, nh, 1, HEAD_DIM, lc), lambda b, c: (b, 0, c, 0, 0)),
                  pl.BlockSpec((1, nh, 1, HEAD_DIM, lc), lambda b, c: (b, 0, rev(c), 0, 0)),
                  pl.BlockSpec((1, lc, LANES), lambda b, c: (b, c, 0)),
                  pl.BlockSpec((1, lc, LANES), lambda b, c: (b, rev(c), 0)),
                  pl.BlockSpec((1, 1, LANES, lc), lambda b, c: (b, c, 0, 0)),
                  pl.BlockSpec((1, 1, LANES, lc), lambda b, c: (b, rev(c), 0, 0))],
        out_specs=[pl.BlockSpec((1, lc, nh * HEAD_DIM), lambda b, c: (b, c, 0)),
                   pl.BlockSpec((1, lc, nh * HEAD_DIM), lambda b, c: (b, rev(c), 0))],
        out_shape=[jax.ShapeDtypeStruct((bsz, t, nh * HEAD_DIM), F32),
                   jax.ShapeDtypeStruct((bsz, t, nh * HEAD_DIM), F32)],
        scratch_shapes=[pltpu.VMEM((2 * nh, HEAD_DIM, HEAD_DIM), F32)],
        compiler_params=_cparams("arbitrary", "arbitrary"),
        name="delta_scan",
    )(xh, xh, kt, kt, gn, gn, gnt4, gnt4)


def kernel(x, c, ctx, c_ctx, ada_w, ada_b, ln1_g, ln2_g, even_w_in, even_conv_w, even_a_log, even_dt_bias, even_out_norm_g, even_q_norm_g, even_k_norm_g, even_w_out, odd_w_in, odd_rpb, odd_w_out, router_w, expert_w_gate, expert_w_up, expert_w_down, final_norm_g):
    bsz, n_lat, d = x.shape
    n_ctx = ctx.shape[1]
    nct = n_ctx // TILE
    h = (ctx, x)
    rows = jnp.zeros((8, d), F32).at[0].set(c_ctx).at[1:1 + bsz].set(c)

    def mod_of(layer):
        m = _adaln(rows, ada_w, ada_b, layer).reshape(8, 6, d)
        return jnp.stack([jnp.broadcast_to(m[0], (bsz, 6, d)), m[1:1 + bsz]], axis=1)

    mod = mod_of(0)
    a_dim = d // 2
    a_heads = a_dim // HEAD_DIM
    w_in = even_w_in[0]
    o_g = 4 * a_dim
    o_bq = o_g + 4 * a_heads
    kv_dim = (w_in.shape[1] - o_bq - a_dim) // 2
    gates_w = jnp.zeros((d, LANES), F32).at[:, :4 * a_heads].set(w_in[:, o_g:o_bq])
    w_perm = jnp.concatenate([w_in[:, :o_g], w_in[:, o_bq:], gates_w], axis=1).astype(BF16)
    p = _modproj(h, mod, ln1_g[0], w_perm, nct, F32)
    xh, kt, gn, gnt = _delta_prep(p, even_conv_w[0], even_a_log[0], even_dt_bias[0], n_ctx=n_ctx, a_dim=a_dim)
    o_delta = _delta_scan(xh, kt, gn, gnt, n_ctx=n_ctx)
    cos, sin = _rope_tables(n_ctx, n_lat)
    qb, kb, vb = _gqa_prep(p, cos, sin, even_q_norm_g[0], even_k_norm_g[0], a_dim=a_dim, kv_dim=kv_dim)
    o_attn = _gqa_attention(qb, kb, vb, n_ctx=n_ctx)
    h = _outproj_even(h, mod, o_delta, p, o_attn, even_out_norm_g[0], even_w_out[0].astype(BF16), nct, a_dim=a_dim)
    h = _moe(h, mod, ln2_g[0], router_w[0], expert_w_gate, expert_w_up, expert_w_down, 0, final_norm_g,
             n_ctx=n_ctx, route_ctx=True, final_norm=False)

    mod = mod_of(1)
    p = _modproj(h, mod, ln1_g[1], odd_w_in[0].astype(BF16), nct, BF16)
    o_na = _natten(p, _natten_bias(odd_rpb[0]), n_ctx=n_ctx, d=d)
    h_lat = _outproj_odd(h, mod, o_na, odd_w_out[0].astype(BF16), nct)
    return _moe(h_lat, mod, ln2_g[1], router_w[1], expert_w_gate, expert_w_up, expert_w_down, 1, final_norm_g,
                n_ctx=0, route_ctx=False, final_norm=True)
```

```python
import functools
import math

import numpy as np
import jax
import jax.numpy as jnp
from jax import lax
from jax.experimental import pallas as pl
from jax.experimental.pallas import tpu as pltpu

F32, BF16, I32 = jnp.float32, jnp.bfloat16, jnp.int32

HEAD_DIM = 64
GRID_W = 64
DELTA_CHUNK = 64
NA_ROWS = 8
NA_COLS = 16
ROPE_THETA = 10000.0
EC_CAPACITY = 2
EPS = 1e-6
LOG2_E = math.log2(math.e)

LANES = 128
TILE = 256
SUBT = 128
WIN_X = SUBT + 8
WIN_Y = SUBT + 16
VMEM_LIMIT = 56 * 1024 * 1024


def _cparams(*sem):
    return pltpu.CompilerParams(dimension_semantics=sem, vmem_limit_bytes=VMEM_LIMIT)


def _split3(a):
    a1 = a.astype(BF16)
    r1 = a - a1.astype(F32)
    a2 = r1.astype(BF16)
    a3 = (r1 - a2.astype(F32)).astype(BF16)
    return a1, a2, a3


def _dot(a, b):
    return jnp.dot(a, b, preferred_element_type=F32)


def _dot_t(a, b):
    return lax.dot_general(a, b, (((1,), (1,)), ((), ())), preferred_element_type=F32)


def _dot_exact_rhs(a, b_bf16):
    a1, a2, a3 = _split3(a)
    return _dot(a1, b_bf16) + _dot(a2, b_bf16) + _dot(a3, b_bf16)


def _dot_hi(a, b):
    a1 = a.astype(BF16)
    a2 = (a - a1.astype(F32)).astype(BF16)
    b1 = b.astype(BF16)
    b2 = (b - b1.astype(F32)).astype(BF16)
    return _dot(a1, b1) + _dot(a1, b2) + _dot(a2, b1)


def _sigmoid(x):
    return 1.0 / (1.0 + jnp.exp(-x))


def _silu(x):
    return x * _sigmoid(x)


def _modulate(x, g, shift, scale):
    ms = jnp.mean(x * x, axis=-1, keepdims=True)
    return x * lax.rsqrt(ms + EPS) * g * (1.0 + scale) + shift


def _adaln_kernel(s_ref, w_ref, b_ref, o_ref):
    s = _silu(s_ref[...])
    o_ref[...] = _dot_hi(s, w_ref[0]) + b_ref[0]


def _adaln(rows, w, b, layer):
    depth, d, n = w.shape
    tn = n // 4
    return pl.pallas_call(
        _adaln_kernel,
        grid=(n // tn,),
        in_specs=[pl.BlockSpec((8, d), lambda j: (0, 0)),
                  pl.BlockSpec((1, d, tn), lambda j: (layer, 0, j)),
                  pl.BlockSpec((1, 1, tn), lambda j: (layer, 0, j))],
        out_specs=pl.BlockSpec((8, tn), lambda j: (0, j)),
        out_shape=jax.ShapeDtypeStruct((8, n), F32),
        compiler_params=_cparams("arbitrary"),
        name="adaln",
    )(rows, w, b.reshape(depth, 1, n))


def _hidden_operands(h, nct):
    if isinstance(h, tuple):
        hc, hl = h
        d = hc.shape[-1]
        specs = [pl.BlockSpec((1, TILE, d), lambda b, i: (b, jnp.minimum(i, nct - 1), 0)),
                 pl.BlockSpec((1, TILE, d), lambda b, i: (b, jnp.maximum(i - nct, 0), 0))]
        return [hc, hl], specs, hc.shape[1] + hl.shape[1]
    return [h], [pl.BlockSpec((1, TILE, h.shape[-1]), lambda b, i: (b, i, 0))], h.shape[1]


def _hidden_tile(h_refs, nct):
    if len(h_refs) == 2:
        return jnp.where(pl.program_id(1) < nct, h_refs[0][0], h_refs[1][0])
    return h_refs[0][0]


def _modproj_kernel(*refs, n_h, nct):
    mod_ref, g_ref, w_ref, o_ref = refs[n_h:]
    mod = mod_ref[0, 0]
    a = _modulate(_hidden_tile(refs[:n_h], nct), g_ref[...], mod[0:1], mod[1:2])
    o_ref[0] = _dot(a.astype(BF16), w_ref[...]).astype(o_ref.dtype)


def _modproj(h, mod, g, w_bf16, nct, out_dtype):
    h_ops, h_specs, t = _hidden_operands(h, nct)
    bsz, _, d = h_ops[0].shape
    n = w_bf16.shape[1]
    return pl.pallas_call(
        functools.partial(_modproj_kernel, n_h=len(h_ops), nct=nct),
        grid=(bsz, t // TILE),
        in_specs=h_specs + [pl.BlockSpec((1, 1, 6, d), lambda b, i: (b, jnp.where(i < nct, 0, 1), 0, 0)),
                            pl.BlockSpec((1, d), lambda b, i: (0, 0)),
                            pl.BlockSpec((d, n), lambda b, i: (0, 0))],
        out_specs=pl.BlockSpec((1, TILE, n), lambda b, i: (b, i, 0)),
        out_shape=jax.ShapeDtypeStruct((bsz, t, n), out_dtype),
        compiler_params=_cparams("arbitrary", "arbitrary"),
        name="modproj",
    )(*h_ops, mod, g.reshape(1, d), w_bf16)


def _router_kernel(h_ref, mod_ref, g_ref, wr_ref, a_ref, aff_ref, afft_ref, *, n_exp):
    mod = mod_ref[0, 0]
    a = _modulate(h_ref[0], g_ref[...], mod[3:4], mod[4:5])
    a_ref[0] = a.astype(BF16)
    logits = _dot_hi(a, wr_ref[...])
    lane = lax.broadcasted_iota(I32, logits.shape, 1)
    logits = jnp.where(lane < n_exp, logits, -jnp.inf)
    p = jnp.exp(logits - jnp.max(logits, axis=-1, keepdims=True))
    aff = p / jnp.sum(p, axis=-1, keepdims=True)
    aff_ref[0] = aff
    afft_ref[0] = aff.T


def _router(h, mod, g, w_router, nct):
    bsz, t, d = h.shape
    n_exp = w_router.shape[1]
    wr = jnp.zeros((d, LANES), F32).at[:, :n_exp].set(w_router)
    return pl.pallas_call(
        functools.partial(_router_kernel, n_exp=n_exp),
        grid=(bsz, t // TILE),
        in_specs=[pl.BlockSpec((1, TILE, d), lambda b, i: (b, i, 0)),
                  pl.BlockSpec((1, 1, 6, d), lambda b, i: (b, jnp.where(i < nct, 0, 1), 0, 0)),
                  pl.BlockSpec((1, d), lambda b, i: (0, 0)),
                  pl.BlockSpec((d, LANES), lambda b, i: (0, 0))],
        out_specs=[pl.BlockSpec((1, TILE, d), lambda b, i: (b, i, 0)),
                   pl.BlockSpec((1, TILE, LANES), lambda b, i: (b, i, 0)),
                   pl.BlockSpec((1, LANES, TILE), lambda b, i: (b, 0, i))],
        out_shape=[jax.ShapeDtypeStruct((bsz, t, d), BF16),
                   jax.ShapeDtypeStruct((bsz, t, LANES), F32),
                   jax.ShapeDtypeStruct((bsz, LANES, t), F32)],
        compiler_params=_cparams("arbitrary", "arbitrary"),
        name="router",
    )(h, mod, g.reshape(1, d), wr)


def _select_kernel(afft_ref, pos_ref, post_ref, cb_ref, cnt_ref, sel_ref, bucket_ref, *, n_exp, n_ctx, k_ctx, k_lat, t):
    tok = lax.broadcasted_iota(I32, (n_exp, t), 1)
    is_ctx = tok < n_ctx
    min_normal = 0x00800000

    def choose(in_set, k):
        def vals():
            return jnp.where(in_set, afft_ref[0][:n_exp], -1.0)

        def bit_step(it, thr):
            cand = thr | jnp.left_shift(jnp.int32(1), 30 - it)
            cnt = jnp.sum(jnp.where(vals() >= pltpu.bitcast(cand, F32), 1, 0), axis=1, keepdims=True)
            return jnp.where(cnt >= k, cand, thr)
        thr = lax.fori_loop(0, 31, bit_step, jnp.zeros((n_exp, 1), I32))
        lo = pltpu.bitcast(thr, F32)
        hi = pltpu.bitcast(jnp.maximum(thr + 1, min_normal), F32)
        v = vals()
        above = v >= hi
        sel_ref[...] = jnp.where(above, 1.0, sel_ref[...])
        bucket_ref[...] = jnp.where((v >= lo) & jnp.logical_not(above), 1.0, 0.0)
        need0 = k - jnp.sum(jnp.where(above, 1, 0), axis=1, keepdims=True)

        def peel(need):
            in_bucket = bucket_ref[...] > 0.0
            v = vals()
            top = jnp.max(jnp.where(in_bucket, v, -1.0), axis=1, keepdims=True)
            first = jnp.min(jnp.where(in_bucket & (v == top), tok, t), axis=1, keepdims=True)
            pick = (tok == first) & (need > 0)
            sel_ref[...] = jnp.where(pick, 1.0, sel_ref[...])
            bucket_ref[...] = jnp.where(pick, 0.0, bucket_ref[...])
            return need - 1
        lax.while_loop(lambda need: jnp.max(need) > 0, peel, need0)

    sel_ref[...] = jnp.zeros_like(sel_ref)
    choose(jnp.logical_not(is_ctx), k_lat)
    if k_ctx > 0:
        choose(is_ctx, k_ctx)

    rr = lax.broadcasted_iota(I32, (LANES, LANES), 0)
    cc = lax.broadcasted_iota(I32, (LANES, LANES), 1)
    upper = jnp.where(rr <= cc, 1.0, 0.0).astype(BF16)
    carry = jnp.zeros((n_exp, 1), F32)
    for blk in range(t // LANES):
        c = _dot(sel_ref[:, blk * LANES:(blk + 1) * LANES].astype(BF16), upper) + carry
        cnt_ref[:, blk * LANES:(blk + 1) * LANES] = c
        carry = c[:, LANES - 1:LANES]
    posinc = cnt_ref[...].astype(I32)
    pos = jnp.where(sel_ref[...] > 0.0, posinc - 1, -1)
    pos_ref[0] = pos

    lane = lax.broadcasted_iota(I32, (n_exp, LANES), 1)
    cb = jnp.zeros((n_exp, LANES), I32)
    for i in range(1, t // SUBT + 1):
        cb = jnp.where(lane == i, posinc[:, i * SUBT - 1:i * SUBT], cb)
    cb_ref[0] = cb

    padded = jnp.concatenate([pos.astype(F32), jnp.full((LANES - n_exp, t), -1.0, F32)], axis=0)
    for i in range(t // TILE):
        post_ref[0, i * TILE:(i + 1) * TILE, :] = padded[:, i * TILE:(i + 1) * TILE].T.astype(I32)


def _select(afft, n_exp, n_ctx, k_ctx, k_lat):
    bsz, _, t = afft.shape
    assert n_ctx % LANES == 0 and t % TILE == 0 and t // SUBT < LANES
    return pl.pallas_call(
        functools.partial(_select_kernel, n_exp=n_exp, n_ctx=n_ctx, k_ctx=k_ctx, k_lat=k_lat, t=t),
        grid=(bsz,),
        in_specs=[pl.BlockSpec((1, LANES, t), lambda b: (b, 0, 0))],
        out_specs=[pl.BlockSpec((1, n_exp, t), lambda b: (b, 0, 0)),
                   pl.BlockSpec((1, t, LANES), lambda b: (b, 0, 0)),
                   pl.BlockSpec((1, n_exp, LANES), lambda b: (b, 0, 0))],
        out_shape=[jax.ShapeDtypeStruct((bsz, n_exp, t), I32),
                   jax.ShapeDtypeStruct((bsz, t, LANES), I32),
                   jax.ShapeDtypeStruct((bsz, n_exp, LANES), I32)],
        scratch_shapes=[pltpu.VMEM((n_exp, t), F32)] * 3,
        compiler_params=_cparams("arbitrary"),
        name="select",
    )(afft)


def _ffn_rows(cap_tot):
    for rc in range(min(cap_tot, 512), 7, -8):
        if cap_tot % rc == 0:
            return rc
    raise ValueError(cap_tot)


def _expert_kernel(cb_ref, pos_ref, a_ref, wg_ref, wu_ref, wd_ref, y_ref, x_ref, *, n_exp, nsub, sub_per, cap_tot):
    e, b, ch = pl.program_id(0), pl.program_id(1), pl.program_id(2)

    @pl.when(ch == 0)
    def _():
        x_ref[...] = jnp.zeros_like(x_ref)

    base = (b * n_exp + e) * (nsub + 1) + ch * sub_per
    row = lax.broadcasted_iota(I32, (WIN_X, SUBT), 0)

    def gather(j, carry):
        start = cb_ref[base + j]
        start_al = pl.multiple_of((start // 8) * 8, 8)
        rel = pos_ref[0, 0, j] - start_al
        onehot = jnp.where(row == rel, 1.0, 0.0).astype(BF16)
        tok0 = pl.multiple_of(j * SUBT, SUBT)
        x_ref[pl.ds(start_al, WIN_X), :] += _dot(onehot, a_ref[0, pl.ds(tok0, SUBT), :])
        return carry
    lax.fori_loop(0, sub_per, gather, 0, unroll=True)

    @pl.when(ch == pl.num_programs(2) - 1)
    def _():
        rc = _ffn_rows(cap_tot)
        wg = wg_ref[0, 0].astype(BF16)
        wu = wu_ref[0, 0].astype(BF16)
        wd = wd_ref[0, 0].astype(BF16)
        for r0 in range(0, cap_tot, rc):
            xc = x_ref[r0:r0 + rc, :].astype(BF16)
            hid = _silu(_dot(xc, wg)) * _dot(xc, wu)
            y_ref[0, 0, r0:r0 + rc, :] = _dot(hid.astype(BF16), wd).astype(y_ref.dtype)
        y_ref[0, 0, cap_tot:, :] = jnp.zeros((y_ref.shape[2] - cap_tot, y_ref.shape[-1]), y_ref.dtype)


def _experts(cb_flat, pos, a, wg, wu, wd, layer, cap_tot):
    bsz, n_exp, t = pos.shape
    d = a.shape[-1]
    ff = wg.shape[-1]
    nsub = t // SUBT
    ntiles = t // TILE
    chunk = TILE * max(m for m in range(1, 13) if ntiles % m == 0)
    sub_per = chunk // SUBT
    assert cap_tot % 16 == 0
    rows_x = cap_tot + WIN_X
    rows_y = cap_tot + WIN_Y
    pos5 = pos.reshape(bsz, n_exp, nsub, 1, SUBT)
    grid_spec = pltpu.PrefetchScalarGridSpec(
        num_scalar_prefetch=1,
        grid=(n_exp, bsz, t // chunk),
        in_specs=[pl.BlockSpec((1, 1, sub_per, 1, SUBT), lambda e, b, c, cb: (b, e, c, 0, 0)),
                  pl.BlockSpec((1, chunk, d), lambda e, b, c, cb: (b, c, 0)),
                  pl.BlockSpec((1, 1, d, ff), lambda e, b, c, cb: (layer, e, 0, 0)),
                  pl.BlockSpec((1, 1, d, ff), lambda e, b, c, cb: (layer, e, 0, 0)),
                  pl.BlockSpec((1, 1, ff, d), lambda e, b, c, cb: (layer, e, 0, 0))],
        out_specs=pl.BlockSpec((1, 1, rows_y, d), lambda e, b, c, cb: (b, e, 0, 0)),
        scratch_shapes=[pltpu.VMEM((rows_x, d), F32)],
    )
    return pl.pallas_call(
        functools.partial(_expert_kernel, n_exp=n_exp, nsub=nsub, sub_per=sub_per, cap_tot=cap_tot),
        grid_spec=grid_spec,
        out_shape=jax.ShapeDtypeStruct((bsz, n_exp, rows_y, d), BF16),
        compiler_params=_cparams("arbitrary", "arbitrary", "arbitrary"),
        name="experts",
    )(cb_flat, pos5, a, wg, wu, wd)


def _combine(cb_flat, h, mod, aff, post, y, g_final, nct, final_norm):
    bsz, t, d = h.shape
    n_exp = y.shape[1]
    ntiles = t // TILE
    first = nct if final_norm else 0
    n_win = n_exp * (TILE // SUBT)
    grid_spec = pltpu.PrefetchScalarGridSpec(
        num_scalar_prefetch=1,
        grid=(bsz, ntiles - first),
        in_specs=[pl.BlockSpec((1, TILE, d), lambda b, i, cb: (b, i + first, 0)),
                  pl.BlockSpec((1, 1, 6, d), lambda b, i, cb: (b, jnp.where(i + first < nct, 0, 1), 0, 0)),
                  pl.BlockSpec((1, TILE, LANES), lambda b, i, cb: (b, i + first, 0)),
                  pl.BlockSpec((1, TILE, LANES), lambda b, i, cb: (b, i + first, 0)),
                  pl.BlockSpec(memory_space=pl.ANY),
                  pl.BlockSpec((1, d), lambda b, i, cb: (0, 0))],
        out_specs=pl.BlockSpec((1, TILE, d), lambda b, i, cb: (b, i, 0)),
        scratch_shapes=[pltpu.VMEM((2, n_win, WIN_Y, d), y.dtype), pltpu.SemaphoreType.DMA((2, n_win))],
    )
    return pl.pallas_call(
        functools.partial(_combine_kernel, n_exp=n_exp, nsub=t // SUBT, first=first, final_norm=final_norm),
        grid_spec=grid_spec,
        out_shape=jax.ShapeDtypeStruct((bsz, t - first * TILE, d), F32),
        compiler_params=_cparams("arbitrary", "arbitrary"),
        name="combine",
    )(cb_flat, h, mod, aff, post, y, g_final.reshape(1, d))


def _combine_kernel(cb_ref, h_ref, mod_ref, aff_ref, post_ref, y_hbm, g_ref, o_ref, ybuf, sem,
                    *, n_exp, nsub, first, final_norm):
    n_i = pl.num_programs(1)
    step = pl.program_id(0) * n_i + pl.program_id(1)
    n_steps = pl.num_programs(0) * n_i
    per_tile = TILE // SUBT

    def window(at_step, sub, e):
        b = at_step // n_i
        i = at_step - b * n_i + first
        start = cb_ref[(b * n_exp + e) * (nsub + 1) + i * per_tile + sub]
        start_al = pl.multiple_of((start // 16) * 16, 16)
        buf, slot = at_step % 2, sub * n_exp + e
        copy = pltpu.make_async_copy(y_hbm.at[b, e, pl.ds(start_al, WIN_Y), :], ybuf.at[buf, slot],
                                     sem.at[buf, slot])
        return copy, start_al

    def start_windows(at_step):
        for sub in range(per_tile):
            for e in range(n_exp):
                window(at_step, sub, e)[0].start()

    @pl.when(step == 0)
    def _():
        start_windows(step)

    @pl.when(step + 1 < n_steps)
    def _():
        start_windows(step + 1)

    lane = lax.broadcasted_iota(I32, (SUBT, WIN_Y), 1)
    mod = mod_ref[0, 0]
    starts = {}
    for sub in range(per_tile):
        for e in range(n_exp):
            copy, starts[sub, e] = window(step, sub, e)
            copy.wait()
    for sub in range(per_tile):
        rows = slice(sub * SUBT, (sub + 1) * SUBT)
        post = post_ref[0, rows, :]
        aff = aff_ref[0, rows, :]
        acc = jnp.zeros((SUBT, o_ref.shape[-1]), F32)
        for e in range(n_exp):
            rel = post[:, e:e + 1] - starts[sub, e]
            onehot = jnp.where(lane == rel, 1.0, 0.0).astype(BF16)
            acc = acc + aff[:, e:e + 1] * _dot(onehot, ybuf[step % 2, sub * n_exp + e])
        out = h_ref[0, rows, :] + mod[5:6] * acc
        if final_norm:
            ms = jnp.mean(out * out, axis=-1, keepdims=True)
            out = out * lax.rsqrt(ms + EPS) * g_ref[...]
        o_ref[0, rows, :] = out


def _moe(h, mod, ln2_g, w_router, wg, wu, wd, layer, g_final, *, n_ctx, route_ctx, final_norm):
    bsz, t, d = h.shape
    n_exp = w_router.shape[1]
    nct = n_ctx // TILE
    n_lat = t - n_ctx
    k_lat = max(1, EC_CAPACITY * n_lat // n_exp)
    k_ctx = max(1, EC_CAPACITY * n_ctx // n_exp) if route_ctx else 0
    a, aff, afft = _router(h, mod, ln2_g, w_router, nct)
    pos, post, cb = _select(afft, n_exp, n_ctx, k_ctx, k_lat)
    cb_flat = cb[:, :, :t // SUBT + 1].reshape(-1)
    y = _experts(cb_flat, pos, a, wg, wu, wd, layer, k_ctx + k_lat)
    return _combine(cb_flat, h, mod, aff, post, y, g_final, nct, final_norm)


HEAD_SHIFT = int(math.log2(HEAD_DIM))


def _same_head(rows, cols):
    r = lax.shift_right_logical(lax.broadcasted_iota(I32, (rows, cols), 0), HEAD_SHIFT)
    c = lax.shift_right_logical(lax.broadcasted_iota(I32, (rows, cols), 1), HEAD_SHIFT)
    return jnp.where(r == c, 1.0, 0.0)


def _head_sum(x2):
    n = x2.shape[-1]
    return _dot_exact_rhs(x2, _same_head(n, n).astype(BF16))


def _outproj_even_kernel(*refs, n_h, nct, a_dim):
    mod_ref, of_ref, ob_ref, z_ref, at_ref, g_ref, w_ref, o_ref = refs[n_h:]
    mod = mod_ref[0, 0]
    o = of_ref[0] + ob_ref[0]
    ms = _head_sum(o * o) * (1.0 / HEAD_DIM)
    ya = o * lax.rsqrt(ms + EPS) * g_ref[...] * _silu(z_ref[0])
    y = _dot(ya.astype(BF16), w_ref[:a_dim, :]) + _dot(at_ref[0], w_ref[a_dim:, :])
    o_ref[0] = _hidden_tile(refs[:n_h], nct) + mod[2:3] * y


def _outproj_even(h, mod, o_delta, p, o_attn, out_norm_g, w_bf16, nct, *, a_dim):
    h_ops, h_specs, t = _hidden_operands(h, nct)
    bsz, _, d = h_ops[0].shape
    o_f, o_b = o_delta
    g = jnp.tile(out_norm_g, a_dim // HEAD_DIM).reshape(1, a_dim)
    tok = lambda b, i: (b, i, 0)
    return pl.pallas_call(
        functools.partial(_outproj_even_kernel, n_h=len(h_ops), nct=nct, a_dim=a_dim),
        grid=(bsz, t // TILE),
        in_specs=h_specs + [pl.BlockSpec((1, 1, 6, d), lambda b, i: (b, jnp.where(i < nct, 0, 1), 0, 0)),
                            pl.BlockSpec((1, TILE, a_dim), tok),
                            pl.BlockSpec((1, TILE, a_dim), tok),
                            pl.BlockSpec((1, TILE, a_dim), lambda b, i: (b, i, 3)),
                            pl.BlockSpec((1, TILE, a_dim), tok),
                            pl.BlockSpec((1, a_dim), lambda b, i: (0, 0)),
                            pl.BlockSpec((d, d), lambda b, i: (0, 0))],
        out_specs=pl.BlockSpec((1, TILE, d), tok),
        out_shape=jax.ShapeDtypeStruct((bsz, t, d), F32),
        compiler_params=_cparams("arbitrary", "arbitrary"),
        name="outproj_even",
    )(*h_ops, mod, o_f, o_b, p, o_attn, g, w_bf16)


def _outproj_odd_kernel(h_ref, mod_ref, a_ref, w_ref, o_ref):
    mod = mod_ref[0, 0]
    o_ref[0] = h_ref[0] + mod[2:3] * _dot(a_ref[0], w_ref[...])


def _outproj_odd(h, mod, o_na, w_bf16, nct):
    bsz, t, d = h.shape
    n_lat = o_na.shape[1]
    return pl.pallas_call(
        _outproj_odd_kernel,
        grid=(bsz, n_lat // TILE),
        in_specs=[pl.BlockSpec((1, TILE, d), lambda b, i: (b, i + nct, 0)),
                  pl.BlockSpec((1, 1, 6, d), lambda b, i: (b, 1, 0, 0)),
                  pl.BlockSpec((1, TILE, d), lambda b, i: (b, i, 0)),
                  pl.BlockSpec((d, d), lambda b, i: (0, 0))],
        out_specs=pl.BlockSpec((1, TILE, d), lambda b, i: (b, i, 0)),
        out_shape=jax.ShapeDtypeStruct((bsz, n_lat, d), F32),
        compiler_params=_cparams("arbitrary", "arbitrary"),
        name="outproj_odd",
    )(h, mod, o_na, w_bf16)


def _rope_tables(n_ctx, n_lat):
    tt = jnp.arange(n_lat)
    n_freq = HEAD_DIM // 4
    inv = ROPE_THETA ** (-jnp.arange(n_freq, dtype=F32) / n_freq)
    ang = jnp.concatenate([(tt // GRID_W).astype(F32)[:, None] * inv, (tt % GRID_W).astype(F32)[:, None] * inv], -1)
    cos = jnp.concatenate([jnp.ones((n_ctx, 2 * n_freq), F32), jnp.cos(ang)], axis=0)
    sin = jnp.concatenate([jnp.zeros((n_ctx, 2 * n_freq), F32), jnp.sin(ang)], axis=0)
    reps = LANES // HEAD_DIM
    return jnp.tile(cos, (1, 2 * reps)), jnp.tile(jnp.concatenate([-sin, sin], axis=1), (1, reps))


def _rope(x, cos, sin):
    n = x.shape[-1]
    half = HEAD_DIM // 2
    lane = lax.broadcasted_iota(I32, x.shape, 1)
    first = (lane & (HEAD_DIM - 1)) < half
    partner = jnp.where(first, pltpu.roll(x, n - half, 1), pltpu.roll(x, half, 1))
    reps = n // LANES
    cos_n = jnp.concatenate([cos] * reps, axis=1) if reps > 1 else cos
    sin_n = jnp.concatenate([sin] * reps, axis=1) if reps > 1 else sin
    return x * cos_n + partner * sin_n


def _gqa_prep_kernel(q_ref, k_ref, v_ref, cos_ref, sin_ref, gq_ref, gk_ref, qo_ref, ko_ref, vo_ref):
    cos, sin = cos_ref[...], sin_ref[...]
    q = q_ref[0]
    q = q * lax.rsqrt(_head_sum(q * q) * (1.0 / HEAD_DIM) + EPS) * gq_ref[...] * (HEAD_DIM ** -0.5 * LOG2_E)
    qo_ref[0] = _rope(q, cos, sin).astype(BF16)
    k = k_ref[0]
    k = k * lax.rsqrt(_head_sum(k * k) * (1.0 / HEAD_DIM) + EPS) * gk_ref[...]
    k = _rope(k, cos, sin).astype(BF16)
    v = v_ref[0].astype(BF16)
    for hh in range(k.shape[-1] // HEAD_DIM):
        ko_ref[0, hh] = k[:, hh * HEAD_DIM:(hh + 1) * HEAD_DIM]
        vo_ref[0, hh] = v[:, hh * HEAD_DIM:(hh + 1) * HEAD_DIM]


def _gqa_prep(p, cos, sin, q_norm_g, k_norm_g, *, a_dim, kv_dim):
    bsz, t, _ = p.shape
    q_dim = a_dim
    kvh = kv_dim // HEAD_DIM
    assert kv_dim == LANES
    c_q = (4 * a_dim) // q_dim
    c_k = (4 * a_dim + q_dim) // kv_dim
    gq = jnp.tile(q_norm_g, q_dim // HEAD_DIM).reshape(1, q_dim)
    gk = jnp.tile(k_norm_g, kvh).reshape(1, kv_dim)
    return pl.pallas_call(
        _gqa_prep_kernel,
        grid=(bsz, t // TILE),
        in_specs=[pl.BlockSpec((1, TILE, q_dim), lambda b, i: (b, i, c_q)),
                  pl.BlockSpec((1, TILE, kv_dim), lambda b, i: (b, i, c_k)),
                  pl.BlockSpec((1, TILE, kv_dim), lambda b, i: (b, i, c_k + 1)),
                  pl.BlockSpec((TILE, LANES), lambda b, i: (i, 0)),
                  pl.BlockSpec((TILE, LANES), lambda b, i: (i, 0)),
                  pl.BlockSpec((1, q_dim), lambda b, i: (0, 0)),
                  pl.BlockSpec((1, kv_dim), lambda b, i: (0, 0))],
        out_specs=[pl.BlockSpec((1, TILE, q_dim), lambda b, i: (b, i, 0)),
                   pl.BlockSpec((1, kvh, TILE, HEAD_DIM), lambda b, i: (b, 0, i, 0)),
                   pl.BlockSpec((1, kvh, TILE, HEAD_DIM), lambda b, i: (b, 0, i, 0))],
        out_shape=[jax.ShapeDtypeStruct((bsz, t, q_dim), BF16),
                   jax.ShapeDtypeStruct((bsz, kvh, t, HEAD_DIM), BF16),
                   jax.ShapeDtypeStruct((bsz, kvh, t, HEAD_DIM), BF16)],
        compiler_params=_cparams("arbitrary", "arbitrary"),
        name="gqa_prep",
    )(p, p, p, cos, sin, gq, gk)


SUB_K = 256
Q_SLAB = 256


def _gqa_kernel(q_ref, k_ref, v_ref, o_ref, qs_ref, m_ref, l_ref, acc_ref, *, n_ctx, nct, tq, tk, n_heads, kvh):
    i, kv = pl.program_id(1), pl.program_id(2)
    total = n_heads * tq
    per_kv = total // kvh

    @pl.when(kv == 0)
    def _():
        m_ref[...] = jnp.full_like(m_ref, -jnp.inf)
        l_ref[...] = jnp.zeros_like(l_ref)
        acc_ref[...] = jnp.zeros_like(acc_ref)
        q = q_ref[0]
        for hh in range(n_heads):
            qs_ref[hh * tq:(hh + 1) * tq, :] = q[:, hh * HEAD_DIM:(hh + 1) * HEAD_DIM]

    def attend(n_keys, n_valid):
        edges = ([0, Q_SLAB // 2] + list(range(Q_SLAB + Q_SLAB // 2, total - Q_SLAB // 2, Q_SLAB))
                 + [total - Q_SLAB // 2, total])
        edges = sorted(set(edges) | set(range(0, total + 1, per_kv)))
        slabs = [slice(a, b) for a, b in zip(edges[:-1], edges[1:])]
        keys = lambda rows: k_ref[0, rows.start // per_kv, :n_keys, :]
        vals = lambda rows: v_ref[0, rows.start // per_kv, :n_keys, :]
        scores = {0: _dot_t(qs_ref[slabs[0], :], keys(slabs[0]))}
        for idx, rows in enumerate(slabs):
            if idx + 1 < len(slabs):
                scores[idx + 1] = _dot_t(qs_ref[slabs[idx + 1], :], keys(slabs[idx + 1]))
            s = scores.pop(idx)
            if n_valid is not None:
                s = jnp.where(lax.broadcasted_iota(I32, s.shape, 1) < n_valid, s, -jnp.inf)
            m_old = m_ref[rows, :]
            m_new = jnp.maximum(m_old, jnp.max(s, axis=-1, keepdims=True))
            alpha = jnp.exp2(m_old - m_new)
            p = jnp.exp2(s - m_new)
            l_ref[rows, :] = alpha * l_ref[rows, :] + jnp.sum(p, axis=-1, keepdims=True)
            acc_ref[rows, :] = alpha * acc_ref[rows, :] + _dot(p.astype(BF16), vals(rows))
            m_ref[rows, :] = m_new

    @pl.when(i >= nct)
    def _():
        attend(tk, None)

    @pl.when((i < nct) & (kv == 0))
    def _():
        n_keys = -(-n_ctx // SUB_K) * SUB_K
        attend(n_keys, n_ctx if n_keys != n_ctx else None)

    @pl.when(kv == pl.num_programs(2) - 1)
    def _():
        out = acc_ref[...] / l_ref[...]
        for hh in range(n_heads):
            o_ref[0, :, hh * HEAD_DIM:(hh + 1) * HEAD_DIM] = out[hh * tq:(hh + 1) * tq, :].astype(o_ref.dtype)


def _gqa_attention(q, k, v, *, n_ctx):
    bsz, t, q_dim = q.shape
    kvh = k.shape[1]
    n_heads = q_dim // HEAD_DIM
    tq = TILE
    nt = t // TILE
    tk = TILE * max(m for m in range(1, 13) if nt % m == 0)
    assert n_ctx <= tk and tk % SUB_K == 0 and (n_heads // kvh * tq) % Q_SLAB == 0
    return pl.pallas_call(
        functools.partial(_gqa_kernel, n_ctx=n_ctx, nct=n_ctx // tq, tq=tq, tk=tk, n_heads=n_heads, kvh=kvh),
        grid=(bsz, t // tq, t // tk),
        in_specs=[pl.BlockSpec((1, tq, q_dim), lambda b, i, j: (b, i, 0)),
                  pl.BlockSpec((1, kvh, tk, HEAD_DIM), lambda b, i, j: (b, 0, j, 0)),
                  pl.BlockSpec((1, kvh, tk, HEAD_DIM), lambda b, i, j: (b, 0, j, 0))],
        out_specs=pl.BlockSpec((1, tq, q_dim), lambda b, i, j: (b, i, 0)),
        out_shape=jax.ShapeDtypeStruct((bsz, t, q_dim), BF16),
        scratch_shapes=[pltpu.VMEM((n_heads * tq, HEAD_DIM), BF16),
                        pltpu.VMEM((n_heads * tq, 1), F32),
                        pltpu.VMEM((n_heads * tq, 1), F32),
                        pltpu.VMEM((n_heads * tq, HEAD_DIM), F32)],
        compiler_params=_cparams("arbitrary", "arbitrary", "arbitrary"),
        name="gqa_attention",
    )(q, k, v)


def _natten_bias(rpb):
    n_heads, n_off, n_rel = rpb.shape
    col = np.arange(GRID_W)
    col_start = np.clip(col - NA_COLS // 2, 0, GRID_W - NA_COLS)
    valid = (col[None, :] >= col_start[:, None]) & (col[None, :] < col_start[:, None] + NA_COLS)
    rel = col[None, :] - col[:, None] + NA_COLS - 1
    onehot = ((rel[None] == np.arange(n_rel)[:, None, None]) & valid[None]).astype(np.float32)
    tab = jnp.einsum('hrk,kcd->hrcd', rpb, jnp.asarray(onehot), precision=lax.Precision.HIGHEST)
    tab = jnp.where(valid[None, None], tab, -jnp.inf)
    pairs = jnp.concatenate([tab[:, :-1], tab[:, 1:]], axis=-1)
    return pairs.reshape(n_heads // 2, 2, n_off - 1, GRID_W, 2 * GRID_W).astype(F32)


def _natten_kernel(*refs, n_ctx, n_rows, rblk, n_q):
    q_refs = refs[:n_q]
    k_ref, v_ref, bias_ref, o_ref = refs[n_q:]
    per_q = TILE // GRID_W
    rb = pl.program_id(2)
    first_head = lax.broadcasted_iota(I32, (1, LANES), 1) < HEAD_DIM
    kc = k_ref[0, 0:n_ctx, :]
    vc = v_ref[0, 0:n_ctx, :]
    band = NA_ROWS * GRID_W
    qs, ks, vs, biases = [], [], [], []
    for rr in range(rblk):
        rg = rb * rblk + rr
        rs = jnp.clip(rg - NA_ROWS // 2, 0, n_rows - NA_ROWS)
        variant = rs - rg + (NA_ROWS - 1)
        start = pl.multiple_of(n_ctx + rs * GRID_W, GRID_W)
        keys = jnp.concatenate([kc, k_ref[0, pl.ds(start, band), :]], axis=0)
        vals = jnp.concatenate([vc, v_ref[0, pl.ds(start, band), :]], axis=0)
        q_lo = (rr % per_q) * GRID_W
        q = q_refs[rr // per_q][0, q_lo:q_lo + GRID_W, :] * (HEAD_DIM ** -0.5)
        qs.append(jnp.concatenate([jnp.where(first_head, q, jnp.zeros_like(q)),
                                   jnp.where(first_head, jnp.zeros_like(q), q)], axis=0))
        ks.append(keys)
        vs.append(vals)
        biases.append(jnp.concatenate(
            [jnp.concatenate([bias_ref[0, hh, variant + j] for j in range(0, NA_ROWS, 2)], axis=1)
             for hh in range(2)], axis=0))
    batch = (((2,), (2,)), ((0,), (0,)))
    s = lax.dot_general(jnp.stack(qs), jnp.stack(ks), batch, preferred_element_type=F32)
    sc = s[:, :, :n_ctx]
    sb = s[:, :, n_ctx:] + jnp.stack(biases)
    m = jnp.maximum(jnp.max(sc, axis=-1, keepdims=True), jnp.max(sb, axis=-1, keepdims=True))
    pc = jnp.exp(sc - m)
    pb = jnp.exp(sb - m)
    l = jnp.sum(pc, axis=-1, keepdims=True) + jnp.sum(pb, axis=-1, keepdims=True)
    p = jnp.concatenate([pc, pb], axis=-1).astype(BF16)
    o = lax.dot_general(p, jnp.stack(vs), (((2,), (1,)), ((0,), (0,))), preferred_element_type=F32) / l
    for rr in range(rblk):
        o_ref[0, rr * GRID_W:(rr + 1) * GRID_W, :] = jnp.where(first_head, o[rr, :GRID_W], o[rr, GRID_W:]).astype(o_ref.dtype)


def _natten(p, bias, *, n_ctx, d):
    bsz, t, _ = p.shape
    n_lat = t - n_ctx
    n_rows = n_lat // GRID_W
    n_q = 2 if n_lat % (2 * TILE) == 0 else 1
    rblk = n_q * TILE // GRID_W
    npairs = d // LANES
    nct = n_ctx // TILE
    assert n_rows >= NA_ROWS and n_ctx % TILE == 0
    return pl.pallas_call(
        functools.partial(_natten_kernel, n_ctx=n_ctx, n_rows=n_rows, rblk=rblk, n_q=n_q),
        grid=(bsz, npairs, n_lat // (n_q * TILE)),
        in_specs=[pl.BlockSpec((1, TILE, LANES), lambda b, hp, r, j=j: (b, nct + n_q * r + j, hp)) for j in range(n_q)] + [
                  pl.BlockSpec((1, t, LANES), lambda b, hp, r: (b, 0, npairs + hp)),
                  pl.BlockSpec((1, t, LANES), lambda b, hp, r: (b, 0, 2 * npairs + hp)),
                  pl.BlockSpec((1, 2, 2 * NA_ROWS - 2, GRID_W, 2 * GRID_W), lambda b, hp, r: (hp, 0, 0, 0, 0))],
        out_specs=pl.BlockSpec((1, n_q * TILE, LANES), lambda b, hp, r: (b, r, hp)),
        out_shape=jax.ShapeDtypeStruct((bsz, n_lat, d), BF16),
        compiler_params=_cparams("arbitrary", "arbitrary", "arbitrary"),
        name="natten",
    )(*([p] * n_q), p, p, bias)


def _dot_exact_lhs(a_bf16, b):
    b1, b2, b3 = _split3(b)
    return _dot(a_bf16, b1) + _dot(a_bf16, b2) + _dot(a_bf16, b3)


def _delta_prep_kernel(x_ref, prev_ref, next_ref, gate_ref, w_ref, alog_ref, dt_ref, xh_ref, kt_ref, gn_ref, gnt_ref,
                       xs_ref,
                       *, nct, a_dim):
    i = pl.program_id(1)
    last = pl.num_programs(1) - 1
    x = x_ref[0]
    rows = x.shape[0]
    xs_ref[0:8, :] = prev_ref[0]
    xs_ref[8:8 + rows, :] = x
    xs_ref[8 + rows:16 + rows, :] = next_ref[0]
    r = lax.broadcasted_iota(I32, (rows, 1), 0)
    at_start = (i == 0) | (i == nct)
    at_end = (i == nct - 1) | (i == last)
    xm = jnp.where((r == 0) & at_start, 0.0, xs_ref[7:7 + rows, :])
    xp = jnp.where((r == rows - 1) & at_end, 0.0, xs_ref[9:9 + rows, :])
    w = w_ref[...]
    y = _silu(w[0:1] * xm + w[1:2] * x + w[2:3] * xp)
    q, k, v = y[:, :a_dim], y[:, a_dim:2 * a_dim], y[:, 2 * a_dim:]
    nh = a_dim // HEAD_DIM
    q = (q * lax.rsqrt(_head_sum(q * q) + EPS) * (HEAD_DIM ** -0.5)).astype(BF16)
    k = k * lax.rsqrt(_head_sum(k * k) + EPS)
    k_t = k.T.astype(BF16)
    for hh in range(nh):
        for cc in range(rows // DELTA_CHUNK):
            kt_ref[0, hh, cc] = k_t[hh * HEAD_DIM:(hh + 1) * HEAD_DIM, cc * DELTA_CHUNK:(cc + 1) * DELTA_CHUNK]
    k = k.astype(BF16)
    v = v.astype(BF16)
    for kind, arr in enumerate((q, k, v)):
        for hh in range(nh):
            xh_ref[0, kind, hh] = arr[:, hh * HEAD_DIM:(hh + 1) * HEAD_DIM]

    g = gate_ref[0]
    z = g + dt_ref[...]
    softplus = jnp.maximum(z, 0.0) + jnp.log1p(jnp.exp(-jnp.abs(z)))
    log_a = -jnp.exp(alog_ref[...]) * softplus
    beta = _sigmoid(g)
    ri = lax.broadcasted_iota(I32, (rows, rows), 0)
    ci = lax.broadcasted_iota(I32, (rows, rows), 1)
    chunk_shift = int(math.log2(DELTA_CHUNK))
    same_chunk = lax.shift_right_logical(ri, chunk_shift) == lax.shift_right_logical(ci, chunk_shift)
    prefix = jnp.where(same_chunk & (ci <= ri), 1.0, 0.0).astype(BF16)
    suffix = jnp.where(same_chunk & (ci >= ri), 1.0, 0.0).astype(BF16)
    g_f = _dot_exact_lhs(prefix, log_a)
    g_b = _dot_exact_lhs(suffix, log_a)
    lane = lax.broadcasted_iota(I32, g.shape, 1)
    narrow = jnp.where(lane < nh, g_f, jnp.where(lane < 2 * nh, beta, jnp.where(lane < 3 * nh, g_b, beta)))
    gn_ref[0] = narrow
    gnt_ref[0] = narrow.T


def _delta_prep(p, conv_w, a_log, dt_bias, *, n_ctx, a_dim):
    bsz, t, _ = p.shape
    nh = a_dim // HEAD_DIM
    assert DELTA_CHUNK == HEAD_DIM and TILE % DELTA_CHUNK == 0 and 4 * nh <= LANES
    nct = n_ctx // TILE
    c_gate = (4 * a_dim + a_dim + 2 * LANES) // LANES
    alog = jnp.zeros((1, LANES), F32).at[0, :nh].set(a_log[0]).at[0, 2 * nh:3 * nh].set(a_log[1])
    dtb = jnp.zeros((1, LANES), F32).at[0, :nh].set(dt_bias[0]).at[0, 2 * nh:3 * nh].set(dt_bias[1])
    nblk8 = t // 8
    per = TILE // 8
    return pl.pallas_call(
        functools.partial(_delta_prep_kernel, nct=nct, a_dim=a_dim),
        grid=(bsz, t // TILE),
        in_specs=[pl.BlockSpec((1, TILE, 3 * a_dim), lambda b, i: (b, i, 0)),
                  pl.BlockSpec((1, 8, 3 * a_dim), lambda b, i: (b, jnp.maximum(i * per - 1, 0), 0)),
                  pl.BlockSpec((1, 8, 3 * a_dim), lambda b, i: (b, jnp.minimum((i + 1) * per, nblk8 - 1), 0)),
                  pl.BlockSpec((1, TILE, LANES), lambda b, i: (b, i, c_gate)),
                  pl.BlockSpec((3, 3 * a_dim), lambda b, i: (0, 0)),
                  pl.BlockSpec((1, LANES), lambda b, i: (0, 0)),
                  pl.BlockSpec((1, LANES), lambda b, i: (0, 0))],
        out_specs=[pl.BlockSpec((1, 3, nh, TILE, HEAD_DIM), lambda b, i: (b, 0, 0, i, 0)),
                   pl.BlockSpec((1, nh, TILE // DELTA_CHUNK, HEAD_DIM, DELTA_CHUNK), lambda b, i: (b, 0, i, 0, 0)),
                   pl.BlockSpec((1, TILE, LANES), lambda b, i: (b, i, 0)),
                   pl.BlockSpec((1, LANES, TILE), lambda b, i: (b, 0, i))],
        out_shape=[jax.ShapeDtypeStruct((bsz, 3, nh, t, HEAD_DIM), BF16),
                   jax.ShapeDtypeStruct((bsz, nh, t // DELTA_CHUNK, HEAD_DIM, DELTA_CHUNK), BF16),
                   jax.ShapeDtypeStruct((bsz, t, LANES), F32),
                   jax.ShapeDtypeStruct((bsz, LANES, t), F32)],
        scratch_shapes=[pltpu.VMEM((TILE + 16, 3 * a_dim), F32)],
        compiler_params=_cparams("arbitrary", "arbitrary"),
        name="delta_prep",
    )(p, p, p, p, conv_w, alog, dtb)


def _bdot(a, b, ca, cb, hi=False):
    dims = (((ca,), (cb,)), ((0,), (0,)))
    dot = lambda x, y: lax.dot_general(x, y, dims, preferred_element_type=F32)
    if not hi:
        return dot(a.astype(BF16), b.astype(BF16))
    a1 = a.astype(BF16)
    a2 = (a - a1.astype(F32)).astype(BF16)
    b1 = b.astype(BF16)
    b2 = (b - b1.astype(F32)).astype(BF16)
    return dot(jnp.concatenate([a1, a1, a2], axis=ca), jnp.concatenate([b1, b2, b1], axis=cb))


def _delta_heads(q, k, kt, v, g_col, beta_col, g_row, s, upper, out):
    nb, lc, _ = q.shape
    ri = lax.broadcasted_iota(I32, (nb, lc, lc), 1)
    ci = lax.broadcasted_iota(I32, (nb, lc, lc), 2)
    tri_incl = (ci >= ri) if upper else (ci <= ri)
    tri_strict = (ci > ri) if upper else (ci < ri)
    eye = jnp.where(ri == ci, 1.0, 0.0)
    k32, q32, v32 = k.astype(F32), q.astype(F32), v.astype(F32)
    eg = jnp.exp(g_col)
    g_last = g_col[:, 0:1] if upper else g_col[:, lc - 1:lc]
    gam = jnp.exp(jnp.where(tri_incl, g_col - g_row, -jnp.inf))
    kq = _bdot(jnp.concatenate([k, q], axis=1), kt, 2, 1)
    yield
    m = jnp.where(tri_strict, kq[:, :lc] * beta_col * gam, 0.0)
    a_qk = kq[:, lc:] * gam
    n = -m
    x = eye + n
    p = _bdot(n, n, 2, 1, hi=True)
    yield
    for _ in range(int(math.log2(lc)) - 2):
        px_pp = _bdot(p, jnp.concatenate([x, p], axis=2), 2, 1, hi=True)
        yield
        x, p = x + px_pp[:, :, :lc], px_pp[:, :, lc:]
    x = x + _bdot(p, x, 2, 1, hi=True)
    yield
    uw = _bdot(x, jnp.concatenate([v32 * beta_col, k32 * (beta_col * eg)], axis=2), 2, 1, hi=True)
    yield
    u, w = uw[:, :, :v.shape[2]], uw[:, :, v.shape[2]:]
    ws_qs = _bdot(jnp.concatenate([w, q32 * eg], axis=1), s, 2, 1)
    yield
    v_new = u - ws_qs[:, :lc]
    o = ws_qs[:, lc:] + _bdot(a_qk, v_new, 2, 1)
    kv = _bdot(kt.astype(F32) * jnp.exp(g_last - g_row), v_new, 2, 1)
    out.append((o, s * jnp.exp(g_last) + kv))


def _delta_scan_kernel(xf_ref, xb_ref, ktf_ref, ktb_ref, gf_ref, gb_ref, gtf_ref, gtb_ref, of_ref, ob_ref, s_ref, *, nh):
    c = pl.program_id(1)

    @pl.when(c == 0)
    def _():
        s_ref[...] = jnp.zeros_like(s_ref)

    results, stages = [], []
    dirs = ((xf_ref, ktf_ref, gf_ref, gtf_ref), (xb_ref, ktb_ref, gb_ref, gtb_ref))
    for d_i, (x_ref, kt_ref, g_ref, gt_ref) in enumerate(dirs):
        gn = g_ref[0]
        gt = gt_ref[0, 0]
        lanes = [2 * d_i * nh + hh for hh in range(nh)]
        g_col = jnp.stack([gn[:, l:l + 1] for l in lanes])
        b_col = jnp.stack([gn[:, l + nh:l + nh + 1] for l in lanes])
        g_row = jnp.stack([gt[l:l + 1, :] for l in lanes])
        out = []
        results.append(out)
        stages.append(_delta_heads(x_ref[0, 0], x_ref[0, 1], kt_ref[0, :, 0], x_ref[0, 2], g_col, b_col, g_row,
                                   s_ref[d_i * nh:(d_i + 1) * nh], d_i == 1, out))
    fwd, bwd = stages
    next(fwd)
    for _ in fwd:
        next(bwd, None)
    for _ in bwd:
        pass
    for d_i, o_ref in enumerate((of_ref, ob_ref)):
        o, s_new = results[d_i][0]
        s_ref[d_i * nh:(d_i + 1) * nh] = s_new
        for hh in range(nh):
            o_ref[0, :, hh * HEAD_DIM:(hh + 1) * HEAD_DIM] = o[hh]


def _delta_scan(xh, kt, gn, gnt, *, n_ctx):
    bsz, _, nh, t, _ = xh.shape
    lc = DELTA_CHUNK
    nc, ncc = t // lc, n_ctx // lc
    gnt4 = gnt.reshape(bsz, LANES, nc, lc).transpose(0, 2, 1, 3)
    rev = lambda c: jnp.where(c < ncc, ncc - 1 - c, nc - 1 - (c - ncc))
    return pl.pallas_call(
        functools.partial(_delta_scan_kernel, nh=nh),
        grid=(bsz, nc),
        in_specs=[pl.BlockSpec((1, 3, nh, lc, HEAD_DIM), lambda b, c: (b, 0, 0, c, 0)),
                  pl.BlockSpec((1, 3, nh, lc, HEAD_DIM), lambda b, c: (b, 0, 0, rev(c), 0)),
                  pl.BlockSpec((1, nh, 1, HEAD_DIM, lc), lambda b, c: (b, 0, c, 0, 0)),
                  pl.BlockSpec((1, nh, 1, HEAD_DIM, lc), lambda b, c: (b, 0, rev(c), 0, 0)),
                  pl.BlockSpec((1, lc, LANES), lambda b, c: (b, c, 0)),
                  pl.BlockSpec((1, lc, LANES), lambda b, c: (b, rev(c), 0)),
                  pl.BlockSpec((1, 1, LANES, lc), lambda b, c: (b, c, 0, 0)),
                  pl.BlockSpec((1, 1, LANES, lc), lambda b, c: (b, rev(c), 0, 0))],
        out_specs=[pl.BlockSpec((1, lc, nh * HEAD_DIM), lambda b, c: (b, c, 0)),
                   pl.BlockSpec((1, lc, nh * HEAD_DIM), lambda b, c: (b, rev(c), 0))],
        out_shape=[jax.ShapeDtypeStruct((bsz, t, nh * HEAD_DIM), F32),
                   jax.ShapeDtypeStruct((bsz, t, nh * HEAD_DIM), F32)],
        scratch_shapes=[pltpu.VMEM((2 * nh, HEAD_DIM, HEAD_DIM), F32)],
        compiler_params=_cparams("arbitrary", "arbitrary"),
        name="delta_scan",
    )(xh, xh, kt, kt, gn, gn, gnt4, gnt4)


def kernel(x, c, ctx, c_ctx, ada_w, ada_b, ln1_g, ln2_g, even_w_in, even_conv_w, even_a_log, even_dt_bias, even_out_norm_g, even_q_norm_g, even_k_norm_g, even_w_out, odd_w_in, odd_rpb, odd_w_out, router_w, expert_w_gate, expert_w_up, expert_w_down, final_norm_g):
    bsz, n_lat, d = x.shape
    n_ctx = ctx.shape[1]
    nct = n_ctx // TILE
    h = (ctx, x)
    rows = jnp.zeros((8, d), F32).at[0].set(c_ctx).at[1:1 + bsz].set(c)

    def mod_of(layer):
        m = _adaln(rows, ada_w, ada_b, layer).reshape(8, 6, d)
        return jnp.stack([jnp.broadcast_to(m[0], (bsz, 6, d)), m[1:1 + bsz]], axis=1)

    mod = mod_of(0)
    a_dim = d // 2
    a_heads = a_dim // HEAD_DIM
    w_in = even_w_in[0]
    o_g = 4 * a_dim
    o_bq = o_g + 4 * a_heads
    kv_dim = (w_in.shape[1] - o_bq - a_dim) // 2
    gates_w = jnp.zeros((d, LANES), F32).at[:, :4 * a_heads].set(w_in[:, o_g:o_bq])
    w_perm = jnp.concatenate([w_in[:, :o_g], w_in[:, o_bq:], gates_w], axis=1).astype(BF16)
    p = _modproj(h, mod, ln1_g[0], w_perm, nct, F32)
    xh, kt, gn, gnt = _delta_prep(p, even_conv_w[0], even_a_log[0], even_dt_bias[0], n_ctx=n_ctx, a_dim=a_dim)
    o_delta = _delta_scan(xh, kt, gn, gnt, n_ctx=n_ctx)
    cos, sin = _rope_tables(n_ctx, n_lat)
    qb, kb, vb = _gqa_prep(p, cos, sin, even_q_norm_g[0], even_k_norm_g[0], a_dim=a_dim, kv_dim=kv_dim)
    o_attn = _gqa_attention(qb, kb, vb, n_ctx=n_ctx)
    h = _outproj_even(h, mod, o_delta, p, o_attn, even_out_norm_g[0], even_w_out[0].astype(BF16), nct, a_dim=a_dim)
    h = _moe(h, mod, ln2_g[0], router_w[0], expert_w_gate, expert_w_up, expert_w_down, 0, final_norm_g,
             n_ctx=n_ctx, route_ctx=True, final_norm=False)

    mod = mod_of(1)
    p = _modproj(h, mod, ln1_g[1], odd_w_in[0].astype(BF16), nct, BF16)
    o_na = _natten(p, _natten_bias(odd_rpb[0]), n_ctx=n_ctx, d=d)
    h_lat = _outproj_odd(h, mod, o_na, odd_w_out[0].astype(BF16), nct)
    return _moe(h_lat, mod, ln2_g[1], router_w[1], expert_w_gate, expert_w_up, expert_w_down, 1, final_norm_g,
                n_ctx=0, route_ctx=False, final_norm=True)
```

```python
import functools
import math

import numpy as np
import jax
import jax.numpy as jnp
from jax import lax
from jax.experimental import pallas as pl
from jax.experimental.pallas import tpu as pltpu

F32, BF16, I32 = jnp.float32, jnp.bfloat16, jnp.int32

HEAD_DIM = 64
GRID_W = 64
DELTA_CHUNK = 64
NA_ROWS = 8
NA_COLS = 16
ROPE_THETA = 10000.0
EC_CAPACITY = 2
EPS = 1e-6
LOG2_E = math.log2(math.e)

LANES = 128
TILE = 256
SUBT = 128
WIN_X = SUBT + 8
WIN_Y = SUBT + 16
VMEM_LIMIT = 56 * 1024 * 1024


def _cparams(*sem):
    return pltpu.CompilerParams(dimension_semantics=sem, vmem_limit_bytes=VMEM_LIMIT)


def _split3(a):
    a1 = a.astype(BF16)
    r1 = a - a1.astype(F32)
    a2 = r1.astype(BF16)
    a3 = (r1 - a2.astype(F32)).astype(BF16)
    return a1, a2, a3


def _dot(a, b):
    return jnp.dot(a, b, preferred_element_type=F32)


def _dot_t(a, b):
    return lax.dot_general(a, b, (((1,), (1,)), ((), ())), preferred_element_type=F32)


def _dot_exact_rhs(a, b_bf16):
    a1, a2, a3 = _split3(a)
    return _dot(a1, b_bf16) + _dot(a2, b_bf16) + _dot(a3, b_bf16)


def _dot_hi(a, b):
    a1 = a.astype(BF16)
    a2 = (a - a1.astype(F32)).astype(BF16)
    b1 = b.astype(BF16)
    b2 = (b - b1.astype(F32)).astype(BF16)
    return _dot(a1, b1) + _dot(a1, b2) + _dot(a2, b1)


def _sigmoid(x):
    return 1.0 / (1.0 + jnp.exp(-x))


def _silu(x):
    return x * _sigmoid(x)


def _modulate(x, g, shift, scale):
    ms = jnp.mean(x * x, axis=-1, keepdims=True)
    return x * lax.rsqrt(ms + EPS) * g * (1.0 + scale) + shift


def _adaln_kernel(s_ref, w_ref, b_ref, o_ref):
    s = _silu(s_ref[...])
    o_ref[...] = _dot_hi(s, w_ref[0]) + b_ref[0]


def _adaln(rows, w, b, layer):
    depth, d, n = w.shape
    tn = n // 4
    return pl.pallas_call(
        _adaln_kernel,
        grid=(n // tn,),
        in_specs=[pl.BlockSpec((8, d), lambda j: (0, 0)),
                  pl.BlockSpec((1, d, tn), lambda j: (layer, 0, j)),
                  pl.BlockSpec((1, 1, tn), lambda j: (layer, 0, j))],
        out_specs=pl.BlockSpec((8, tn), lambda j: (0, j)),
        out_shape=jax.ShapeDtypeStruct((8, n), F32),
        compiler_params=_cparams("arbitrary"),
        name="adaln",
    )(rows, w, b.reshape(depth, 1, n))


def _hidden_operands(h, nct):
    if isinstance(h, tuple):
        hc, hl = h
        d = hc.shape[-1]
        specs = [pl.BlockSpec((1, TILE, d), lambda b, i: (b, jnp.minimum(i, nct - 1), 0)),
                 pl.BlockSpec((1, TILE, d), lambda b, i: (b, jnp.maximum(i - nct, 0), 0))]
        return [hc, hl], specs, hc.shape[1] + hl.shape[1]
    return [h], [pl.BlockSpec((1, TILE, h.shape[-1]), lambda b, i: (b, i, 0))], h.shape[1]


def _hidden_tile(h_refs, nct):
    if len(h_refs) == 2:
        return jnp.where(pl.program_id(1) < nct, h_refs[0][0], h_refs[1][0])
    return h_refs[0][0]


def _modproj_kernel(*refs, n_h, nct):
    mod_ref, g_ref, w_ref, o_ref = refs[n_h:]
    mod = mod_ref[0, 0]
    a = _modulate(_hidden_tile(refs[:n_h], nct), g_ref[...], mod[0:1], mod[1:2])
    o_ref[0] = _dot(a.astype(BF16), w_ref[...]).astype(o_ref.dtype)


def _modproj(h, mod, g, w_bf16, nct, out_dtype):
    h_ops, h_specs, t = _hidden_operands(h, nct)
    bsz, _, d = h_ops[0].shape
    n = w_bf16.shape[1]
    return pl.pallas_call(
        functools.partial(_modproj_kernel, n_h=len(h_ops), nct=nct),
        grid=(bsz, t // TILE),
        in_specs=h_specs + [pl.BlockSpec((1, 1, 6, d), lambda b, i: (b, jnp.where(i < nct, 0, 1), 0, 0)),
                            pl.BlockSpec((1, d), lambda b, i: (0, 0)),
                            pl.BlockSpec((d, n), lambda b, i: (0, 0))],
        out_specs=pl.BlockSpec((1, TILE, n), lambda b, i: (b, i, 0)),
        out_shape=jax.ShapeDtypeStruct((bsz, t, n), out_dtype),
        compiler_params=_cparams("arbitrary", "arbitrary"),
        name="modproj",
    )(*h_ops, mod, g.reshape(1, d), w_bf16)


def _router_kernel(h_ref, mod_ref, g_ref, wr_ref, a_ref, aff_ref, afft_ref, *, n_exp):
    mod = mod_ref[0, 0]
    a = _modulate(h_ref[0], g_ref[...], mod[3:4], mod[4:5])
    a_ref[0] = a.astype(BF16)
    logits = _dot_hi(a, wr_ref[...])
    lane = lax.broadcasted_iota(I32, logits.shape, 1)
    logits = jnp.where(lane < n_exp, logits, -jnp.inf)
    p = jnp.exp(logits - jnp.max(logits, axis=-1, keepdims=True))
    aff = p / jnp.sum(p, axis=-1, keepdims=True)
    aff_ref[0] = aff
    afft_ref[0] = aff.T


def _router(h, mod, g, w_router, nct):
    bsz, t, d = h.shape
    n_exp = w_router.shape[1]
    wr = jnp.zeros((d, LANES), F32).at[:, :n_exp].set(w_router)
    return pl.pallas_call(
        functools.partial(_router_kernel, n_exp=n_exp),
        grid=(bsz, t // TILE),
        in_specs=[pl.BlockSpec((1, TILE, d), lambda b, i: (b, i, 0)),
                  pl.BlockSpec((1, 1, 6, d), lambda b, i: (b, jnp.where(i < nct, 0, 1), 0, 0)),
                  pl.BlockSpec((1, d), lambda b, i: (0, 0)),
                  pl.BlockSpec((d, LANES), lambda b, i: (0, 0))],
        out_specs=[pl.BlockSpec((1, TILE, d), lambda b, i: (b, i, 0)),
                   pl.BlockSpec((1, TILE, LANES), lambda b, i: (b, i, 0)),
                   pl.BlockSpec((1, LANES, TILE), lambda b, i: (b, 0, i))],
        out_shape=[jax.ShapeDtypeStruct((bsz, t, d), BF16),
                   jax.ShapeDtypeStruct((bsz, t, LANES), F32),
                   jax.ShapeDtypeStruct((bsz, LANES, t), F32)],
        compiler_params=_cparams("arbitrary", "arbitrary"),
        name="router",
    )(h, mod, g.reshape(1, d), wr)


def _select_kernel(afft_ref, pos_ref, post_ref, cb_ref, cnt_ref, sel_ref, bucket_ref, *, n_exp, n_ctx, k_ctx, k_lat, t):
    tok = lax.broadcasted_iota(I32, (n_exp, t), 1)
    is_ctx = tok < n_ctx
    min_normal = 0x00800000

    def choose(in_set, k):
        def vals():
            return jnp.where(in_set, afft_ref[0][:n_exp], -1.0)

        def bit_step(it, thr):
            cand = thr | jnp.left_shift(jnp.int32(1), 30 - it)
            cnt = jnp.sum(jnp.where(vals() >= pltpu.bitcast(cand, F32), 1, 0), axis=1, keepdims=True)
            return jnp.where(cnt >= k, cand, thr)
        thr = lax.fori_loop(0, 31, bit_step, jnp.zeros((n_exp, 1), I32))
        lo = pltpu.bitcast(thr, F32)
        hi = pltpu.bitcast(jnp.maximum(thr + 1, min_normal), F32)
        v = vals()
        above = v >= hi
        sel_ref[...] = jnp.where(above, 1.0, sel_ref[...])
        bucket_ref[...] = jnp.where((v >= lo) & jnp.logical_not(above), 1.0, 0.0)
        need0 = k - jnp.sum(jnp.where(above, 1, 0), axis=1, keepdims=True)

        def peel(need):
            in_bucket = bucket_ref[...] > 0.0
            v = vals()
            top = jnp.max(jnp.where(in_bucket, v, -1.0), axis=1, keepdims=True)
            first = jnp.min(jnp.where(in_bucket & (v == top), tok, t), axis=1, keepdims=True)
            pick = (tok == first) & (need > 0)
            sel_ref[...] = jnp.where(pick, 1.0, sel_ref[...])
            bucket_ref[...] = jnp.where(pick, 0.0, bucket_ref[...])
            return need - 1
        lax.while_loop(lambda need: jnp.max(need) > 0, peel, need0)

    sel_ref[...] = jnp.zeros_like(sel_ref)
    choose(jnp.logical_not(is_ctx), k_lat)
    if k_ctx > 0:
        choose(is_ctx, k_ctx)

    rr = lax.broadcasted_iota(I32, (LANES, LANES), 0)
    cc = lax.broadcasted_iota(I32, (LANES, LANES), 1)
    upper = jnp.where(rr <= cc, 1.0, 0.0).astype(BF16)
    carry = jnp.zeros((n_exp, 1), F32)
    for blk in range(t // LANES):
        c = _dot(sel_ref[:, blk * LANES:(blk + 1) * LANES].astype(BF16), upper) + carry
        cnt_ref[:, blk * LANES:(blk + 1) * LANES] = c
        carry = c[:, LANES - 1:LANES]
    posinc = cnt_ref[...].astype(I32)
    pos = jnp.where(sel_ref[...] > 0.0, posinc - 1, -1)
    pos_ref[0] = pos

    lane = lax.broadcasted_iota(I32, (n_exp, LANES), 1)
    cb = jnp.zeros((n_exp, LANES), I32)
    for i in range(1, t // SUBT + 1):
        cb = jnp.where(lane == i, posinc[:, i * SUBT - 1:i * SUBT], cb)
    cb_ref[0] = cb

    padded = jnp.concatenate([pos.astype(F32), jnp.full((LANES - n_exp, t), -1.0, F32)], axis=0)
    for i in range(t // TILE):
        post_ref[0, i * TILE:(i + 1) * TILE, :] = padded[:, i * TILE:(i + 1) * TILE].T.astype(I32)


def _select(afft, n_exp, n_ctx, k_ctx, k_lat):
    bsz, _, t = afft.shape
    assert n_ctx % LANES == 0 and t % TILE == 0 and t // SUBT < LANES
    return pl.pallas_call(
        functools.partial(_select_kernel, n_exp=n_exp, n_ctx=n_ctx, k_ctx=k_ctx, k_lat=k_lat, t=t),
        grid=(bsz,),
        in_specs=[pl.BlockSpec((1, LANES, t), lambda b: (b, 0, 0))],
        out_specs=[pl.BlockSpec((1, n_exp, t), lambda b: (b, 0, 0)),
                   pl.BlockSpec((1, t, LANES), lambda b: (b, 0, 0)),
                   pl.BlockSpec((1, n_exp, LANES), lambda b: (b, 0, 0))],
        out_shape=[jax.ShapeDtypeStruct((bsz, n_exp, t), I32),
                   jax.ShapeDtypeStruct((bsz, t, LANES), I32),
                   jax.ShapeDtypeStruct((bsz, n_exp, LANES), I32)],
        scratch_shapes=[pltpu.VMEM((n_exp, t), F32)] * 3,
        compiler_params=_cparams("arbitrary"),
        name="select",
    )(afft)


def _ffn_rows(cap_tot):
    for rc in range(min(cap_tot, 512), 7, -8):
        if cap_tot % rc == 0:
            return rc
    raise ValueError(cap_tot)


def _expert_kernel(cb_ref, pos_ref, a_ref, wg_ref, wu_ref, wd_ref, y_ref, x_ref, *, n_exp, nsub, sub_per, cap_tot):
    e, b, ch = pl.program_id(0), pl.program_id(1), pl.program_id(2)

    @pl.when(ch == 0)
    def _():
        x_ref[...] = jnp.zeros_like(x_ref)

    base = (b * n_exp + e) * (nsub + 1) + ch * sub_per
    row = lax.broadcasted_iota(I32, (WIN_X, SUBT), 0)

    def gather(j, carry):
        start = cb_ref[base + j]
        start_al = pl.multiple_of((start // 8) * 8, 8)
        rel = pos_ref[0, 0, j] - start_al
        onehot = jnp.where(row == rel, 1.0, 0.0).astype(BF16)
        tok0 = pl.multiple_of(j * SUBT, SUBT)
        x_ref[pl.ds(start_al, WIN_X), :] += _dot(onehot, a_ref[0, pl.ds(tok0, SUBT), :])
        return carry
    lax.fori_loop(0, sub_per, gather, 0, unroll=True)

    @pl.when(ch == pl.num_programs(2) - 1)
    def _():
        rc = _ffn_rows(cap_tot)
        wg = wg_ref[0, 0].astype(BF16)
        wu = wu_ref[0, 0].astype(BF16)
        wd = wd_ref[0, 0].astype(BF16)
        for r0 in range(0, cap_tot, rc):
            xc = x_ref[r0:r0 + rc, :].astype(BF16)
            hid = _silu(_dot(xc, wg)) * _dot(xc, wu)
            y_ref[0, 0, r0:r0 + rc, :] = _dot(hid.astype(BF16), wd).astype(y_ref.dtype)
        y_ref[0, 0, cap_tot:, :] = jnp.zeros((y_ref.shape[2] - cap_tot, y_ref.shape[-1]), y_ref.dtype)


def _experts(cb_flat, pos, a, wg, wu, wd, layer, cap_tot):
    bsz, n_exp, t = pos.shape
    d = a.shape[-1]
    ff = wg.shape[-1]
    nsub = t // SUBT
    ntiles = t // TILE
    chunk = TILE * max(m for m in range(1, 13) if ntiles % m == 0)
    sub_per = chunk // SUBT
    assert cap_tot % 16 == 0
    rows_x = cap_tot + WIN_X
    rows_y = cap_tot + WIN_Y
    pos5 = pos.reshape(bsz, n_exp, nsub, 1, SUBT)
    grid_spec = pltpu.PrefetchScalarGridSpec(
        num_scalar_prefetch=1,
        grid=(n_exp, bsz, t // chunk),
        in_specs=[pl.BlockSpec((1, 1, sub_per, 1, SUBT), lambda e, b, c, cb: (b, e, c, 0, 0)),
                  pl.BlockSpec((1, chunk, d), lambda e, b, c, cb: (b, c, 0)),
                  pl.BlockSpec((1, 1, d, ff), lambda e, b, c, cb: (layer, e, 0, 0)),
                  pl.BlockSpec((1, 1, d, ff), lambda e, b, c, cb: (layer, e, 0, 0)),
                  pl.BlockSpec((1, 1, ff, d), lambda e, b, c, cb: (layer, e, 0, 0))],
        out_specs=pl.BlockSpec((1, 1, rows_y, d), lambda e, b, c, cb: (b, e, 0, 0)),
        scratch_shapes=[pltpu.VMEM((rows_x, d), F32)],
    )
    return pl.pallas_call(
        functools.partial(_expert_kernel, n_exp=n_exp, nsub=nsub, sub_per=sub_per, cap_tot=cap_tot),
        grid_spec=grid_spec,
        out_shape=jax.ShapeDtypeStruct((bsz, n_exp, rows_y, d), BF16),
        compiler_params=_cparams("arbitrary", "arbitrary", "arbitrary"),
        name="experts",
    )(cb_flat, pos5, a, wg, wu, wd)


def _combine(cb_flat, h, mod, aff, post, y, g_final, nct, final_norm):
    bsz, t, d = h.shape
    n_exp = y.shape[1]
    ntiles = t // TILE
    first = nct if final_norm else 0
    n_win = n_exp * (TILE // SUBT)
    grid_spec = pltpu.PrefetchScalarGridSpec(
        num_scalar_prefetch=1,
        grid=(bsz, ntiles - first),
        in_specs=[pl.BlockSpec((1, TILE, d), lambda b, i, cb: (b, i + first, 0)),
                  pl.BlockSpec((1, 1, 6, d), lambda b, i, cb: (b, jnp.where(i + first < nct, 0, 1), 0, 0)),
                  pl.BlockSpec((1, TILE, LANES), lambda b, i, cb: (b, i + first, 0)),
                  pl.BlockSpec((1, TILE, LANES), lambda b, i, cb: (b, i + first, 0)),
                  pl.BlockSpec(memory_space=pl.ANY),
                  pl.BlockSpec((1, d), lambda b, i, cb: (0, 0))],
        out_specs=pl.BlockSpec((1, TILE, d), lambda b, i, cb: (b, i, 0)),
        scratch_shapes=[pltpu.VMEM((2, n_win, WIN_Y, d), y.dtype), pltpu.SemaphoreType.DMA((2, n_win))],
    )
    return pl.pallas_call(
        functools.partial(_combine_kernel, n_exp=n_exp, nsub=t // SUBT, first=first, final_norm=final_norm),
        grid_spec=grid_spec,
        out_shape=jax.ShapeDtypeStruct((bsz, t - first * TILE, d), F32),
        compiler_params=_cparams("arbitrary", "arbitrary"),
        name="combine",
    )(cb_flat, h, mod, aff, post, y, g_final.reshape(1, d))


def _combine_kernel(cb_ref, h_ref, mod_ref, aff_ref, post_ref, y_hbm, g_ref, o_ref, ybuf, sem,
                    *, n_exp, nsub, first, final_norm):
    n_i = pl.num_programs(1)
    step = pl.program_id(0) * n_i + pl.program_id(1)
    n_steps = pl.num_programs(0) * n_i
    per_tile = TILE // SUBT

    def window(at_step, sub, e):
        b = at_step // n_i
        i = at_step - b * n_i + first
        start = cb_ref[(b * n_exp + e) * (nsub + 1) + i * per_tile + sub]
        start_al = pl.multiple_of((start // 16) * 16, 16)
        buf, slot = at_step % 2, sub * n_exp + e
        copy = pltpu.make_async_copy(y_hbm.at[b, e, pl.ds(start_al, WIN_Y), :], ybuf.at[buf, slot],
                                     sem.at[buf, slot])
        return copy, start_al

    def start_windows(at_step):
        for sub in range(per_tile):
            for e in range(n_exp):
                window(at_step, sub, e)[0].start()

    @pl.when(step == 0)
    def _():
        start_windows(step)

    @pl.when(step + 1 < n_steps)
    def _():
        start_windows(step + 1)

    lane = lax.broadcasted_iota(I32, (SUBT, WIN_Y), 1)
    mod = mod_ref[0, 0]
    for sub in range(per_tile):
        rows = slice(sub * SUBT, (sub + 1) * SUBT)
        post = post_ref[0, rows, :]
        aff = aff_ref[0, rows, :]
        acc = jnp.zeros((SUBT, o_ref.shape[-1]), F32)
        starts = []
        for e in range(n_exp):
            copy, start_al = window(step, sub, e)
            copy.wait()
            starts.append(start_al)
        for e, start_al in enumerate(starts):
            rel = post[:, e:e + 1] - start_al
            onehot = jnp.where(lane == rel, 1.0, 0.0).astype(BF16)
            acc = acc + aff[:, e:e + 1] * _dot(onehot, ybuf[step % 2, sub * n_exp + e])
        out = h_ref[0, rows, :] + mod[5:6] * acc
        if final_norm:
            ms = jnp.mean(out * out, axis=-1, keepdims=True)
            out = out * lax.rsqrt(ms + EPS) * g_ref[...]
        o_ref[0, rows, :] = out


def _moe(h, mod, ln2_g, w_router, wg, wu, wd, layer, g_final, *, n_ctx, route_ctx, final_norm):
    bsz, t, d = h.shape
    n_exp = w_router.shape[1]
    nct = n_ctx // TILE
    n_lat = t - n_ctx
    k_lat = max(1, EC_CAPACITY * n_lat // n_exp)
    k_ctx = max(1, EC_CAPACITY * n_ctx // n_exp) if route_ctx else 0
    a, aff, afft = _router(h, mod, ln2_g, w_router, nct)
    pos, post, cb = _select(afft, n_exp, n_ctx, k_ctx, k_lat)
    cb_flat = cb[:, :, :t // SUBT + 1].reshape(-1)
    y = _experts(cb_flat, pos, a, wg, wu, wd, layer, k_ctx + k_lat)
    return _combine(cb_flat, h, mod, aff, post, y, g_final, nct, final_norm)


HEAD_SHIFT = int(math.log2(HEAD_DIM))


def _same_head(rows, cols):
    r = lax.shift_right_logical(lax.broadcasted_iota(I32, (rows, cols), 0), HEAD_SHIFT)
    c = lax.shift_right_logical(lax.broadcasted_iota(I32, (rows, cols), 1), HEAD_SHIFT)
    return jnp.where(r == c, 1.0, 0.0)


def _head_sum(x2):
    n = x2.shape[-1]
    return _dot_exact_rhs(x2, _same_head(n, n).astype(BF16))


def _outproj_even_kernel(*refs, n_h, nct, a_dim):
    mod_ref, of_ref, ob_ref, z_ref, at_ref, g_ref, w_ref, o_ref = refs[n_h:]
    mod = mod_ref[0, 0]
    o = of_ref[0] + ob_ref[0]
    ms = _head_sum(o * o) * (1.0 / HEAD_DIM)
    ya = o * lax.rsqrt(ms + EPS) * g_ref[...] * _silu(z_ref[0])
    y = _dot(ya.astype(BF16), w_ref[:a_dim, :]) + _dot(at_ref[0], w_ref[a_dim:, :])
    o_ref[0] = _hidden_tile(refs[:n_h], nct) + mod[2:3] * y


def _outproj_even(h, mod, o_delta, p, o_attn, out_norm_g, w_bf16, nct, *, a_dim):
    h_ops, h_specs, t = _hidden_operands(h, nct)
    bsz, _, d = h_ops[0].shape
    o_f, o_b = o_delta
    g = jnp.tile(out_norm_g, a_dim // HEAD_DIM).reshape(1, a_dim)
    tok = lambda b, i: (b, i, 0)
    return pl.pallas_call(
        functools.partial(_outproj_even_kernel, n_h=len(h_ops), nct=nct, a_dim=a_dim),
        grid=(bsz, t // TILE),
        in_specs=h_specs + [pl.BlockSpec((1, 1, 6, d), lambda b, i: (b, jnp.where(i < nct, 0, 1), 0, 0)),
                            pl.BlockSpec((1, TILE, a_dim), tok),
                            pl.BlockSpec((1, TILE, a_dim), tok),
                            pl.BlockSpec((1, TILE, a_dim), lambda b, i: (b, i, 3)),
                            pl.BlockSpec((1, TILE, a_dim), tok),
                            pl.BlockSpec((1, a_dim), lambda b, i: (0, 0)),
                            pl.BlockSpec((d, d), lambda b, i: (0, 0))],
        out_specs=pl.BlockSpec((1, TILE, d), tok),
        out_shape=jax.ShapeDtypeStruct((bsz, t, d), F32),
        compiler_params=_cparams("arbitrary", "arbitrary"),
        name="outproj_even",
    )(*h_ops, mod, o_f, o_b, p, o_attn, g, w_bf16)


def _outproj_odd_kernel(h_ref, mod_ref, a_ref, w_ref, o_ref):
    mod = mod_ref[0, 0]
    o_ref[0] = h_ref[0] + mod[2:3] * _dot(a_ref[0], w_ref[...])


def _outproj_odd(h, mod, o_na, w_bf16, nct):
    bsz, t, d = h.shape
    n_lat = o_na.shape[1]
    return pl.pallas_call(
        _outproj_odd_kernel,
        grid=(bsz, n_lat // TILE),
        in_specs=[pl.BlockSpec((1, TILE, d), lambda b, i: (b, i + nct, 0)),
                  pl.BlockSpec((1, 1, 6, d), lambda b, i: (b, 1, 0, 0)),
                  pl.BlockSpec((1, TILE, d), lambda b, i: (b, i, 0)),
                  pl.BlockSpec((d, d), lambda b, i: (0, 0))],
        out_specs=pl.BlockSpec((1, TILE, d), lambda b, i: (b, i, 0)),
        out_shape=jax.ShapeDtypeStruct((bsz, n_lat, d), F32),
        compiler_params=_cparams("arbitrary", "arbitrary"),
        name="outproj_odd",
    )(h, mod, o_na, w_bf16)


def _rope_tables(n_ctx, n_lat):
    tt = jnp.arange(n_lat)
    n_freq = HEAD_DIM // 4
    inv = ROPE_THETA ** (-jnp.arange(n_freq, dtype=F32) / n_freq)
    ang = jnp.concatenate([(tt // GRID_W).astype(F32)[:, None] * inv, (tt % GRID_W).astype(F32)[:, None] * inv], -1)
    cos = jnp.concatenate([jnp.ones((n_ctx, 2 * n_freq), F32), jnp.cos(ang)], axis=0)
    sin = jnp.concatenate([jnp.zeros((n_ctx, 2 * n_freq), F32), jnp.sin(ang)], axis=0)
    reps = LANES // HEAD_DIM
    return jnp.tile(cos, (1, 2 * reps)), jnp.tile(jnp.concatenate([-sin, sin], axis=1), (1, reps))


def _rope(x, cos, sin):
    n = x.shape[-1]
    half = HEAD_DIM // 2
    lane = lax.broadcasted_iota(I32, x.shape, 1)
    first = (lane & (HEAD_DIM - 1)) < half
    partner = jnp.where(first, pltpu.roll(x, n - half, 1), pltpu.roll(x, half, 1))
    reps = n // LANES
    cos_n = jnp.concatenate([cos] * reps, axis=1) if reps > 1 else cos
    sin_n = jnp.concatenate([sin] * reps, axis=1) if reps > 1 else sin
    return x * cos_n + partner * sin_n


def _gqa_prep_kernel(q_ref, k_ref, v_ref, cos_ref, sin_ref, gq_ref, gk_ref, qo_ref, ko_ref, vo_ref):
    cos, sin = cos_ref[...], sin_ref[...]
    q = q_ref[0]
    q = q * lax.rsqrt(_head_sum(q * q) * (1.0 / HEAD_DIM) + EPS) * gq_ref[...] * (HEAD_DIM ** -0.5 * LOG2_E)
    qo_ref[0] = _rope(q, cos, sin).astype(BF16)
    k = k_ref[0]
    k = k * lax.rsqrt(_head_sum(k * k) * (1.0 / HEAD_DIM) + EPS) * gk_ref[...]
    k = _rope(k, cos, sin).astype(BF16)
    v = v_ref[0].astype(BF16)
    for hh in range(k.shape[-1] // HEAD_DIM):
        ko_ref[0, hh] = k[:, hh * HEAD_DIM:(hh + 1) * HEAD_DIM]
        vo_ref[0, hh] = v[:, hh * HEAD_DIM:(hh + 1) * HEAD_DIM]


def _gqa_prep(p, cos, sin, q_norm_g, k_norm_g, *, a_dim, kv_dim):
    bsz, t, _ = p.shape
    q_dim = a_dim
    kvh = kv_dim // HEAD_DIM
    assert kv_dim == LANES
    c_q = (4 * a_dim) // q_dim
    c_k = (4 * a_dim + q_dim) // kv_dim
    gq = jnp.tile(q_norm_g, q_dim // HEAD_DIM).reshape(1, q_dim)
    gk = jnp.tile(k_norm_g, kvh).reshape(1, kv_dim)
    return pl.pallas_call(
        _gqa_prep_kernel,
        grid=(bsz, t // TILE),
        in_specs=[pl.BlockSpec((1, TILE, q_dim), lambda b, i: (b, i, c_q)),
                  pl.BlockSpec((1, TILE, kv_dim), lambda b, i: (b, i, c_k)),
                  pl.BlockSpec((1, TILE, kv_dim), lambda b, i: (b, i, c_k + 1)),
                  pl.BlockSpec((TILE, LANES), lambda b, i: (i, 0)),
                  pl.BlockSpec((TILE, LANES), lambda b, i: (i, 0)),
                  pl.BlockSpec((1, q_dim), lambda b, i: (0, 0)),
                  pl.BlockSpec((1, kv_dim), lambda b, i: (0, 0))],
        out_specs=[pl.BlockSpec((1, TILE, q_dim), lambda b, i: (b, i, 0)),
                   pl.BlockSpec((1, kvh, TILE, HEAD_DIM), lambda b, i: (b, 0, i, 0)),
                   pl.BlockSpec((1, kvh, TILE, HEAD_DIM), lambda b, i: (b, 0, i, 0))],
        out_shape=[jax.ShapeDtypeStruct((bsz, t, q_dim), BF16),
                   jax.ShapeDtypeStruct((bsz, kvh, t, HEAD_DIM), BF16),
                   jax.ShapeDtypeStruct((bsz, kvh, t, HEAD_DIM), BF16)],
        compiler_params=_cparams("arbitrary", "arbitrary"),
        name="gqa_prep",
    )(p, p, p, cos, sin, gq, gk)


SUB_K = 256
Q_SLAB = 256


def _gqa_kernel(q_ref, k_ref, v_ref, o_ref, qs_ref, m_ref, l_ref, acc_ref, *, n_ctx, nct, tq, tk, n_heads, kvh):
    i, kv = pl.program_id(1), pl.program_id(2)
    total = n_heads * tq
    per_kv = total // kvh

    @pl.when(kv == 0)
    def _():
        m_ref[...] = jnp.full_like(m_ref, -jnp.inf)
        l_ref[...] = jnp.zeros_like(l_ref)
        acc_ref[...] = jnp.zeros_like(acc_ref)
        q = q_ref[0]
        for hh in range(n_heads):
            qs_ref[hh * tq:(hh + 1) * tq, :] = q[:, hh * HEAD_DIM:(hh + 1) * HEAD_DIM]

    def attend(n_keys, n_valid):
        edges = ([0, Q_SLAB // 2] + list(range(Q_SLAB + Q_SLAB // 2, total - Q_SLAB // 2, Q_SLAB))
                 + [total - Q_SLAB // 2, total])
        edges = sorted(set(edges) | set(range(0, total + 1, per_kv)))
        slabs = [slice(a, b) for a, b in zip(edges[:-1], edges[1:])]
        keys = lambda rows: k_ref[0, rows.start // per_kv, :n_keys, :]
        vals = lambda rows: v_ref[0, rows.start // per_kv, :n_keys, :]
        scores = {0: _dot_t(qs_ref[slabs[0], :], keys(slabs[0]))}
        for idx, rows in enumerate(slabs):
            if idx + 1 < len(slabs):
                scores[idx + 1] = _dot_t(qs_ref[slabs[idx + 1], :], keys(slabs[idx + 1]))
            s = scores.pop(idx)
            if n_valid is not None:
                s = jnp.where(lax.broadcasted_iota(I32, s.shape, 1) < n_valid, s, -jnp.inf)
            m_old = m_ref[rows, :]
            m_new = jnp.maximum(m_old, jnp.max(s, axis=-1, keepdims=True))
            alpha = jnp.exp2(m_old - m_new)
            p = jnp.exp2(s - m_new)
            l_ref[rows, :] = alpha * l_ref[rows, :] + jnp.sum(p, axis=-1, keepdims=True)
            acc_ref[rows, :] = alpha * acc_ref[rows, :] + _dot(p.astype(BF16), vals(rows))
            m_ref[rows, :] = m_new

    @pl.when(i >= nct)
    def _():
        attend(tk, None)

    @pl.when((i < nct) & (kv == 0))
    def _():
        n_keys = -(-n_ctx // SUB_K) * SUB_K
        attend(n_keys, n_ctx if n_keys != n_ctx else None)

    @pl.when(kv == pl.num_programs(2) - 1)
    def _():
        out = acc_ref[...] / l_ref[...]
        for hh in range(n_heads):
            o_ref[0, :, hh * HEAD_DIM:(hh + 1) * HEAD_DIM] = out[hh * tq:(hh + 1) * tq, :].astype(o_ref.dtype)


def _gqa_attention(q, k, v, *, n_ctx):
    bsz, t, q_dim = q.shape
    kvh = k.shape[1]
    n_heads = q_dim // HEAD_DIM
    tq = TILE
    nt = t // TILE
    tk = TILE * max(m for m in range(1, 13) if nt % m == 0)
    assert n_ctx <= tk and tk % SUB_K == 0 and (n_heads // kvh * tq) % Q_SLAB == 0
    return pl.pallas_call(
        functools.partial(_gqa_kernel, n_ctx=n_ctx, nct=n_ctx // tq, tq=tq, tk=tk, n_heads=n_heads, kvh=kvh),
        grid=(bsz, t // tq, t // tk),
        in_specs=[pl.BlockSpec((1, tq, q_dim), lambda b, i, j: (b, i, 0)),
                  pl.BlockSpec((1, kvh, tk, HEAD_DIM), lambda b, i, j: (b, 0, j, 0)),
                  pl.BlockSpec((1, kvh, tk, HEAD_DIM), lambda b, i, j: (b, 0, j, 0))],
        out_specs=pl.BlockSpec((1, tq, q_dim), lambda b, i, j: (b, i, 0)),
        out_shape=jax.ShapeDtypeStruct((bsz, t, q_dim), BF16),
        scratch_shapes=[pltpu.VMEM((n_heads * tq, HEAD_DIM), BF16),
                        pltpu.VMEM((n_heads * tq, 1), F32),
                        pltpu.VMEM((n_heads * tq, 1), F32),
                        pltpu.VMEM((n_heads * tq, HEAD_DIM), F32)],
        compiler_params=_cparams("arbitrary", "arbitrary", "arbitrary"),
        name="gqa_attention",
    )(q, k, v)


def _natten_bias(rpb):
    n_heads, n_off, n_rel = rpb.shape
    col = np.arange(GRID_W)
    col_start = np.clip(col - NA_COLS // 2, 0, GRID_W - NA_COLS)
    valid = (col[None, :] >= col_start[:, None]) & (col[None, :] < col_start[:, None] + NA_COLS)
    rel = col[None, :] - col[:, None] + NA_COLS - 1
    onehot = ((rel[None] == np.arange(n_rel)[:, None, None]) & valid[None]).astype(np.float32)
    tab = jnp.einsum('hrk,kcd->hrcd', rpb, jnp.asarray(onehot), precision=lax.Precision.HIGHEST)
    tab = jnp.where(valid[None, None], tab, -jnp.inf)
    pairs = jnp.concatenate([tab[:, :-1], tab[:, 1:]], axis=-1)
    return pairs.reshape(n_heads // 2, 2, n_off - 1, GRID_W, 2 * GRID_W).astype(F32)


def _natten_kernel(*refs, n_ctx, n_rows, rblk, n_q):
    q_refs = refs[:n_q]
    k_ref, v_ref, bias_ref, o_ref = refs[n_q:]
    per_q = TILE // GRID_W
    rb = pl.program_id(2)
    first_head = lax.broadcasted_iota(I32, (1, LANES), 1) < HEAD_DIM
    kc = k_ref[0, 0:n_ctx, :]
    vc = v_ref[0, 0:n_ctx, :]
    band = NA_ROWS * GRID_W
    qs, ks, vs, biases = [], [], [], []
    for rr in range(rblk):
        rg = rb * rblk + rr
        rs = jnp.clip(rg - NA_ROWS // 2, 0, n_rows - NA_ROWS)
        variant = rs - rg + (NA_ROWS - 1)
        start = pl.multiple_of(n_ctx + rs * GRID_W, GRID_W)
        keys = jnp.concatenate([kc, k_ref[0, pl.ds(start, band), :]], axis=0)
        vals = jnp.concatenate([vc, v_ref[0, pl.ds(start, band), :]], axis=0)
        q_lo = (rr % per_q) * GRID_W
        q = q_refs[rr // per_q][0, q_lo:q_lo + GRID_W, :] * (HEAD_DIM ** -0.5)
        qs.append(jnp.concatenate([jnp.where(first_head, q, jnp.zeros_like(q)),
                                   jnp.where(first_head, jnp.zeros_like(q), q)], axis=0))
        ks.append(keys)
        vs.append(vals)
        biases.append(jnp.concatenate(
            [jnp.concatenate([bias_ref[0, hh, variant + j] for j in range(0, NA_ROWS, 2)], axis=1)
             for hh in range(2)], axis=0))
    batch = (((2,), (2,)), ((0,), (0,)))
    s = lax.dot_general(jnp.stack(qs), jnp.stack(ks), batch, preferred_element_type=F32)
    sc = s[:, :, :n_ctx]
    sb = s[:, :, n_ctx:] + jnp.stack(biases)
    m = jnp.maximum(jnp.max(sc, axis=-1, keepdims=True), jnp.max(sb, axis=-1, keepdims=True))
    pc = jnp.exp(sc - m)
    pb = jnp.exp(sb - m)
    l = jnp.sum(pc, axis=-1, keepdims=True) + jnp.sum(pb, axis=-1, keepdims=True)
    p = jnp.concatenate([pc, pb], axis=-1).astype(BF16)
    o = lax.dot_general(p, jnp.stack(vs), (((2,), (1,)), ((0,), (0,))), preferred_element_type=F32) / l
    for rr in range(rblk):
        o_ref[0, rr * GRID_W:(rr + 1) * GRID_W, :] = jnp.where(first_head, o[rr, :GRID_W], o[rr, GRID_W:]).astype(o_ref.dtype)


def _natten(p, bias, *, n_ctx, d):
    bsz, t, _ = p.shape
    n_lat = t - n_ctx
    n_rows = n_lat // GRID_W
    n_q = 2 if n_lat % (2 * TILE) == 0 else 1
    rblk = n_q * TILE // GRID_W
    npairs = d // LANES
    nct = n_ctx // TILE
    assert n_rows >= NA_ROWS and n_ctx % TILE == 0
    return pl.pallas_call(
        functools.partial(_natten_kernel, n_ctx=n_ctx, n_rows=n_rows, rblk=rblk, n_q=n_q),
        grid=(bsz, npairs, n_lat // (n_q * TILE)),
        in_specs=[pl.BlockSpec((1, TILE, LANES), lambda b, hp, r, j=j: (b, nct + n_q * r + j, hp)) for j in range(n_q)] + [
                  pl.BlockSpec((1, t, LANES), lambda b, hp, r: (b, 0, npairs + hp)),
                  pl.BlockSpec((1, t, LANES), lambda b, hp, r: (b, 0, 2 * npairs + hp)),
                  pl.BlockSpec((1, 2, 2 * NA_ROWS - 2, GRID_W, 2 * GRID_W), lambda b, hp, r: (hp, 0, 0, 0, 0))],
        out_specs=pl.BlockSpec((1, n_q * TILE, LANES), lambda b, hp, r: (b, r, hp)),
        out_shape=jax.ShapeDtypeStruct((bsz, n_lat, d), BF16),
        compiler_params=_cparams("arbitrary", "arbitrary", "arbitrary"),
        name="natten",
    )(*([p] * n_q), p, p, bias)


def _dot_exact_lhs(a_bf16, b):
    b1, b2, b3 = _split3(b)
    return _dot(a_bf16, b1) + _dot(a_bf16, b2) + _dot(a_bf16, b3)


def _delta_prep_kernel(x_ref, prev_ref, next_ref, gate_ref, w_ref, alog_ref, dt_ref, xh_ref, kt_ref, gn_ref, gnt_ref,
                       xs_ref,
                       *, nct, a_dim):
    i = pl.program_id(1)
    last = pl.num_programs(1) - 1
    x = x_ref[0]
    rows = x.shape[0]
    xs_ref[0:8, :] = prev_ref[0]
    xs_ref[8:8 + rows, :] = x
    xs_ref[8 + rows:16 + rows, :] = next_ref[0]
    r = lax.broadcasted_iota(I32, (rows, 1), 0)
    at_start = (i == 0) | (i == nct)
    at_end = (i == nct - 1) | (i == last)
    xm = jnp.where((r == 0) & at_start, 0.0, xs_ref[7:7 + rows, :])
    xp = jnp.where((r == rows - 1) & at_end, 0.0, xs_ref[9:9 + rows, :])
    w = w_ref[...]
    y = _silu(w[0:1] * xm + w[1:2] * x + w[2:3] * xp)
    q, k, v = y[:, :a_dim], y[:, a_dim:2 * a_dim], y[:, 2 * a_dim:]
    nh = a_dim // HEAD_DIM
    q = (q * lax.rsqrt(_head_sum(q * q) + EPS) * (HEAD_DIM ** -0.5)).astype(BF16)
    k = k * lax.rsqrt(_head_sum(k * k) + EPS)
    k_t = k.T.astype(BF16)
    for hh in range(nh):
        for cc in range(rows // DELTA_CHUNK):
            kt_ref[0, hh, cc] = k_t[hh * HEAD_DIM:(hh + 1) * HEAD_DIM, cc * DELTA_CHUNK:(cc + 1) * DELTA_CHUNK]
    k = k.astype(BF16)
    v = v.astype(BF16)
    for kind, arr in enumerate((q, k, v)):
        for hh in range(nh):
            xh_ref[0, kind, hh] = arr[:, hh * HEAD_DIM:(hh + 1) * HEAD_DIM]

    g = gate_ref[0]
    z = g + dt_ref[...]
    softplus = jnp.maximum(z, 0.0) + jnp.log1p(jnp.exp(-jnp.abs(z)))
    log_a = -jnp.exp(alog_ref[...]) * softplus
    beta = _sigmoid(g)
    ri = lax.broadcasted_iota(I32, (rows, rows), 0)
    ci = lax.broadcasted_iota(I32, (rows, rows), 1)
    chunk_shift = int(math.log2(DELTA_CHUNK))
    same_chunk = lax.shift_right_logical(ri, chunk_shift) == lax.shift_right_logical(ci, chunk_shift)
    prefix = jnp.where(same_chunk & (ci <= ri), 1.0, 0.0).astype(BF16)
    suffix = jnp.where(same_chunk & (ci >= ri), 1.0, 0.0).astype(BF16)
    g_f = _dot_exact_lhs(prefix, log_a)
    g_b = _dot_exact_lhs(suffix, log_a)
    lane = lax.broadcasted_iota(I32, g.shape, 1)
    narrow = jnp.where(lane < nh, g_f, jnp.where(lane < 2 * nh, beta, jnp.where(lane < 3 * nh, g_b, beta)))
    gn_ref[0] = narrow
    gnt_ref[0] = narrow.T


def _delta_prep(p, conv_w, a_log, dt_bias, *, n_ctx, a_dim):
    bsz, t, _ = p.shape
    nh = a_dim // HEAD_DIM
    assert DELTA_CHUNK == HEAD_DIM and TILE % DELTA_CHUNK == 0 and 4 * nh <= LANES
    nct = n_ctx // TILE
    c_gate = (4 * a_dim + a_dim + 2 * LANES) // LANES
    alog = jnp.zeros((1, LANES), F32).at[0, :nh].set(a_log[0]).at[0, 2 * nh:3 * nh].set(a_log[1])
    dtb = jnp.zeros((1, LANES), F32).at[0, :nh].set(dt_bias[0]).at[0, 2 * nh:3 * nh].set(dt_bias[1])
    nblk8 = t // 8
    per = TILE // 8
    return pl.pallas_call(
        functools.partial(_delta_prep_kernel, nct=nct, a_dim=a_dim),
        grid=(bsz, t // TILE),
        in_specs=[pl.BlockSpec((1, TILE, 3 * a_dim), lambda b, i: (b, i, 0)),
                  pl.BlockSpec((1, 8, 3 * a_dim), lambda b, i: (b, jnp.maximum(i * per - 1, 0), 0)),
                  pl.BlockSpec((1, 8, 3 * a_dim), lambda b, i: (b, jnp.minimum((i + 1) * per, nblk8 - 1), 0)),
                  pl.BlockSpec((1, TILE, LANES), lambda b, i: (b, i, c_gate)),
                  pl.BlockSpec((3, 3 * a_dim), lambda b, i: (0, 0)),
                  pl.BlockSpec((1, LANES), lambda b, i: (0, 0)),
                  pl.BlockSpec((1, LANES), lambda b, i: (0, 0))],
        out_specs=[pl.BlockSpec((1, 3, nh, TILE, HEAD_DIM), lambda b, i: (b, 0, 0, i, 0)),
                   pl.BlockSpec((1, nh, TILE // DELTA_CHUNK, HEAD_DIM, DELTA_CHUNK), lambda b, i: (b, 0, i, 0, 0)),
                   pl.BlockSpec((1, TILE, LANES), lambda b, i: (b, i, 0)),
                   pl.BlockSpec((1, LANES, TILE), lambda b, i: (b, 0, i))],
        out_shape=[jax.ShapeDtypeStruct((bsz, 3, nh, t, HEAD_DIM), BF16),
                   jax.ShapeDtypeStruct((bsz, nh, t // DELTA_CHUNK, HEAD_DIM, DELTA_CHUNK), BF16),
                   jax.ShapeDtypeStruct((bsz, t, LANES), F32),
                   jax.ShapeDtypeStruct((bsz, LANES, t), F32)],
        scratch_shapes=[pltpu.VMEM((TILE + 16, 3 * a_dim), F32)],
        compiler_params=_cparams("arbitrary", "arbitrary"),
        name="delta_prep",
    )(p, p, p, p, conv_w, alog, dtb)


def _bdot(a, b, ca, cb, hi=False):
    dims = (((ca,), (cb,)), ((0,), (0,)))
    dot = lambda x, y: lax.dot_general(x, y, dims, preferred_element_type=F32)
    if not hi:
        return dot(a.astype(BF16), b.astype(BF16))
    a1 = a.astype(BF16)
    a2 = (a - a1.astype(F32)).astype(BF16)
    b1 = b.astype(BF16)
    b2 = (b - b1.astype(F32)).astype(BF16)
    return dot(jnp.concatenate([a1, a1, a2], axis=ca), jnp.concatenate([b1, b2, b1], axis=cb))


def _delta_heads(q, k, kt, v, g_col, beta_col, g_row, s, upper, out):
    nb, lc, _ = q.shape
    ri = lax.broadcasted_iota(I32, (nb, lc, lc), 1)
    ci = lax.broadcasted_iota(I32, (nb, lc, lc), 2)
    tri_incl = (ci >= ri) if upper else (ci <= ri)
    tri_strict = (ci > ri) if upper else (ci < ri)
    eye = jnp.where(ri == ci, 1.0, 0.0)
    k32, q32, v32 = k.astype(F32), q.astype(F32), v.astype(F32)
    eg = jnp.exp(g_col)
    g_last = g_col[:, 0:1] if upper else g_col[:, lc - 1:lc]
    gam = jnp.exp(jnp.where(tri_incl, g_col - g_row, -jnp.inf))
    kq = _bdot(jnp.concatenate([k, q], axis=1), kt, 2, 1)
    yield
    m = jnp.where(tri_strict, kq[:, :lc] * beta_col * gam, 0.0)
    a_qk = kq[:, lc:] * gam
    n = -m
    x = eye + n
    p = _bdot(n, n, 2, 1, hi=True)
    yield
    for _ in range(int(math.log2(lc)) - 2):
        px_pp = _bdot(p, jnp.concatenate([x, p], axis=2), 2, 1, hi=True)
        yield
        x, p = x + px_pp[:, :, :lc], px_pp[:, :, lc:]
    x = x + _bdot(p, x, 2, 1, hi=True)
    yield
    uw = _bdot(x, jnp.concatenate([v32 * beta_col, k32 * (beta_col * eg)], axis=2), 2, 1, hi=True)
    yield
    u, w = uw[:, :, :v.shape[2]], uw[:, :, v.shape[2]:]
    ws_qs = _bdot(jnp.concatenate([w, q32 * eg], axis=1), s, 2, 1)
    yield
    v_new = u - ws_qs[:, :lc]
    o = ws_qs[:, lc:] + _bdot(a_qk, v_new, 2, 1)
    kv = _bdot(kt.astype(F32) * jnp.exp(g_last - g_row), v_new, 2, 1)
    out.append((o, s * jnp.exp(g_last) + kv))


def _delta_scan_kernel(xf_ref, xb_ref, ktf_ref, ktb_ref, gf_ref, gb_ref, gtf_ref, gtb_ref, of_ref, ob_ref, s_ref, *, nh):
    c = pl.program_id(1)

    @pl.when(c == 0)
    def _():
        s_ref[...] = jnp.zeros_like(s_ref)

    results, stages = [], []
    dirs = ((xf_ref, ktf_ref, gf_ref, gtf_ref), (xb_ref, ktb_ref, gb_ref, gtb_ref))
    for d_i, (x_ref, kt_ref, g_ref, gt_ref) in enumerate(dirs):
        gn = g_ref[0]
        gt = gt_ref[0, 0]
        lanes = [2 * d_i * nh + hh for hh in range(nh)]
        g_col = jnp.stack([gn[:, l:l + 1] for l in lanes])
        b_col = jnp.stack([gn[:, l + nh:l + nh + 1] for l in lanes])
        g_row = jnp.stack([gt[l:l + 1, :] for l in lanes])
        out = []
        results.append(out)
        stages.append(_delta_heads(x_ref[0, 0], x_ref[0, 1], kt_ref[0, :, 0], x_ref[0, 2], g_col, b_col, g_row,
                                   s_ref[d_i * nh:(d_i + 1) * nh], d_i == 1, out))
    fwd, bwd = stages
    next(fwd)
    for _ in fwd:
        next(bwd, None)
    for _ in bwd:
        pass
    for d_i, o_ref in enumerate((of_ref, ob_ref)):
        o, s_new = results[d_i][0]
        s_ref[d_i * nh:(d_i + 1) * nh] = s_new
        for hh in range(nh):
            o_ref[0, :, hh * HEAD_DIM:(hh + 1) * HEAD_DIM] = o[hh]


def _delta_scan(xh, kt, gn, gnt, *, n_ctx):
    bsz, _, nh, t, _ = xh.shape
    lc = DELTA_CHUNK
    nc, ncc = t // lc, n_ctx // lc
    gnt4 = gnt.reshape(bsz, LANES, nc, lc).transpose(0, 2, 1, 3)
    rev = lambda c: jnp.where(c < ncc, ncc - 1 - c, nc - 1 - (c - ncc))
    return pl.pallas_call(
        functools.partial(_delta_scan_kernel, nh=nh),
        grid=(bsz, nc),
        in_specs=[pl.BlockSpec((1, 3, nh, lc, HEAD_DIM), lambda b, c: (b, 0, 0, c, 0)),
                  pl.BlockSpec((1, 3, nh, lc, HEAD_DIM), lambda b, c: (b, 0, 0, rev(c), 0)),
                  pl.BlockSpec((1, nh, 1, HEAD_DIM, lc), lambda b, c: (b, 0, c, 0, 0)),
                  pl.BlockSpec((1, nh, 1, HEAD_DIM, lc), lambda b, c: (b, 0, rev(c), 0, 0)),
                  pl.BlockSpec((1, lc, LANES), lambda b, c: (b, c, 0)),
                  pl.BlockSpec((1, lc, LANES), lambda b, c: (b, rev(c), 0)),
                  pl.BlockSpec((1, 1, LANES, lc), lambda b, c: (b, c, 0, 0)),
                  pl.BlockSpec((1, 1, LANES, lc), lambda b, c: (b, rev(c), 0, 0))],
        out_specs=[pl.BlockSpec((1, lc, nh * HEAD_DIM), lambda b, c: (b, c, 0)),
                   pl.BlockSpec((1, lc, nh * HEAD_DIM), lambda b, c: (b, rev(c), 0))],
        out_shape=[jax.ShapeDtypeStruct((bsz, t, nh * HEAD_DIM), F32),
                   jax.ShapeDtypeStruct((bsz, t, nh * HEAD_DIM), F32)],
        scratch_shapes=[pltpu.VMEM((2 * nh, HEAD_DIM, HEAD_DIM), F32)],
        compiler_params=_cparams("arbitrary", "arbitrary"),
        name="delta_scan",
    )(xh, xh, kt, kt, gn, gn, gnt4, gnt4)


def kernel(x, c, ctx, c_ctx, ada_w, ada_b, ln1_g, ln2_g, even_w_in, even_conv_w, even_a_log, even_dt_bias, even_out_norm_g, even_q_norm_g, even_k_norm_g, even_w_out, odd_w_in, odd_rpb, odd_w_out, router_w, expert_w_gate, expert_w_up, expert_w_down, final_norm_g):
    bsz, n_lat, d = x.shape
    n_ctx = ctx.shape[1]
    nct = n_ctx // TILE
    h = (ctx, x)
    rows = jnp.zeros((8, d), F32).at[0].set(c_ctx).at[1:1 + bsz].set(c)

    def mod_of(layer):
        m = _adaln(rows, ada_w, ada_b, layer).reshape(8, 6, d)
        return jnp.stack([jnp.broadcast_to(m[0], (bsz, 6, d)), m[1:1 + bsz]], axis=1)

    mod = mod_of(0)
    a_dim = d // 2
    a_heads = a_dim // HEAD_DIM
    w_in = even_w_in[0]
    o_g = 4 * a_dim
    o_bq = o_g + 4 * a_heads
    kv_dim = (w_in.shape[1] - o_bq - a_dim) // 2
    gates_w = jnp.zeros((d, LANES), F32).at[:, :4 * a_heads].set(w_in[:, o_g:o_bq])
    w_perm = jnp.concatenate([w_in[:, :o_g], w_in[:, o_bq:], gates_w], axis=1).astype(BF16)
    p = _modproj(h, mod, ln1_g[0], w_perm, nct, F32)
    xh, kt, gn, gnt = _delta_prep(p, even_conv_w[0], even_a_log[0], even_dt_bias[0], n_ctx=n_ctx, a_dim=a_dim)
    o_delta = _delta_scan(xh, kt, gn, gnt, n_ctx=n_ctx)
    cos, sin = _rope_tables(n_ctx, n_lat)
    qb, kb, vb = _gqa_prep(p, cos, sin, even_q_norm_g[0], even_k_norm_g[0], a_dim=a_dim, kv_dim=kv_dim)
    o_attn = _gqa_attention(qb, kb, vb, n_ctx=n_ctx)
    h = _outproj_even(h, mod, o_delta, p, o_attn, even_out_norm_g[0], even_w_out[0].astype(BF16), nct, a_dim=a_dim)
    h = _moe(h, mod, ln2_g[0], router_w[0], expert_w_gate, expert_w_up, expert_w_down, 0, final_norm_g,
             n_ctx=n_ctx, route_ctx=True, final_norm=False)

    mod = mod_of(1)
    p = _modproj(h, mod, ln1_g[1], odd_w_in[0].astype(BF16), nct, BF16)
    o_na = _natten(p, _natten_bias(odd_rpb[0]), n_ctx=n_ctx, d=d)
    h_lat = _outproj_odd(h, mod, o_na, odd_w_out[0].astype(BF16), nct)
    return _moe(h_lat, mod, ln2_g[1], router_w[1], expert_w_gate, expert_w_up, expert_w_down, 1, final_norm_g,
                n_ctx=0, route_ctx=False, final_norm=True)
```

```python
import functools
import math

import numpy as np
import jax
import jax.numpy as jnp
from jax import lax
from jax.experimental import pallas as pl
from jax.experimental.pallas import tpu as pltpu

F32, BF16, I32 = jnp.float32, jnp.bfloat16, jnp.int32

HEAD_DIM = 64
GRID_W = 64
DELTA_CHUNK = 64
NA_ROWS = 8
NA_COLS = 16
ROPE_THETA = 10000.0
EC_CAPACITY = 2
EPS = 1e-6
LOG2_E = math.log2(math.e)

LANES = 128
TILE = 256
SUBT = 128
WIN_X = SUBT + 8
WIN_Y = SUBT + 16
VMEM_LIMIT = 56 * 1024 * 1024


def _cparams(*sem):
    return pltpu.CompilerParams(dimension_semantics=sem, vmem_limit_bytes=VMEM_LIMIT)


def _split3(a):
    a1 = a.astype(BF16)
    r1 = a - a1.astype(F32)
    a2 = r1.astype(BF16)
    a3 = (r1 - a2.astype(F32)).astype(BF16)
    return a1, a2, a3


def _dot(a, b):
    return jnp.dot(a, b, preferred_element_type=F32)


def _dot_t(a, b):
    return lax.dot_general(a, b, (((1,), (1,)), ((), ())), preferred_element_type=F32)


def _dot_exact_rhs(a, b_bf16):
    a1, a2, a3 = _split3(a)
    return _dot(a1, b_bf16) + _dot(a2, b_bf16) + _dot(a3, b_bf16)


def _dot_hi(a, b):
    a1 = a.astype(BF16)
    a2 = (a - a1.astype(F32)).astype(BF16)
    b1 = b.astype(BF16)
    b2 = (b - b1.astype(F32)).astype(BF16)
    return _dot(a1, b1) + _dot(a1, b2) + _dot(a2, b1)


def _sigmoid(x):
    return 1.0 / (1.0 + jnp.exp(-x))


def _silu(x):
    return x * _sigmoid(x)


def _modulate(x, g, shift, scale):
    ms = jnp.mean(x * x, axis=-1, keepdims=True)
    return x * lax.rsqrt(ms + EPS) * g * (1.0 + scale) + shift


def _adaln_kernel(s_ref, w_ref, b_ref, o_ref):
    s = _silu(s_ref[...])
    o_ref[...] = _dot_hi(s, w_ref[0]) + b_ref[0]


def _adaln(rows, w, b, layer):
    depth, d, n = w.shape
    tn = n // 4
    return pl.pallas_call(
        _adaln_kernel,
        grid=(n // tn,),
        in_specs=[pl.BlockSpec((8, d), lambda j: (0, 0)),
                  pl.BlockSpec((1, d, tn), lambda j: (layer, 0, j)),
                  pl.BlockSpec((1, 1, tn), lambda j: (layer, 0, j))],
        out_specs=pl.BlockSpec((8, tn), lambda j: (0, j)),
        out_shape=jax.ShapeDtypeStruct((8, n), F32),
        compiler_params=_cparams("arbitrary"),
        name="adaln",
    )(rows, w, b.reshape(depth, 1, n))


def _hidden_operands(h, nct):
    if isinstance(h, tuple):
        hc, hl = h
        d = hc.shape[-1]
        specs = [pl.BlockSpec((1, TILE, d), lambda b, i: (b, jnp.minimum(i, nct - 1), 0)),
                 pl.BlockSpec((1, TILE, d), lambda b, i: (b, jnp.maximum(i - nct, 0), 0))]
        return [hc, hl], specs, hc.shape[1] + hl.shape[1]
    return [h], [pl.BlockSpec((1, TILE, h.shape[-1]), lambda b, i: (b, i, 0))], h.shape[1]


def _hidden_tile(h_refs, nct):
    if len(h_refs) == 2:
        return jnp.where(pl.program_id(1) < nct, h_refs[0][0], h_refs[1][0])
    return h_refs[0][0]


def _modproj_kernel(*refs, n_h, nct):
    mod_ref, g_ref, w_ref, o_ref = refs[n_h:]
    mod = mod_ref[0, 0]
    a = _modulate(_hidden_tile(refs[:n_h], nct), g_ref[...], mod[0:1], mod[1:2])
    o_ref[0] = _dot(a.astype(BF16), w_ref[...]).astype(o_ref.dtype)


def _modproj(h, mod, g, w_bf16, nct, out_dtype):
    h_ops, h_specs, t = _hidden_operands(h, nct)
    bsz, _, d = h_ops[0].shape
    n = w_bf16.shape[1]
    return pl.pallas_call(
        functools.partial(_modproj_kernel, n_h=len(h_ops), nct=nct),
        grid=(bsz, t // TILE),
        in_specs=h_specs + [pl.BlockSpec((1, 1, 6, d), lambda b, i: (b, jnp.where(i < nct, 0, 1), 0, 0)),
                            pl.BlockSpec((1, d), lambda b, i: (0, 0)),
                            pl.BlockSpec((d, n), lambda b, i: (0, 0))],
        out_specs=pl.BlockSpec((1, TILE, n), lambda b, i: (b, i, 0)),
        out_shape=jax.ShapeDtypeStruct((bsz, t, n), out_dtype),
        compiler_params=_cparams("arbitrary", "arbitrary"),
        name="modproj",
    )(*h_ops, mod, g.reshape(1, d), w_bf16)


def _router_kernel(h_ref, mod_ref, g_ref, wr_ref, a_ref, aff_ref, afft_ref, *, n_exp):
    mod = mod_ref[0, 0]
    a = _modulate(h_ref[0], g_ref[...], mod[3:4], mod[4:5])
    a_ref[0] = a.astype(BF16)
    logits = _dot_hi(a, wr_ref[...])
    lane = lax.broadcasted_iota(I32, logits.shape, 1)
    logits = jnp.where(lane < n_exp, logits, -jnp.inf)
    p = jnp.exp(logits - jnp.max(logits, axis=-1, keepdims=True))
    aff = p / jnp.sum(p, axis=-1, keepdims=True)
    aff_ref[0] = aff
    afft_ref[0] = aff.T


def _router(h, mod, g, w_router, nct):
    bsz, t, d = h.shape
    n_exp = w_router.shape[1]
    wr = jnp.zeros((d, LANES), F32).at[:, :n_exp].set(w_router)
    return pl.pallas_call(
        functools.partial(_router_kernel, n_exp=n_exp),
        grid=(bsz, t // TILE),
        in_specs=[pl.BlockSpec((1, TILE, d), lambda b, i: (b, i, 0)),
                  pl.BlockSpec((1, 1, 6, d), lambda b, i: (b, jnp.where(i < nct, 0, 1), 0, 0)),
                  pl.BlockSpec((1, d), lambda b, i: (0, 0)),
                  pl.BlockSpec((d, LANES), lambda b, i: (0, 0))],
        out_specs=[pl.BlockSpec((1, TILE, d), lambda b, i: (b, i, 0)),
                   pl.BlockSpec((1, TILE, LANES), lambda b, i: (b, i, 0)),
                   pl.BlockSpec((1, LANES, TILE), lambda b, i: (b, 0, i))],
        out_shape=[jax.ShapeDtypeStruct((bsz, t, d), BF16),
                   jax.ShapeDtypeStruct((bsz, t, LANES), F32),
                   jax.ShapeDtypeStruct((bsz, LANES, t), F32)],
        compiler_params=_cparams("arbitrary", "arbitrary"),
        name="router",
    )(h, mod, g.reshape(1, d), wr)


def _select_kernel(afft_ref, pos_ref, post_ref, cb_ref, cnt_ref, sel_ref, bucket_ref, *, n_exp, n_ctx, k_ctx, k_lat, t):
    tok = lax.broadcasted_iota(I32, (n_exp, t), 1)
    is_ctx = tok < n_ctx
    min_normal = 0x00800000

    def choose(in_set, k):
        def vals():
            return jnp.where(in_set, afft_ref[0][:n_exp], -1.0)

        def bit_step(it, thr):
            cand = thr | jnp.left_shift(jnp.int32(1), 30 - it)
            cnt = jnp.sum(jnp.where(vals() >= pltpu.bitcast(cand, F32), 1, 0), axis=1, keepdims=True)
            return jnp.where(cnt >= k, cand, thr)
        thr = lax.fori_loop(0, 31, bit_step, jnp.zeros((n_exp, 1), I32))
        lo = pltpu.bitcast(thr, F32)
        hi = pltpu.bitcast(jnp.maximum(thr + 1, min_normal), F32)
        v = vals()
        above = v >= hi
        sel_ref[...] = jnp.where(above, 1.0, sel_ref[...])
        bucket_ref[...] = jnp.where((v >= lo) & jnp.logical_not(above), 1.0, 0.0)
        need0 = k - jnp.sum(jnp.where(above, 1, 0), axis=1, keepdims=True)

        def peel(need):
            in_bucket = bucket_ref[...] > 0.0
            v = vals()
            top = jnp.max(jnp.where(in_bucket, v, -1.0), axis=1, keepdims=True)
            first = jnp.min(jnp.where(in_bucket & (v == top), tok, t), axis=1, keepdims=True)
            pick = (tok == first) & (need > 0)
            sel_ref[...] = jnp.where(pick, 1.0, sel_ref[...])
            bucket_ref[...] = jnp.where(pick, 0.0, bucket_ref[...])
            return need - 1
        lax.while_loop(lambda need: jnp.max(need) > 0, peel, need0)

    sel_ref[...] = jnp.zeros_like(sel_ref)
    choose(jnp.logical_not(is_ctx), k_lat)
    if k_ctx > 0:
        choose(is_ctx, k_ctx)

    rr = lax.broadcasted_iota(I32, (LANES, LANES), 0)
    cc = lax.broadcasted_iota(I32, (LANES, LANES), 1)
    upper = jnp.where(rr <= cc, 1.0, 0.0).astype(BF16)
    carry = jnp.zeros((n_exp, 1), F32)
    for blk in range(t // LANES):
        c = _dot(sel_ref[:, blk * LANES:(blk + 1) * LANES].astype(BF16), upper) + carry
        cnt_ref[:, blk * LANES:(blk + 1) * LANES] = c
        carry = c[:, LANES - 1:LANES]
    posinc = cnt_ref[...].astype(I32)
    pos = jnp.where(sel_ref[...] > 0.0, posinc - 1, -1)
    pos_ref[0] = pos

    lane = lax.broadcasted_iota(I32, (n_exp, LANES), 1)
    cb = jnp.zeros((n_exp, LANES), I32)
    for i in range(1, t // SUBT + 1):
        cb = jnp.where(lane == i, posinc[:, i * SUBT - 1:i * SUBT], cb)
    cb_ref[0] = cb

    padded = jnp.concatenate([pos.astype(F32), jnp.full((LANES - n_exp, t), -1.0, F32)], axis=0)
    for i in range(t // TILE):
        post_ref[0, i * TILE:(i + 1) * TILE, :] = padded[:, i * TILE:(i + 1) * TILE].T.astype(I32)


def _select(afft, n_exp, n_ctx, k_ctx, k_lat):
    bsz, _, t = afft.shape
    assert n_ctx % LANES == 0 and t % TILE == 0 and t // SUBT < LANES
    return pl.pallas_call(
        functools.partial(_select_kernel, n_exp=n_exp, n_ctx=n_ctx, k_ctx=k_ctx, k_lat=k_lat, t=t),
        grid=(bsz,),
        in_specs=[pl.BlockSpec((1, LANES, t), lambda b: (b, 0, 0))],
        out_specs=[pl.BlockSpec((1, n_exp, t), lambda b: (b, 0, 0)),
                   pl.BlockSpec((1, t, LANES), lambda b: (b, 0, 0)),
                   pl.BlockSpec((1, n_exp, LANES), lambda b: (b, 0, 0))],
        out_shape=[jax.ShapeDtypeStruct((bsz, n_exp, t), I32),
                   jax.ShapeDtypeStruct((bsz, t, LANES), I32),
                   jax.ShapeDtypeStruct((bsz, n_exp, LANES), I32)],
        scratch_shapes=[pltpu.VMEM((n_exp, t), F32)] * 3,
        compiler_params=_cparams("arbitrary"),
        name="select",
    )(afft)


def _ffn_rows(cap_tot):
    for rc in range(min(cap_tot, 512), 7, -8):
        if cap_tot % rc == 0:
            return rc
    raise ValueError(cap_tot)


def _expert_kernel(cb_ref, pos_ref, a_ref, wg_ref, wu_ref, wd_ref, y_ref, x_ref, *, n_exp, nsub, sub_per, cap_tot):
    e, b, ch = pl.program_id(0), pl.program_id(1), pl.program_id(2)

    @pl.when(ch == 0)
    def _():
        x_ref[...] = jnp.zeros_like(x_ref)

    base = (b * n_exp + e) * (nsub + 1) + ch * sub_per
    row = lax.broadcasted_iota(I32, (WIN_X, SUBT), 0)

    def gather(j, carry):
        start = cb_ref[base + j]
        start_al = pl.multiple_of((start // 8) * 8, 8)
        rel = pos_ref[0, 0, j] - start_al
        onehot = jnp.where(row == rel, 1.0, 0.0).astype(BF16)
        tok0 = pl.multiple_of(j * SUBT, SUBT)
        x_ref[pl.ds(start_al, WIN_X), :] += _dot(onehot, a_ref[0, pl.ds(tok0, SUBT), :])
        return carry
    lax.fori_loop(0, sub_per, gather, 0, unroll=True)

    @pl.when(ch == pl.num_programs(2) - 1)
    def _():
        rc = _ffn_rows(cap_tot)
        wg = wg_ref[0, 0].astype(BF16)
        wu = wu_ref[0, 0].astype(BF16)
        wd = wd_ref[0, 0].astype(BF16)
        for r0 in range(0, cap_tot, rc):
            xc = x_ref[r0:r0 + rc, :].astype(BF16)
            hid = _silu(_dot(xc, wg)) * _dot(xc, wu)
            y_ref[0, 0, r0:r0 + rc, :] = _dot(hid.astype(BF16), wd).astype(y_ref.dtype)
        y_ref[0, 0, cap_tot:, :] = jnp.zeros((y_ref.shape[2] - cap_tot, y_ref.shape[-1]), y_ref.dtype)


def _experts(cb_flat, pos, a, wg, wu, wd, layer, cap_tot):
    bsz, n_exp, t = pos.shape
    d = a.shape[-1]
    ff = wg.shape[-1]
    nsub = t // SUBT
    ntiles = t // TILE
    chunk = TILE * max(m for m in range(1, 13) if ntiles % m == 0)
    sub_per = chunk // SUBT
    assert cap_tot % 16 == 0
    rows_x = cap_tot + WIN_X
    rows_y = cap_tot + WIN_Y
    pos5 = pos.reshape(bsz, n_exp, nsub, 1, SUBT)
    grid_spec = pltpu.PrefetchScalarGridSpec(
        num_scalar_prefetch=1,
        grid=(n_exp, bsz, t // chunk),
        in_specs=[pl.BlockSpec((1, 1, sub_per, 1, SUBT), lambda e, b, c, cb: (b, e, c, 0, 0)),
                  pl.BlockSpec((1, chunk, d), lambda e, b, c, cb: (b, c, 0)),
                  pl.BlockSpec((1, 1, d, ff), lambda e, b, c, cb: (layer, e, 0, 0)),
                  pl.BlockSpec((1, 1, d, ff), lambda e, b, c, cb: (layer, e, 0, 0)),
                  pl.BlockSpec((1, 1, ff, d), lambda e, b, c, cb: (layer, e, 0, 0))],
        out_specs=pl.BlockSpec((1, 1, rows_y, d), lambda e, b, c, cb: (b, e, 0, 0)),
        scratch_shapes=[pltpu.VMEM((rows_x, d), F32)],
    )
    return pl.pallas_call(
        functools.partial(_expert_kernel, n_exp=n_exp, nsub=nsub, sub_per=sub_per, cap_tot=cap_tot),
        grid_spec=grid_spec,
        out_shape=jax.ShapeDtypeStruct((bsz, n_exp, rows_y, d), BF16),
        compiler_params=_cparams("arbitrary", "arbitrary", "arbitrary"),
        name="experts",
    )(cb_flat, pos5, a, wg, wu, wd)


def _combine(cb_flat, h, mod, aff, post, y, g_final, nct, final_norm):
    bsz, t, d = h.shape
    n_exp = y.shape[1]
    ntiles = t // TILE
    first = nct if final_norm else 0
    n_win = n_exp * (TILE // SUBT)
    grid_spec = pltpu.PrefetchScalarGridSpec(
        num_scalar_prefetch=1,
        grid=(bsz, ntiles - first),
        in_specs=[pl.BlockSpec((1, TILE, d), lambda b, i, cb: (b, i + first, 0)),
                  pl.BlockSpec((1, 1, 6, d), lambda b, i, cb: (b, jnp.where(i + first < nct, 0, 1), 0, 0)),
                  pl.BlockSpec((1, TILE, LANES), lambda b, i, cb: (b, i + first, 0)),
                  pl.BlockSpec((1, TILE, LANES), lambda b, i, cb: (b, i + first, 0)),
                  pl.BlockSpec(memory_space=pl.ANY),
                  pl.BlockSpec((1, d), lambda b, i, cb: (0, 0))],
        out_specs=pl.BlockSpec((1, TILE, d), lambda b, i, cb: (b, i, 0)),
        scratch_shapes=[pltpu.VMEM((2, n_win, WIN_Y, d), y.dtype), pltpu.SemaphoreType.DMA((2, n_win))],
    )
    return pl.pallas_call(
        functools.partial(_combine_kernel, n_exp=n_exp, nsub=t // SUBT, first=first, final_norm=final_norm),
        grid_spec=grid_spec,
        out_shape=jax.ShapeDtypeStruct((bsz, t - first * TILE, d), F32),
        compiler_params=_cparams("arbitrary", "arbitrary"),
        name="combine",
    )(cb_flat, h, mod, aff, post, y, g_final.reshape(1, d))


def _combine_kernel(cb_ref, h_ref, mod_ref, aff_ref, post_ref, y_hbm, g_ref, o_ref, ybuf, sem,
                    *, n_exp, nsub, first, final_norm):
    n_i = pl.num_programs(1)
    step = pl.program_id(0) * n_i + pl.program_id(1)
    n_steps = pl.num_programs(0) * n_i
    per_tile = TILE // SUBT

    def window(at_step, sub, e):
        b = at_step // n_i
        i = at_step - b * n_i + first
        start = cb_ref[(b * n_exp + e) * (nsub + 1) + i * per_tile + sub]
        start_al = pl.multiple_of((start // 16) * 16, 16)
        buf, slot = at_step % 2, sub * n_exp + e
        copy = pltpu.make_async_copy(y_hbm.at[b, e, pl.ds(start_al, WIN_Y), :], ybuf.at[buf, slot],
                                     sem.at[buf, slot])
        return copy, start_al

    def start_windows(at_step):
        for sub in range(per_tile):
            for e in range(n_exp):
                window(at_step, sub, e)[0].start()

    @pl.when(step == 0)
    def _():
        start_windows(step)

    @pl.when(step + 1 < n_steps)
    def _():
        start_windows(step + 1)

    lane = lax.broadcasted_iota(I32, (SUBT, WIN_Y), 1)
    mod = mod_ref[0, 0]
    for sub in range(per_tile):
        rows = slice(sub * SUBT, (sub + 1) * SUBT)
        post = post_ref[0, rows, :]
        aff = aff_ref[0, rows, :]
        acc = jnp.zeros((SUBT, o_ref.shape[-1]), F32)
        starts = []
        for e in range(n_exp):
            copy, start_al = window(step, sub, e)
            copy.wait()
            starts.append(start_al)
        for e, start_al in enumerate(starts):
            rel = post[:, e:e + 1] - start_al
            onehot = jnp.where(lane == rel, 1.0, 0.0).astype(BF16)
            acc = acc + aff[:, e:e + 1] * _dot(onehot, ybuf[step % 2, sub * n_exp + e])
        out = h_ref[0, rows, :] + mod[5:6] * acc
        if final_norm:
            ms = jnp.mean(out * out, axis=-1, keepdims=True)
            out = out * lax.rsqrt(ms + EPS) * g_ref[...]
        o_ref[0, rows, :] = out


def _moe(h, mod, ln2_g, w_router, wg, wu, wd, layer, g_final, *, n_ctx, route_ctx, final_norm):
    bsz, t, d = h.shape
    n_exp = w_router.shape[1]
    nct = n_ctx // TILE
    n_lat = t - n_ctx
    k_lat = max(1, EC_CAPACITY * n_lat // n_exp)
    k_ctx = max(1, EC_CAPACITY * n_ctx // n_exp) if route_ctx else 0
    a, aff, afft = _router(h, mod, ln2_g, w_router, nct)
    pos, post, cb = _select(afft, n_exp, n_ctx, k_ctx, k_lat)
    cb_flat = cb[:, :, :t // SUBT + 1].reshape(-1)
    y = _experts(cb_flat, pos, a, wg, wu, wd, layer, k_ctx + k_lat)
    return _combine(cb_flat, h, mod, aff, post, y, g_final, nct, final_norm)


HEAD_SHIFT = int(math.log2(HEAD_DIM))


def _same_head(rows, cols):
    r = lax.shift_right_logical(lax.broadcasted_iota(I32, (rows, cols), 0), HEAD_SHIFT)
    c = lax.shift_right_logical(lax.broadcasted_iota(I32, (rows, cols), 1), HEAD_SHIFT)
    return jnp.where(r == c, 1.0, 0.0)


def _head_sum(x2):
    n = x2.shape[-1]
    return _dot_exact_rhs(x2, _same_head(n, n).astype(BF16))


def _outproj_even_kernel(*refs, n_h, nct, a_dim):
    mod_ref, of_ref, ob_ref, z_ref, at_ref, g_ref, w_ref, o_ref = refs[n_h:]
    mod = mod_ref[0, 0]
    o = of_ref[0] + ob_ref[0]
    ms = _head_sum(o * o) * (1.0 / HEAD_DIM)
    ya = o * lax.rsqrt(ms + EPS) * g_ref[...] * _silu(z_ref[0])
    y = _dot(ya.astype(BF16), w_ref[:a_dim, :]) + _dot(at_ref[0], w_ref[a_dim:, :])
    o_ref[0] = _hidden_tile(refs[:n_h], nct) + mod[2:3] * y


def _outproj_even(h, mod, o_delta, p, o_attn, out_norm_g, w_bf16, nct, *, a_dim):
    h_ops, h_specs, t = _hidden_operands(h, nct)
    bsz, _, d = h_ops[0].shape
    o_f, o_b = o_delta
    g = jnp.tile(out_norm_g, a_dim // HEAD_DIM).reshape(1, a_dim)
    tok = lambda b, i: (b, i, 0)
    return pl.pallas_call(
        functools.partial(_outproj_even_kernel, n_h=len(h_ops), nct=nct, a_dim=a_dim),
        grid=(bsz, t // TILE),
        in_specs=h_specs + [pl.BlockSpec((1, 1, 6, d), lambda b, i: (b, jnp.where(i < nct, 0, 1), 0, 0)),
                            pl.BlockSpec((1, TILE, a_dim), tok),
                            pl.BlockSpec((1, TILE, a_dim), tok),
                            pl.BlockSpec((1, TILE, a_dim), lambda b, i: (b, i, 3)),
                            pl.BlockSpec((1, TILE, a_dim), tok),
                            pl.BlockSpec((1, a_dim), lambda b, i: (0, 0)),
                            pl.BlockSpec((d, d), lambda b, i: (0, 0))],
        out_specs=pl.BlockSpec((1, TILE, d), tok),
        out_shape=jax.ShapeDtypeStruct((bsz, t, d), F32),
        compiler_params=_cparams("arbitrary", "arbitrary"),
        name="outproj_even",
    )(*h_ops, mod, o_f, o_b, p, o_attn, g, w_bf16)


def _outproj_odd_kernel(h_ref, mod_ref, a_ref, w_ref, o_ref):
    mod = mod_ref[0, 0]
    o_ref[0] = h_ref[0] + mod[2:3] * _dot(a_ref[0], w_ref[...])


def _outproj_odd(h, mod, o_na, w_bf16, nct):
    bsz, t, d = h.shape
    n_lat = o_na.shape[1]
    return pl.pallas_call(
        _outproj_odd_kernel,
        grid=(bsz, n_lat // TILE),
        in_specs=[pl.BlockSpec((1, TILE, d), lambda b, i: (b, i + nct, 0)),
                  pl.BlockSpec((1, 1, 6, d), lambda b, i: (b, 1, 0, 0)),
                  pl.BlockSpec((1, TILE, d), lambda b, i: (b, i, 0)),
                  pl.BlockSpec((d, d), lambda b, i: (0, 0))],
        out_specs=pl.BlockSpec((1, TILE, d), lambda b, i: (b, i, 0)),
        out_shape=jax.ShapeDtypeStruct((bsz, n_lat, d), F32),
        compiler_params=_cparams("arbitrary", "arbitrary"),
        name="outproj_odd",
    )(h, mod, o_na, w_bf16)


def _rope_tables(n_ctx, n_lat):
    tt = jnp.arange(n_lat)
    n_freq = HEAD_DIM // 4
    inv = ROPE_THETA ** (-jnp.arange(n_freq, dtype=F32) / n_freq)
    ang = jnp.concatenate([(tt // GRID_W).astype(F32)[:, None] * inv, (tt % GRID_W).astype(F32)[:, None] * inv], -1)
    cos = jnp.concatenate([jnp.ones((n_ctx, 2 * n_freq), F32), jnp.cos(ang)], axis=0)
    sin = jnp.concatenate([jnp.zeros((n_ctx, 2 * n_freq), F32), jnp.sin(ang)], axis=0)
    reps = LANES // HEAD_DIM
    return jnp.tile(cos, (1, 2 * reps)), jnp.tile(jnp.concatenate([-sin, sin], axis=1), (1, reps))


def _rope(x, cos, sin):
    n = x.shape[-1]
    half = HEAD_DIM // 2
    lane = lax.broadcasted_iota(I32, x.shape, 1)
    first = (lane & (HEAD_DIM - 1)) < half
    partner = jnp.where(first, pltpu.roll(x, n - half, 1), pltpu.roll(x, half, 1))
    reps = n // LANES
    cos_n = jnp.concatenate([cos] * reps, axis=1) if reps > 1 else cos
    sin_n = jnp.concatenate([sin] * reps, axis=1) if reps > 1 else sin
    return x * cos_n + partner * sin_n


def _gqa_prep_kernel(q_ref, k_ref, v_ref, cos_ref, sin_ref, gq_ref, gk_ref, qo_ref, ko_ref, vo_ref):
    cos, sin = cos_ref[...], sin_ref[...]
    q = q_ref[0]
    q = q * lax.rsqrt(_head_sum(q * q) * (1.0 / HEAD_DIM) + EPS) * gq_ref[...] * (HEAD_DIM ** -0.5 * LOG2_E)
    qo_ref[0] = _rope(q, cos, sin).astype(BF16)
    k = k_ref[0]
    k = k * lax.rsqrt(_head_sum(k * k) * (1.0 / HEAD_DIM) + EPS) * gk_ref[...]
    k = _rope(k, cos, sin).astype(BF16)
    v = v_ref[0].astype(BF16)
    ones_col = jnp.where(lax.broadcasted_iota(I32, (v.shape[0], LANES - HEAD_DIM), 1) == 0, 1.0, 0.0).astype(BF16)
    for hh in range(k.shape[-1] // HEAD_DIM):
        ko_ref[0, hh] = k[:, hh * HEAD_DIM:(hh + 1) * HEAD_DIM]
        vo_ref[0, hh] = jnp.concatenate([v[:, hh * HEAD_DIM:(hh + 1) * HEAD_DIM], ones_col], axis=1)


def _gqa_prep(p, cos, sin, q_norm_g, k_norm_g, *, a_dim, kv_dim):
    bsz, t, _ = p.shape
    q_dim = a_dim
    kvh = kv_dim // HEAD_DIM
    assert kv_dim == LANES
    c_q = (4 * a_dim) // q_dim
    c_k = (4 * a_dim + q_dim) // kv_dim
    gq = jnp.tile(q_norm_g, q_dim // HEAD_DIM).reshape(1, q_dim)
    gk = jnp.tile(k_norm_g, kvh).reshape(1, kv_dim)
    return pl.pallas_call(
        _gqa_prep_kernel,
        grid=(bsz, t // TILE),
        in_specs=[pl.BlockSpec((1, TILE, q_dim), lambda b, i: (b, i, c_q)),
                  pl.BlockSpec((1, TILE, kv_dim), lambda b, i: (b, i, c_k)),
                  pl.BlockSpec((1, TILE, kv_dim), lambda b, i: (b, i, c_k + 1)),
                  pl.BlockSpec((TILE, LANES), lambda b, i: (i, 0)),
                  pl.BlockSpec((TILE, LANES), lambda b, i: (i, 0)),
                  pl.BlockSpec((1, q_dim), lambda b, i: (0, 0)),
                  pl.BlockSpec((1, kv_dim), lambda b, i: (0, 0))],
        out_specs=[pl.BlockSpec((1, TILE, q_dim), lambda b, i: (b, i, 0)),
                   pl.BlockSpec((1, kvh, TILE, HEAD_DIM), lambda b, i: (b, 0, i, 0)),
                   pl.BlockSpec((1, kvh, TILE, LANES), lambda b, i: (b, 0, i, 0))],
        out_shape=[jax.ShapeDtypeStruct((bsz, t, q_dim), BF16),
                   jax.ShapeDtypeStruct((bsz, kvh, t, HEAD_DIM), BF16),
                   jax.ShapeDtypeStruct((bsz, kvh, t, LANES), BF16)],
        compiler_params=_cparams("arbitrary", "arbitrary"),
        name="gqa_prep",
    )(p, p, p, cos, sin, gq, gk)


SUB_K = 256
Q_SLAB = 256


def _gqa_kernel(q_ref, k_ref, v_ref, o_ref, qs_ref, m_ref, acc_ref, *, n_ctx, nct, tq, tk, n_heads, kvh):
    i, kv = pl.program_id(1), pl.program_id(2)
    total = n_heads * tq
    per_kv = total // kvh

    @pl.when(kv == 0)
    def _():
        m_ref[...] = jnp.full_like(m_ref, -jnp.inf)
        acc_ref[...] = jnp.zeros_like(acc_ref)
        q = q_ref[0]
        for hh in range(n_heads):
            qs_ref[hh * tq:(hh + 1) * tq, :] = q[:, hh * HEAD_DIM:(hh + 1) * HEAD_DIM]

    def attend(n_keys, n_valid):
        edges = ([0, Q_SLAB // 2] + list(range(Q_SLAB + Q_SLAB // 2, total - Q_SLAB // 2, Q_SLAB))
                 + [total - Q_SLAB // 2, total])
        edges = sorted(set(edges) | set(range(0, total + 1, per_kv)))
        slabs = [slice(a, b) for a, b in zip(edges[:-1], edges[1:])]
        keys = lambda rows: k_ref[0, rows.start // per_kv, :n_keys, :]
        vals = lambda rows: v_ref[0, rows.start // per_kv, :n_keys, :]
        scores = {0: _dot_t(qs_ref[slabs[0], :], keys(slabs[0]))}
        for idx, rows in enumerate(slabs):
            if idx + 1 < len(slabs):
                scores[idx + 1] = _dot_t(qs_ref[slabs[idx + 1], :], keys(slabs[idx + 1]))
            s = scores.pop(idx)
            if n_valid is not None:
                s = jnp.where(lax.broadcasted_iota(I32, s.shape, 1) < n_valid, s, -jnp.inf)
            m_old = m_ref[rows, :]
            m_new = jnp.maximum(m_old, jnp.max(s, axis=-1, keepdims=True))
            alpha = jnp.exp2(m_old - m_new)
            p = jnp.exp2(s - m_new)
            acc_ref[rows, :] = alpha * acc_ref[rows, :] + _dot(p.astype(BF16), vals(rows))
            m_ref[rows, :] = m_new

    @pl.when(i >= nct)
    def _():
        attend(tk, None)

    @pl.when((i < nct) & (kv == 0))
    def _():
        n_keys = -(-n_ctx // SUB_K) * SUB_K
        attend(n_keys, n_ctx if n_keys != n_ctx else None)

    @pl.when(kv == pl.num_programs(2) - 1)
    def _():
        acc = acc_ref[...]
        out = acc[:, :HEAD_DIM] / acc[:, HEAD_DIM:HEAD_DIM + 1]
        for hh in range(n_heads):
            o_ref[0, :, hh * HEAD_DIM:(hh + 1) * HEAD_DIM] = out[hh * tq:(hh + 1) * tq, :].astype(o_ref.dtype)


def _gqa_attention(q, k, v, *, n_ctx):
    bsz, t, q_dim = q.shape
    kvh = k.shape[1]
    n_heads = q_dim // HEAD_DIM
    tq = TILE
    nt = t // TILE
    tk = TILE * max(m for m in range(1, 13) if nt % m == 0)
    assert n_ctx <= tk and tk % SUB_K == 0 and (n_heads // kvh * tq) % Q_SLAB == 0
    return pl.pallas_call(
        functools.partial(_gqa_kernel, n_ctx=n_ctx, nct=n_ctx // tq, tq=tq, tk=tk, n_heads=n_heads, kvh=kvh),
        grid=(bsz, t // tq, t // tk),
        in_specs=[pl.BlockSpec((1, tq, q_dim), lambda b, i, j: (b, i, 0)),
                  pl.BlockSpec((1, kvh, tk, HEAD_DIM), lambda b, i, j: (b, 0, j, 0)),
                  pl.BlockSpec((1, kvh, tk, LANES), lambda b, i, j: (b, 0, j, 0))],
        out_specs=pl.BlockSpec((1, tq, q_dim), lambda b, i, j: (b, i, 0)),
        out_shape=jax.ShapeDtypeStruct((bsz, t, q_dim), BF16),
        scratch_shapes=[pltpu.VMEM((n_heads * tq, HEAD_DIM), BF16),
                        pltpu.VMEM((n_heads * tq, 1), F32),
                        pltpu.VMEM((n_heads * tq, LANES), F32)],
        compiler_params=_cparams("arbitrary", "arbitrary", "arbitrary"),
        name="gqa_attention",
    )(q, k, v)


def _natten_bias(rpb):
    n_heads, n_off, n_rel = rpb.shape
    col = np.arange(GRID_W)
    col_start = np.clip(col - NA_COLS // 2, 0, GRID_W - NA_COLS)
    valid = (col[None, :] >= col_start[:, None]) & (col[None, :] < col_start[:, None] + NA_COLS)
    rel = col[None, :] - col[:, None] + NA_COLS - 1
    onehot = ((rel[None] == np.arange(n_rel)[:, None, None]) & valid[None]).astype(np.float32)
    tab = jnp.einsum('hrk,kcd->hrcd', rpb, jnp.asarray(onehot), precision=lax.Precision.HIGHEST)
    tab = jnp.where(valid[None, None], tab, -jnp.inf)
    pairs = jnp.concatenate([tab[:, :-1], tab[:, 1:]], axis=-1)
    return pairs.reshape(n_heads // 2, 2, n_off - 1, GRID_W, 2 * GRID_W).astype(F32)


def _natten_kernel(*refs, n_ctx, n_rows, rblk, n_q):
    q_refs = refs[:n_q]
    k_ref, v_ref, bias_ref, o_ref = refs[n_q:]
    per_q = TILE // GRID_W
    rb = pl.program_id(2)
    first_head = lax.broadcasted_iota(I32, (1, LANES), 1) < HEAD_DIM
    kc = k_ref[0, 0:n_ctx, :]
    vc = v_ref[0, 0:n_ctx, :]
    band = NA_ROWS * GRID_W
    qs, ks, vs, biases = [], [], [], []
    for rr in range(rblk):
        rg = rb * rblk + rr
        rs = jnp.clip(rg - NA_ROWS // 2, 0, n_rows - NA_ROWS)
        variant = rs - rg + (NA_ROWS - 1)
        start = pl.multiple_of(n_ctx + rs * GRID_W, GRID_W)
        keys = jnp.concatenate([kc, k_ref[0, pl.ds(start, band), :]], axis=0)
        vals = jnp.concatenate([vc, v_ref[0, pl.ds(start, band), :]], axis=0)
        q_lo = (rr % per_q) * GRID_W
        q = q_refs[rr // per_q][0, q_lo:q_lo + GRID_W, :] * (HEAD_DIM ** -0.5)
        qs.append(jnp.concatenate([jnp.where(first_head, q, jnp.zeros_like(q)),
                                   jnp.where(first_head, jnp.zeros_like(q), q)], axis=0))
        ks.append(keys)
        vs.append(vals)
        biases.append(jnp.concatenate(
            [jnp.concatenate([bias_ref[0, hh, variant + j] for j in range(0, NA_ROWS, 2)], axis=1)
             for hh in range(2)], axis=0))
    batch = (((2,), (2,)), ((0,), (0,)))
    s = lax.dot_general(jnp.stack(qs), jnp.stack(ks), batch, preferred_element_type=F32)
    sc = s[:, :, :n_ctx]
    sb = s[:, :, n_ctx:] + jnp.stack(biases)
    m = jnp.maximum(jnp.max(sc, axis=-1, keepdims=True), jnp.max(sb, axis=-1, keepdims=True))
    pc = jnp.exp(sc - m)
    pb = jnp.exp(sb - m)
    l = jnp.sum(pc, axis=-1, keepdims=True) + jnp.sum(pb, axis=-1, keepdims=True)
    p = jnp.concatenate([pc, pb], axis=-1).astype(BF16)
    o = lax.dot_general(p, jnp.stack(vs), (((2,), (1,)), ((0,), (0,))), preferred_element_type=F32) / l
    for rr in range(rblk):
        o_ref[0, rr * GRID_W:(rr + 1) * GRID_W, :] = jnp.where(first_head, o[rr, :GRID_W], o[rr, GRID_W:]).astype(o_ref.dtype)


def _natten(p, bias, *, n_ctx, d):
    bsz, t, _ = p.shape
    n_lat = t - n_ctx
    n_rows = n_lat // GRID_W
    n_q = 2 if n_lat % (2 * TILE) == 0 else 1
    rblk = n_q * TILE // GRID_W
    npairs = d // LANES
    nct = n_ctx // TILE
    assert n_rows >= NA_ROWS and n_ctx % TILE == 0
    return pl.pallas_call(
        functools.partial(_natten_kernel, n_ctx=n_ctx, n_rows=n_rows, rblk=rblk, n_q=n_q),
        grid=(bsz, npairs, n_lat // (n_q * TILE)),
        in_specs=[pl.BlockSpec((1, TILE, LANES), lambda b, hp, r, j=j: (b, nct + n_q * r + j, hp)) for j in range(n_q)] + [
                  pl.BlockSpec((1, t, LANES), lambda b, hp, r: (b, 0, npairs + hp)),
                  pl.BlockSpec((1, t, LANES), lambda b, hp, r: (b, 0, 2 * npairs + hp)),
                  pl.BlockSpec((1, 2, 2 * NA_ROWS - 2, GRID_W, 2 * GRID_W), lambda b, hp, r: (hp, 0, 0, 0, 0))],
        out_specs=pl.BlockSpec((1, n_q * TILE, LANES), lambda b, hp, r: (b, r, hp)),
        out_shape=jax.ShapeDtypeStruct((bsz, n_lat, d), BF16),
        compiler_params=_cparams("arbitrary", "arbitrary", "arbitrary"),
        name="natten",
    )(*([p] * n_q), p, p, bias)


def _dot_exact_lhs(a_bf16, b):
    b1, b2, b3 = _split3(b)
    return _dot(a_bf16, b1) + _dot(a_bf16, b2) + _dot(a_bf16, b3)


def _delta_prep_kernel(x_ref, prev_ref, next_ref, gate_ref, w_ref, alog_ref, dt_ref, xh_ref, kt_ref, gn_ref, gnt_ref,
                       xs_ref,
                       *, nct, a_dim):
    i = pl.program_id(1)
    last = pl.num_programs(1) - 1
    x = x_ref[0]
    rows = x.shape[0]
    xs_ref[0:8, :] = prev_ref[0]
    xs_ref[8:8 + rows, :] = x
    xs_ref[8 + rows:16 + rows, :] = next_ref[0]
    r = lax.broadcasted_iota(I32, (rows, 1), 0)
    at_start = (i == 0) | (i == nct)
    at_end = (i == nct - 1) | (i == last)
    xm = jnp.where((r == 0) & at_start, 0.0, xs_ref[7:7 + rows, :])
    xp = jnp.where((r == rows - 1) & at_end, 0.0, xs_ref[9:9 + rows, :])
    w = w_ref[...]
    y = _silu(w[0:1] * xm + w[1:2] * x + w[2:3] * xp)
    q, k, v = y[:, :a_dim], y[:, a_dim:2 * a_dim], y[:, 2 * a_dim:]
    nh = a_dim // HEAD_DIM
    q = (q * lax.rsqrt(_head_sum(q * q) + EPS) * (HEAD_DIM ** -0.5)).astype(BF16)
    k = k * lax.rsqrt(_head_sum(k * k) + EPS)
    k_t = k.T.astype(BF16)
    for hh in range(nh):
        for cc in range(rows // DELTA_CHUNK):
            kt_ref[0, hh, cc] = k_t[hh * HEAD_DIM:(hh + 1) * HEAD_DIM, cc * DELTA_CHUNK:(cc + 1) * DELTA_CHUNK]
    k = k.astype(BF16)
    v = v.astype(BF16)
    for kind, arr in enumerate((q, k, v)):
        for hh in range(nh):
            xh_ref[0, kind, hh] = arr[:, hh * HEAD_DIM:(hh + 1) * HEAD_DIM]

    g = gate_ref[0]
    z = g + dt_ref[...]
    softplus = jnp.maximum(z, 0.0) + jnp.log1p(jnp.exp(-jnp.abs(z)))
    log_a = -jnp.exp(alog_ref[...]) * softplus
    beta = _sigmoid(g)
    ri = lax.broadcasted_iota(I32, (rows, rows), 0)
    ci = lax.broadcasted_iota(I32, (rows, rows), 1)
    chunk_shift = int(math.log2(DELTA_CHUNK))
    same_chunk = lax.shift_right_logical(ri, chunk_shift) == lax.shift_right_logical(ci, chunk_shift)
    prefix = jnp.where(same_chunk & (ci <= ri), 1.0, 0.0).astype(BF16)
    suffix = jnp.where(same_chunk & (ci >= ri), 1.0, 0.0).astype(BF16)
    g_f = _dot_exact_lhs(prefix, log_a)
    g_b = _dot_exact_lhs(suffix, log_a)
    lane = lax.broadcasted_iota(I32, g.shape, 1)
    narrow = jnp.where(lane < nh, g_f, jnp.where(lane < 2 * nh, beta, jnp.where(lane < 3 * nh, g_b, beta)))
    gn_ref[0] = narrow
    gnt_ref[0] = narrow.T


def _delta_prep(p, conv_w, a_log, dt_bias, *, n_ctx, a_dim):
    bsz, t, _ = p.shape
    nh = a_dim // HEAD_DIM
    assert DELTA_CHUNK == HEAD_DIM and TILE % DELTA_CHUNK == 0 and 4 * nh <= LANES
    nct = n_ctx // TILE
    c_gate = (4 * a_dim + a_dim + 2 * LANES) // LANES
    alog = jnp.zeros((1, LANES), F32).at[0, :nh].set(a_log[0]).at[0, 2 * nh:3 * nh].set(a_log[1])
    dtb = jnp.zeros((1, LANES), F32).at[0, :nh].set(dt_bias[0]).at[0, 2 * nh:3 * nh].set(dt_bias[1])
    nblk8 = t // 8
    per = TILE // 8
    return pl.pallas_call(
        functools.partial(_delta_prep_kernel, nct=nct, a_dim=a_dim),
        grid=(bsz, t // TILE),
        in_specs=[pl.BlockSpec((1, TILE, 3 * a_dim), lambda b, i: (b, i, 0)),
                  pl.BlockSpec((1, 8, 3 * a_dim), lambda b, i: (b, jnp.maximum(i * per - 1, 0), 0)),
                  pl.BlockSpec((1, 8, 3 * a_dim), lambda b, i: (b, jnp.minimum((i + 1) * per, nblk8 - 1), 0)),
                  pl.BlockSpec((1, TILE, LANES), lambda b, i: (b, i, c_gate)),
                  pl.BlockSpec((3, 3 * a_dim), lambda b, i: (0, 0)),
                  pl.BlockSpec((1, LANES), lambda b, i: (0, 0)),
                  pl.BlockSpec((1, LANES), lambda b, i: (0, 0))],
        out_specs=[pl.BlockSpec((1, 3, nh, TILE, HEAD_DIM), lambda b, i: (b, 0, 0, i, 0)),
                   pl.BlockSpec((1, nh, TILE // DELTA_CHUNK, HEAD_DIM, DELTA_CHUNK), lambda b, i: (b, 0, i, 0, 0)),
                   pl.BlockSpec((1, TILE, LANES), lambda b, i: (b, i, 0)),
                   pl.BlockSpec((1, LANES, TILE), lambda b, i: (b, 0, i))],
        out_shape=[jax.ShapeDtypeStruct((bsz, 3, nh, t, HEAD_DIM), BF16),
                   jax.ShapeDtypeStruct((bsz, nh, t // DELTA_CHUNK, HEAD_DIM, DELTA_CHUNK), BF16),
                   jax.ShapeDtypeStruct((bsz, t, LANES), F32),
                   jax.ShapeDtypeStruct((bsz, LANES, t), F32)],
        scratch_shapes=[pltpu.VMEM((TILE + 16, 3 * a_dim), F32)],
        compiler_params=_cparams("arbitrary", "arbitrary"),
        name="delta_prep",
    )(p, p, p, p, conv_w, alog, dtb)


def _bdot(a, b, ca, cb, hi=False):
    dims = (((ca,), (cb,)), ((0,), (0,)))
    dot = lambda x, y: lax.dot_general(x, y, dims, preferred_element_type=F32)
    if not hi:
        return dot(a.astype(BF16), b.astype(BF16))
    a1 = a.astype(BF16)
    a2 = (a - a1.astype(F32)).astype(BF16)
    b1 = b.astype(BF16)
    b2 = (b - b1.astype(F32)).astype(BF16)
    return dot(jnp.concatenate([a1, a1, a2], axis=ca), jnp.concatenate([b1, b2, b1], axis=cb))


def _delta_heads(q, k, kt, v, g_col, beta_col, g_row, s, upper, out):
    nb, lc, _ = q.shape
    ri = lax.broadcasted_iota(I32, (nb, lc, lc), 1)
    ci = lax.broadcasted_iota(I32, (nb, lc, lc), 2)
    tri_incl = (ci >= ri) if upper else (ci <= ri)
    tri_strict = (ci > ri) if upper else (ci < ri)
    eye = jnp.where(ri == ci, 1.0, 0.0)
    k32, q32, v32 = k.astype(F32), q.astype(F32), v.astype(F32)
    eg = jnp.exp(g_col)
    g_last = g_col[:, 0:1] if upper else g_col[:, lc - 1:lc]
    gam = jnp.exp(jnp.where(tri_incl, g_col - g_row, -jnp.inf))
    kq = _bdot(jnp.concatenate([k, q], axis=1), kt, 2, 1)
    yield
    m = jnp.where(tri_strict, kq[:, :lc] * beta_col * gam, 0.0)
    a_qk = kq[:, lc:] * gam
    n = -m
    x = eye + n
    p = _bdot(n, n, 2, 1, hi=True)
    yield
    for _ in range(int(math.log2(lc)) - 2):
        px_pp = _bdot(p, jnp.concatenate([x, p], axis=2), 2, 1, hi=True)
        yield
        x, p = x + px_pp[:, :, :lc], px_pp[:, :, lc:]
    x = x + _bdot(p, x, 2, 1, hi=True)
    yield
    uw = _bdot(x, jnp.concatenate([v32 * beta_col, k32 * (beta_col * eg)], axis=2), 2, 1, hi=True)
    yield
    u, w = uw[:, :, :v.shape[2]], uw[:, :, v.shape[2]:]
    ws_qs = _bdot(jnp.concatenate([w, q32 * eg], axis=1), s, 2, 1)
    yield
    v_new = u - ws_qs[:, :lc]
    o = ws_qs[:, lc:] + _bdot(a_qk, v_new, 2, 1)
    kv = _bdot(kt.astype(F32) * jnp.exp(g_last - g_row), v_new, 2, 1)
    out.append((o, s * jnp.exp(g_last) + kv))


def _delta_scan_kernel(xf_ref, xb_ref, ktf_ref, ktb_ref, gf_ref, gb_ref, gtf_ref, gtb_ref, of_ref, ob_ref, s_ref, *, nh):
    c = pl.program_id(1)

    @pl.when(c == 0)
    def _():
        s_ref[...] = jnp.zeros_like(s_ref)

    results, stages = [], []
    dirs = ((xf_ref, ktf_ref, gf_ref, gtf_ref), (xb_ref, ktb_ref, gb_ref, gtb_ref))
    for d_i, (x_ref, kt_ref, g_ref, gt_ref) in enumerate(dirs):
        gn = g_ref[0]
        gt = gt_ref[0, 0]
        lanes = [2 * d_i * nh + hh for hh in range(nh)]
        g_col = jnp.stack([gn[:, l:l + 1] for l in lanes])
        b_col = jnp.stack([gn[:, l + nh:l + nh + 1] for l in lanes])
        g_row = jnp.stack([gt[l:l + 1, :] for l in lanes])
        out = []
        results.append(out)
        stages.append(_delta_heads(x_ref[0, 0], x_ref[0, 1], kt_ref[0, :, 0], x_ref[0, 2], g_col, b_col, g_row,
                                   s_ref[d_i * nh:(d_i + 1) * nh], d_i == 1, out))
    fwd, bwd = stages
    next(fwd)
    for _ in fwd:
        next(bwd, None)
    for _ in bwd:
        pass
    for d_i, o_ref in enumerate((of_ref, ob_ref)):
        o, s_new = results[d_i][0]
        s_ref[d_i * nh:(d_i + 1) * nh] = s_new
        for hh in range(nh):
            o_ref[0, :, hh * HEAD_DIM:(hh + 1) * HEAD_DIM] = o[hh]


def _delta_scan(xh, kt, gn, gnt, *, n_ctx):
    bsz, _, nh, t, _ = xh.shape
    lc = DELTA_CHUNK
    nc, ncc = t // lc, n_ctx // lc
    gnt4 = gnt.reshape(bsz, LANES, nc, lc).transpose(0, 2, 1, 3)
    rev = lambda c: jnp.where(c < ncc, ncc - 1 - c, nc - 1 - (c - ncc))
    return pl.pallas_call(
        functools.partial(_delta_scan_kernel, nh=nh),
        grid=(bsz, nc),
        in_specs=[pl.BlockSpec((1, 3, nh, lc, HEAD_DIM), lambda b, c: (b, 0, 0, c, 0)),
                  pl.BlockSpec((1, 3, nh, lc, HEAD_DIM), lambda b, c: (b, 0, 0, rev(c), 0)),
                  pl.BlockSpec((1, nh, 1, HEAD_DIM, lc), lambda b, c: (b, 0, c, 0, 0)),
                  pl.BlockSpec((1, nh, 1, HEAD_DIM, lc), lambda b, c: (b, 0, rev(c), 0, 0)),
                  pl.BlockSpec((1, lc, LANES), lambda b, c: (b, c, 0)),
                  pl.BlockSpec((1, lc, LANES), lambda b, c: (b, rev(c), 0)),
                  pl.BlockSpec((1, 1, LANES, lc), lambda b, c: (b, c, 0, 0)),
                  pl.BlockSpec((1, 1, LANES, lc), lambda b, c: (b, rev(c), 0, 0))],
        out_specs=[pl.BlockSpec((1, lc, nh * HEAD_DIM), lambda b, c: (b, c, 0)),
                   pl.BlockSpec((1, lc, nh * HEAD_DIM), lambda b, c: (b, rev(c), 0))],
        out_shape=[jax.ShapeDtypeStruct((bsz, t, nh * HEAD_DIM), F32),
                   jax.ShapeDtypeStruct((bsz, t, nh * HEAD_DIM), F32)],
        scratch_shapes=[pltpu.VMEM((2 * nh, HEAD_DIM, HEAD_DIM), F32)],
        compiler_params=_cparams("arbitrary", "arbitrary"),
        name="delta_scan",
    )(xh, xh, kt, kt, gn, gn, gnt4, gnt4)


def kernel(x, c, ctx, c_ctx, ada_w, ada_b, ln1_g, ln2_g, even_w_in, even_conv_w, even_a_log, even_dt_bias, even_out_norm_g, even_q_norm_g, even_k_norm_g, even_w_out, odd_w_in, odd_rpb, odd_w_out, router_w, expert_w_gate, expert_w_up, expert_w_down, final_norm_g):
    bsz, n_lat, d = x.shape
    n_ctx = ctx.shape[1]
    nct = n_ctx // TILE
    h = (ctx, x)
    rows = jnp.zeros((8, d), F32).at[0].set(c_ctx).at[1:1 + bsz].set(c)

    def mod_of(layer):
        m = _adaln(rows, ada_w, ada_b, layer).reshape(8, 6, d)
        return jnp.stack([jnp.broadcast_to(m[0], (bsz, 6, d)), m[1:1 + bsz]], axis=1)

    mod = mod_of(0)
    a_dim = d // 2
    a_heads = a_dim // HEAD_DIM
    w_in = even_w_in[0]
    o_g = 4 * a_dim
    o_bq = o_g + 4 * a_heads
    kv_dim = (w_in.shape[1] - o_bq - a_dim) // 2
    gates_w = jnp.zeros((d, LANES), F32).at[:, :4 * a_heads].set(w_in[:, o_g:o_bq])
    w_perm = jnp.concatenate([w_in[:, :o_g], w_in[:, o_bq:], gates_w], axis=1).astype(BF16)
    p = _modproj(h, mod, ln1_g[0], w_perm, nct, F32)
    xh, kt, gn, gnt = _delta_prep(p, even_conv_w[0], even_a_log[0], even_dt_bias[0], n_ctx=n_ctx, a_dim=a_dim)
    o_delta = _delta_scan(xh, kt, gn, gnt, n_ctx=n_ctx)
    cos, sin = _rope_tables(n_ctx, n_lat)
    qb, kb, vb = _gqa_prep(p, cos, sin, even_q_norm_g[0], even_k_norm_g[0], a_dim=a_dim, kv_dim=kv_dim)
    o_attn = _gqa_attention(qb, kb, vb, n_ctx=n_ctx)
    h = _outproj_even(h, mod, o_delta, p, o_attn, even_out_norm_g[0], even_w_out[0].astype(BF16), nct, a_dim=a_dim)
    h = _moe(h, mod, ln2_g[0], router_w[0], expert_w_gate, expert_w_up, expert_w_down, 0, final_norm_g,
             n_ctx=n_ctx, route_ctx=True, final_norm=False)

    mod = mod_of(1)
    p = _modproj(h, mod, ln1_g[1], odd_w_in[0].astype(BF16), nct, BF16)
    o_na = _natten(p, _natten_bias(odd_rpb[0]), n_ctx=n_ctx, d=d)
    h_lat = _outproj_odd(h, mod, o_na, odd_w_out[0].astype(BF16), nct)
    return _moe(h_lat, mod, ln2_g[1], router_w[1], expert_w_gate, expert_w_up, expert_w_down, 1, final_norm_g,
                n_ctx=0, route_ctx=False, final_norm=True)
```
